```python
import jax, jax.numpy as jnp
from jax import lax
import numpy as np

D_MODEL = 1024
BATCH = 4
SEQ = 4096
DEPTH = 1

HEAD_DIM = 64
N_Q_HEADS = D_MODEL // HEAD_DIM
N_KV_HEADS = max(1, N_Q_HEADS // 8)
GROUP = N_Q_HEADS // N_KV_HEADS
WINDOW = 128
ATTN_BLOCK = 128
ROPE_THETA = 500000.0
ROPE_DIM = HEAD_DIM // 4
Q_WIDTH = N_Q_HEADS * HEAD_DIM
KV_WIDTH = N_KV_HEADS * HEAD_DIM
NEG_INF = -1e30

POOL_WINDOWS = (2, 4, 8, 16)
N_POOL_GROUPS = len(POOL_WINDOWS)
POOL_WIDTH = D_MODEL // 2
POOL_GROUP_WIDTH = POOL_WIDTH // N_POOL_GROUPS

IN_WIDTH = Q_WIDTH + 2 * KV_WIDTH + POOL_WIDTH + 2 * D_MODEL

N_EXPERTS = 32
TOP_K = 4
D_FF = D_MODEL
SWIGLU_ALPHA = 1.702
SWIGLU_LIMIT = 7.0
MOE_BLOCK = 128

RMS_EPS = 1e-5

kernel_name = "hybrid_gated_swa_pool_moe"


def rmsnorm(x, g):
    xf = x.astype(jnp.float32)
    y = xf * lax.rsqrt(jnp.mean(xf * xf, axis=-1, keepdims=True) + RMS_EPS)
    return (y * g.astype(jnp.float32)).astype(x.dtype)


def partial_rope(t, positions):
    half = ROPE_DIM // 2
    inv_freq = jnp.power(jnp.float32(ROPE_THETA), -jnp.arange(half, dtype=jnp.float32) * (2.0 / ROPE_DIM))
    ang = positions.astype(jnp.float32)[..., None] * inv_freq
    cos = jnp.cos(ang)[:, :, None, :]
    sin = jnp.sin(ang)[:, :, None, :]
    tr = t[..., :ROPE_DIM].astype(jnp.float32)
    t1, t2 = tr[..., :half], tr[..., half:]
    rot = jnp.concatenate([t1 * cos - t2 * sin, t2 * cos + t1 * sin], axis=-1).astype(t.dtype)
    return jnp.concatenate([rot, t[..., ROPE_DIM:]], axis=-1)


def sliding_window_gqa(q, k, v, sinks):
    B, S = q.shape[0], q.shape[1]
    NB = S // ATTN_BLOCK
    qb = q.reshape(B, NB, ATTN_BLOCK, N_KV_HEADS, GROUP, HEAD_DIM)

    def band(t):
        tb = t.reshape(B, NB, ATTN_BLOCK, N_KV_HEADS, HEAD_DIM)
        prev = jnp.concatenate([jnp.zeros_like(tb[:, :1]), tb[:, :-1]], axis=1)
        return jnp.concatenate([prev, tb], axis=2)

    kb, vb = band(k), band(v)
    scores = jnp.einsum('bnqhgd,bnkhd->bnhgqk', qb, kb,
                        preferred_element_type=jnp.float32) * (HEAD_DIM ** -0.5)
    qi = jnp.arange(ATTN_BLOCK)[:, None]
    kj = jnp.arange(2 * ATTN_BLOCK)[None, :]
    dist = qi + ATTN_BLOCK - kj
    blk = jnp.arange(NB)[:, None, None]
    valid = (dist >= 0) & (dist < WINDOW) & (blk * ATTN_BLOCK + kj[None] - ATTN_BLOCK >= 0)
    scores = jnp.where(valid[None, :, None, None], scores, NEG_INF)
    sink = jnp.broadcast_to(sinks.astype(jnp.float32).reshape(1, 1, N_KV_HEADS, GROUP, 1, 1),
                            scores.shape[:-1] + (1,))
    probs = jax.nn.softmax(jnp.concatenate([scores, sink], axis=-1), axis=-1)[..., :-1]
    out = jnp.einsum('bnhgqk,bnkhd->bnqhgd', probs.astype(v.dtype), vb)
    return out.reshape(B, S, Q_WIDTH)


def multiscale_causal_pool(u, pool_w, pool_scale):
    B, S = u.shape[0], u.shape[1]
    uf = u.astype(jnp.float32)
    cs0 = jnp.pad(jnp.cumsum(uf, axis=1), ((0, 0), (1, 0), (0, 0)))
    outs = []
    for g, w in enumerate(POOL_WINDOWS):
        sl = slice(g * POOL_GROUP_WIDTH, (g + 1) * POOL_GROUP_WIDTH)
        c = cs0[:, :, sl]
        lower = jnp.pad(c, ((0, 0), (w - 1, 0), (0, 0)))[:, :S]
        total = c[:, 1:] - lower
        count = jnp.minimum(jnp.arange(1, S + 1), w).astype(jnp.float32)[None, :, None]
        outs.append(total / count - uf[:, :, sl])
    pooled = jnp.stack(outs, axis=2)
    mixed = jnp.einsum('bsgc,gcd->bsgd', pooled, pool_w.astype(jnp.float32))
    return (mixed.reshape(B, S, POOL_WIDTH) * pool_scale.astype(jnp.float32)).astype(u.dtype)


def clamped_swiglu(u):
    glu, lin = u[..., :D_FF], u[..., D_FF:]
    glu = jnp.minimum(glu, SWIGLU_LIMIT)
    lin = jnp.clip(lin, -SWIGLU_LIMIT, SWIGLU_LIMIT)
    return glu * jax.nn.sigmoid(SWIGLU_ALPHA * glu) * (lin + 1.0)


def moe_ffn(h, w_router, b_router, w_up, b_up, w_down, b_down):
    T, D = h.shape
    logits = (h @ w_router + b_router).astype(jnp.float32)
    top_vals, top_idx = lax.top_k(logits, TOP_K)
    gates = jax.nn.softmax(top_vals, axis=-1)
    A = T * TOP_K
    e_flat = top_idx.reshape(A).astype(jnp.int32)
    tok_flat = jnp.arange(A, dtype=jnp.int32) // TOP_K
    g_flat = gates.reshape(A)
    order = jnp.argsort(e_flat)
    e_sorted = e_flat[order]
    counts = jnp.bincount(e_flat, length=N_EXPERTS).astype(jnp.int32)
    padded = (counts + MOE_BLOCK - 1) // MOE_BLOCK * MOE_BLOCK
    start = jnp.cumsum(counts) - counts
    pend = jnp.cumsum(padded)
    pstart = pend - padded
    dest = pstart[e_sorted] + jnp.arange(A, dtype=jnp.int32) - start[e_sorted]
    n_slots = (A + MOE_BLOCK - 1) // MOE_BLOCK * MOE_BLOCK + N_EXPERTS * MOE_BLOCK
    n_blocks = n_slots // MOE_BLOCK
    slot_tok = jnp.full((n_slots,), T, jnp.int32).at[dest].set(tok_flat[order])
    slot_gate = jnp.zeros((n_slots,), jnp.float32).at[dest].set(g_flat[order])
    blk_expert = jnp.minimum(
        jnp.searchsorted(pend, jnp.arange(n_blocks, dtype=jnp.int32) * MOE_BLOCK, side='right'),
        N_EXPERTS - 1).astype(jnp.int32)
    h_pad = jnp.concatenate([h, jnp.zeros((1, D), h.dtype)], axis=0)
    xs = h_pad[slot_tok].reshape(n_blocks, MOE_BLOCK, D)

    def expert_block(args):
        xb, e = args
        u = xb @ w_up[e] + b_up[e]
        return clamped_swiglu(u) @ w_down[e] + b_down[e]

    ys = lax.map(expert_block, (xs, blk_expert)).reshape(n_slots, D)
    out = jnp.zeros((T + 1, D), jnp.float32).at[slot_tok].add(
        ys.astype(jnp.float32) * slot_gate[:, None])[:T]
    return out.astype(h.dtype)


def setup_inputs(seed: int = 0) -> dict:
    key = jax.random.key(seed)
    ks = jax.random.split(key, 20)
    L, D = DEPTH, D_MODEL
    nrm = jax.random.normal
    return {
        "x": nrm(ks[0], (BATCH, SEQ, D), jnp.float32),
        "positions": jnp.broadcast_to(jnp.arange(SEQ, dtype=jnp.int32), (BATCH, SEQ)),
        "norm1_g": 1.0 + 0.05 * nrm(ks[1], (L, D), jnp.float32),
        "w_in": nrm(ks[2], (L, D, IN_WIDTH), jnp.float32) * D ** -0.5,
        "b_in": 0.02 * nrm(ks[3], (L, IN_WIDTH), jnp.float32),
        "attn_sinks": nrm(ks[4], (L, N_Q_HEADS), jnp.float32),
        "w_attn_br": nrm(ks[5], (L, Q_WIDTH, D), jnp.float32) * Q_WIDTH ** -0.5,
        "pool_w": nrm(ks[6], (L, N_POOL_GROUPS, POOL_GROUP_WIDTH, POOL_GROUP_WIDTH), jnp.float32) * POOL_GROUP_WIDTH ** -0.5,
        "pool_scale": 1.0 + 0.1 * nrm(ks[7], (L, POOL_WIDTH), jnp.float32),
        "w_pool_br": nrm(ks[8], (L, POOL_WIDTH, D), jnp.float32) * POOL_WIDTH ** -0.5,
        "w_out": nrm(ks[9], (L, D, D), jnp.float32) * D ** -0.5,
        "norm2_g": 1.0 + 0.05 * nrm(ks[10], (L, D), jnp.float32),
        "w_router": nrm(ks[11], (L, D, N_EXPERTS), jnp.float32) * D ** -0.5,
        "b_router": 0.01 * nrm(ks[12], (L, N_EXPERTS), jnp.float32),
        "w_up": nrm(ks[13], (L, N_EXPERTS, D, 2 * D_FF), jnp.float32) * D ** -0.5,
        "b_up": 0.02 * nrm(ks[14], (L, N_EXPERTS, 2 * D_FF), jnp.float32),
        "w_down": nrm(ks[15], (L, N_EXPERTS, D_FF, D), jnp.float32) * D_FF ** -0.5,
        "b_down": 0.02 * nrm(ks[16], (L, N_EXPERTS, D), jnp.float32),
        "normf_g": 1.0 + 0.05 * nrm(ks[17], (D,), jnp.float32),
    }


def reference(x, positions, norm1_g, w_in, b_in, attn_sinks, w_attn_br, pool_w, pool_scale,
              w_pool_br, w_out, norm2_g, w_router, b_router, w_up, b_up, w_down, b_down, normf_g):
    B, S, D = x.shape
    o_q = Q_WIDTH
    o_k = o_q + KV_WIDTH
    o_v = o_k + KV_WIDTH
    o_p = o_v + POOL_WIDTH
    o_ga = o_p + D_MODEL
    for l in range(DEPTH):
        h = rmsnorm(x, norm1_g[l])
        proj = jnp.einsum('bsd,de->bse', h, w_in[l]) + b_in[l]
        q = proj[..., :o_q].reshape(B, S, N_Q_HEADS, HEAD_DIM)
        k = proj[..., o_q:o_k].reshape(B, S, N_KV_HEADS, HEAD_DIM)
        v = proj[..., o_k:o_v].reshape(B, S, N_KV_HEADS, HEAD_DIM)
        u_pool = proj[..., o_v:o_p]
        g_attn = jax.nn.sigmoid(proj[..., o_p:o_ga].astype(jnp.float32))
        g_pool = jax.nn.sigmoid(proj[..., o_ga:].astype(jnp.float32))
        q = partial_rope(q, positions)
        k = partial_rope(k, positions)
        y_attn = sliding_window_gqa(q, k, v, attn_sinks[l]) @ w_attn_br[l]
        y_pool = multiscale_causal_pool(u_pool, pool_w[l], pool_scale[l]) @ w_pool_br[l]
        merged = (g_attn * y_attn.astype(jnp.float32) + g_pool * y_pool.astype(jnp.float32)).astype(x.dtype)
        x = x + merged @ w_out[l]
        h2 = rmsnorm(x, norm2_g[l]).reshape(B * S, D)
        x = x + moe_ffn(h2, w_router[l], b_router[l], w_up[l], b_up[l], w_down[l], b_down[l]).reshape(B, S, D)
    return rmsnorm(x, normf_g)
```

```python
import functools

import numpy as np
import jax
import jax.numpy as jnp
from jax import lax
from jax.experimental import pallas as pl
from jax.experimental.pallas import tpu as pltpu

F32 = jnp.float32
BF16 = jnp.bfloat16

D_MODEL = 1024
HEAD_DIM = 64
N_Q_HEADS = 16
N_KV_HEADS = 2
GROUP = N_Q_HEADS // N_KV_HEADS
WINDOW = 128
ROPE_THETA = 500000.0
ROPE_DIM = HEAD_DIM // 4
ROPE_HALF = ROPE_DIM // 2
Q_WIDTH = N_Q_HEADS * HEAD_DIM
KV_WIDTH = N_KV_HEADS * HEAD_DIM
NEG_INF = -1e30
POOL_WINDOWS = (2, 4, 8, 16)
POOL_WIDTH = D_MODEL // 2
POOL_GROUP_WIDTH = POOL_WIDTH // len(POOL_WINDOWS)
POOL_HALO = 16
IN_WIDTH = Q_WIDTH + 2 * KV_WIDTH + POOL_WIDTH + 2 * D_MODEL
O_K = Q_WIDTH
O_V = O_K + KV_WIDTH
O_P = O_V + KV_WIDTH
O_GA = O_P + POOL_WIDTH
O_GP = O_GA + D_MODEL
N_EXPERTS = 32
TOP_K = 4
D_FF = D_MODEL
SWIGLU_ALPHA = 1.702
SWIGLU_LIMIT = 7.0
RMS_EPS = 1e-5

LANES = 128
TM = 256
TB = 128
BM = 256
ROPE_TN = 2048
MIX_VMEM = 56 * 1024 * 1024
MOE_VMEM = 56 * 1024 * 1024


def _const_spec(shape):
    return pl.BlockSpec(shape, lambda *_: (0,) * len(shape))


def _rope_table_kernel(pos_ref, invf_ref, out_ref):
    ang = invf_ref[...] * pos_ref[...].astype(F32)
    out_ref[0:ROPE_HALF, :] = jnp.cos(ang)
    out_ref[ROPE_HALF:, :] = jnp.sin(ang)


def _rope_table(pos_row, inv_freq):
    t = pos_row.shape[1]
    return pl.pallas_call(
        _rope_table_kernel,
        grid=(t // ROPE_TN,),
        in_specs=[pl.BlockSpec((1, ROPE_TN), lambda i: (0, i)), _const_spec((ROPE_HALF, 1))],
        out_specs=pl.BlockSpec((2 * ROPE_HALF, ROPE_TN), lambda i: (0, i)),
        out_shape=jax.ShapeDtypeStruct((2 * ROPE_HALF, t), F32),
        name="rope_table",
    )(pos_row, inv_freq.reshape(ROPE_HALF, 1))


def _rope_expand_matrix():
    e = np.zeros((LANES, 2 * LANES), np.float32)
    ones = np.zeros((1, LANES), np.float32)
    for l in range(LANES):
        hl = l % HEAD_DIM
        if hl < ROPE_DIM:
            j = hl % ROPE_HALF
            e[j, l] = 1.0
            e[ROPE_HALF + j, LANES + l] = -1.0 if hl < ROPE_HALF else 1.0
        else:
            ones[0, l] = 1.0
    return e, ones


def _rmsnorm(x, g):
    ms = jnp.mean(x * x, axis=-1, keepdims=True)
    return x * lax.rsqrt(ms + RMS_EPS) * g


def _split3(x):
    a = x.astype(BF16)
    r = x - a.astype(F32)
    b = r.astype(BF16)
    c = (r - b.astype(F32)).astype(BF16)
    return a, b, c


def _mixer_kernel(sinks_ref, x_ref, cs_ref, erope_ref, ones_ref, g1_ref, win_ref, bin_ref, wab_ref,
                  pw_ref, ps_ref, wpb_ref, wout_ref, g2_ref, wrh_ref, wrl_ref, br_ref,
                  x1_ref, h2_ref, idx_ref, gate_ref,
                  kprev, vprev, uprev, q_scr, attn_scr, *, blocks_per_seq):
    i = pl.program_id(0)
    blk_in_seq = lax.rem(i, blocks_per_seq)
    seq_start = blk_in_seq == 0

    @pl.when(seq_start)
    def _():
        kprev[...] = jnp.zeros_like(kprev)
        vprev[...] = jnp.zeros_like(vprev)
        uprev[...] = jnp.zeros_like(uprev)

    x = x_ref[...]
    h = _rmsnorm(x, g1_ref[...]).astype(BF16)

    c1, c2, c3 = _split3(cs_ref[...])
    er = erope_ref[...]
    tab = (jnp.dot(c1, er, preferred_element_type=F32) + jnp.dot(c2, er, preferred_element_type=F32)
           + jnp.dot(c3, er, preferred_element_type=F32))
    cos_t = tab[:, :LANES] + ones_ref[...]
    sin_t = tab[:, LANES:]
    lane = lax.broadcasted_iota(jnp.int32, (TM, LANES), 1)
    first_half = (lane & (HEAD_DIM - 1)) < ROPE_HALF

    def rope(t):
        partner = jnp.where(first_half, pltpu.roll(t, LANES - ROPE_HALF, 1), pltpu.roll(t, ROPE_HALF, 1))
        return t * cos_t + partner * sin_t

    q = jnp.dot(h, win_ref[:, 0:Q_WIDTH], preferred_element_type=F32) + bin_ref[:, 0:Q_WIDTH]
    for c in range(Q_WIDTH // LANES):
        qc = rope(q[:, c * LANES:(c + 1) * LANES]) * (HEAD_DIM ** -0.5)
        q_scr[:, c * LANES:(c + 1) * LANES] = qc.astype(BF16)

    kvu = jnp.dot(h, win_ref[:, O_K:O_GA], preferred_element_type=F32) + bin_ref[:, O_K:O_GA]
    k = rope(kvu[:, 0:KV_WIDTH])
    v = kvu[:, KV_WIDTH:2 * KV_WIDTH]
    u = kvu[:, 2 * KV_WIDTH:]

    lane_b = lax.broadcasted_iota(jnp.int32, (TM + WINDOW, LANES), 1)
    low = lane_b < HEAD_DIM

    def head_bands(prev_ref, cur):
        band = jnp.concatenate([prev_ref[...], cur], axis=0)
        swapped = pltpu.roll(band, HEAD_DIM, 1)
        zero = jnp.zeros_like(band)
        a0 = jnp.where(low, band, zero).astype(BF16)
        b0 = jnp.where(low, zero, swapped).astype(BF16)
        a1 = jnp.where(low, swapped, zero).astype(BF16)
        b1 = jnp.where(low, zero, band).astype(BF16)
        return ((a0, b0), (a1, b1))

    kb = head_bands(kprev, k)
    vb = head_bands(vprev, v)
    kprev[...] = k[TM - WINDOW:, :]
    vprev[...] = v[TM - WINDOW:, :]

    qi = lax.broadcasted_iota(jnp.int32, (WINDOW, 2 * WINDOW), 0)
    kj = lax.broadcasted_iota(jnp.int32, (WINDOW, 2 * WINDOW), 1)
    mask_mid = jnp.logical_or(jnp.logical_and(kj < WINDOW, kj > qi),
                              jnp.logical_and(kj >= WINDOW, (kj - WINDOW) <= qi))
    first_key = jnp.where(seq_start, WINDOW, 0)
    mask_first = jnp.logical_and(mask_mid, kj >= first_key)

    low_w = lax.broadcasted_iota(jnp.int32, (WINDOW, LANES), 1) < HEAD_DIM
    n_chunks = GROUP // 2
    for j in range(TM // WINDOW):
        mask = mask_first if j == 0 else mask_mid
        rows = slice(j * WINDOW, (j + 1) * WINDOW)
        band_rows = slice(j * WINDOW, j * WINDOW + 2 * WINDOW)
        for hkv in range(N_KV_HEADS):
            q4 = jnp.concatenate(
                [q_scr[rows, (hkv * n_chunks + c) * LANES:(hkv * n_chunks + c + 1) * LANES]
                 for c in range(n_chunks)], axis=0)
            kk = jnp.concatenate([kb[hkv][0][band_rows], kb[hkv][1][band_rows]], axis=0)
            vv = jnp.concatenate([vb[hkv][0][band_rows], vb[hkv][1][band_rows]], axis=0)
            s = lax.dot_general(q4, kk, (((1,), (1,)), ((), ())), preferred_element_type=F32)
            p_rows = []
            rden = []
            for c in range(n_chunks):
                p_cols = []
                rd = []
                for par in range(2):
                    sink = sinks_ref[hkv * GROUP + 2 * c + par]
                    sc = s[c * WINDOW:(c + 1) * WINDOW, par * 2 * WINDOW:(par + 1) * 2 * WINDOW]
                    sc = jnp.where(mask, sc, NEG_INF)
                    m = jnp.maximum(jnp.max(sc, axis=-1, keepdims=True), sink)
                    e = jnp.exp(sc - m)
                    den = jnp.sum(e, axis=-1, keepdims=True) + jnp.exp(sink - m)
                    p_cols.append(e.astype(BF16))
                    rd.append(1.0 / den)
                p_rows.append(jnp.concatenate(p_cols, axis=1))
                rden.append(rd)
            p = jnp.concatenate(p_rows, axis=0)
            o = jnp.dot(p, vv, preferred_element_type=F32)
            for c in range(n_chunks):
                scale = jnp.where(low_w, rden[c][0], rden[c][1])
                oc = o[c * WINDOW:(c + 1) * WINDOW, :] * scale
                col = (hkv * n_chunks + c) * LANES
                attn_scr[rows, col:col + LANES] = oc.astype(BF16)

    y_attn = jnp.dot(attn_scr[...], wab_ref[...], preferred_element_type=F32)

    ext = jnp.concatenate([uprev[...], u], axis=0)
    uprev[...] = u[TM - POOL_HALO:, :]
    pos_in_seq = blk_in_seq * TM + lax.broadcasted_iota(jnp.int32, (TM, 1), 0)
    mixed = []
    for g, w in enumerate(POOL_WINDOWS):
        cols = slice(g * POOL_GROUP_WIDTH, (g + 1) * POOL_GROUP_WIDTH)
        sg = ext[:, cols]
        step = 1
        while step < w:
            sg = sg + pltpu.roll(sg, step, 0)
            step *= 2
        count = jnp.minimum(pos_in_seq + 1, w).astype(F32)
        pooled = sg[POOL_HALO:, :] / count - u[:, cols]
        mg = jnp.dot(pooled.astype(BF16), pw_ref[g], preferred_element_type=F32)
        mixed.append((mg * ps_ref[:, cols]).astype(BF16))
    y_pool = jnp.dot(jnp.concatenate(mixed, axis=1), wpb_ref[...], preferred_element_type=F32)

    g_attn = jax.nn.sigmoid(jnp.dot(h, win_ref[:, O_GA:O_GP], preferred_element_type=F32)
                            + bin_ref[:, O_GA:O_GP])
    g_pool = jax.nn.sigmoid(jnp.dot(h, win_ref[:, O_GP:], preferred_element_type=F32)
                            + bin_ref[:, O_GP:])
    merged = (g_attn * y_attn + g_pool * y_pool).astype(BF16)
    x1 = x + jnp.dot(merged, wout_ref[...], preferred_element_type=F32)
    x1_ref[...] = x1

    h2 = _rmsnorm(x1, g2_ref[...])
    h2_ref[...] = h2

    ha = h2.astype(BF16)
    hb = (h2 - ha.astype(F32)).astype(BF16)
    wrh = wrh_ref[...]
    logits = (jnp.dot(ha, wrh, preferred_element_type=F32) + jnp.dot(ha, wrl_ref[...], preferred_element_type=F32)
              + jnp.dot(hb, wrh, preferred_element_type=F32) + br_ref[...])
    e_iota = lax.broadcasted_iota(jnp.int32, (TM, N_EXPERTS), 1)
    vals, idxs = [], []
    work = logits
    for _ in range(TOP_K):
        m = jnp.max(work, axis=-1, keepdims=True)
        ix = jnp.min(jnp.where(work == m, e_iota, N_EXPERTS), axis=-1, keepdims=True)
        vals.append(m)
        idxs.append(ix)
        work = jnp.where(e_iota == ix, -jnp.inf, work)
    exps = [jnp.exp(vk - vals[0]) for vk in vals]
    den = exps[0] + exps[1] + exps[2] + exps[3]
    idx_out = jnp.zeros((TM, LANES), jnp.int32)
    gate_out = jnp.zeros((TM, LANES), F32)
    for kk_ in range(TOP_K):
        idx_out = jnp.where(lane == kk_, idxs[kk_], idx_out)
        gate_out = jnp.where(lane == kk_, exps[kk_] / den, gate_out)
    idx_ref[...] = idx_out
    gate_ref[...] = gate_out


def _mixer(x2d, cs, sinks, g1, w_in, b_in, w_ab, pool_w, pool_scale, w_pb, w_out, g2, w_r, b_r, seq_len):
    t = x2d.shape[0]
    erope, ones = _rope_expand_matrix()
    w_rh = w_r.astype(BF16)
    w_rl = (w_r - w_rh.astype(F32)).astype(BF16)
    row = lambda i: (i, 0)
    in_specs = [
        pl.BlockSpec(memory_space=pltpu.SMEM),
        pl.BlockSpec((TM, D_MODEL), row),
        pl.BlockSpec((TM, LANES), row),
        _const_spec((LANES, 2 * LANES)),
        _const_spec((1, LANES)),
        _const_spec((1, D_MODEL)),
        _const_spec((D_MODEL, IN_WIDTH)),
        _const_spec((1, IN_WIDTH)),
        _const_spec((Q_WIDTH, D_MODEL)),
        _const_spec((len(POOL_WINDOWS), POOL_GROUP_WIDTH, POOL_GROUP_WIDTH)),
        _const_spec((1, POOL_WIDTH)),
        _const_spec((POOL_WIDTH, D_MODEL)),
        _const_spec((D_MODEL, D_MODEL)),
        _const_spec((1, D_MODEL)),
        _const_spec((D_MODEL, N_EXPERTS)),
        _const_spec((D_MODEL, N_EXPERTS)),
        _const_spec((1, N_EXPERTS)),
    ]
    out_specs = [pl.BlockSpec((TM, D_MODEL), row), pl.BlockSpec((TM, D_MODEL), row),
                 pl.BlockSpec((TM, LANES), row), pl.BlockSpec((TM, LANES), row)]
    out_shape = [jax.ShapeDtypeStruct((t, D_MODEL), F32), jax.ShapeDtypeStruct((t, D_MODEL), F32),
                 jax.ShapeDtypeStruct((t, LANES), jnp.int32), jax.ShapeDtypeStruct((t, LANES), F32)]
    scratch = [pltpu.VMEM((WINDOW, KV_WIDTH), F32), pltpu.VMEM((WINDOW, KV_WIDTH), F32),
               pltpu.VMEM((POOL_HALO, POOL_WIDTH), F32),
               pltpu.VMEM((TM, Q_WIDTH), BF16), pltpu.VMEM((TM, Q_WIDTH), BF16)]
    return pl.pallas_call(
        functools.partial(_mixer_kernel, blocks_per_seq=seq_len // TM),
        grid=(t // TM,),
        in_specs=in_specs, out_specs=out_specs, out_shape=out_shape, scratch_shapes=scratch,
        compiler_params=pltpu.CompilerParams(dimension_semantics=("arbitrary",), vmem_limit_bytes=MIX_VMEM),
        name="mixer",
    )(sinks, x2d, cs, jnp.asarray(erope, BF16), jnp.asarray(ones), g1.reshape(1, -1), w_in.astype(BF16),
      b_in.reshape(1, -1), w_ab.astype(BF16), pool_w.astype(BF16), pool_scale.reshape(1, -1),
      w_pb.astype(BF16), w_out.astype(BF16), g2.reshape(1, -1), w_rh, w_rl, b_r.reshape(1, -1))


def _row_copy(src, src_row, dst, dst_row, sem):
    return pltpu.make_async_copy(src.at[pl.ds(src_row, 1)], dst.at[pl.ds(dst_row, 1)], sem)


def _dispatch_kernel(dest_ref, h2_ref, xs_in_ref, xs_ref, sem):
    del xs_in_ref
    for r in range(TB):
        for k in range(TOP_K):
            _row_copy(h2_ref, r, xs_ref, dest_ref[0, 0, r * TOP_K + k], sem).start()
    for r in range(TB):
        for k in range(TOP_K):
            _row_copy(h2_ref, r, xs_ref, dest_ref[0, 0, r * TOP_K + k], sem).wait()


def _dispatch(h2, dest_blocks, n_slots):
    t = h2.shape[0]
    xs0 = jnp.zeros((n_slots, D_MODEL), F32)
    return pl.pallas_call(
        _dispatch_kernel,
        grid=(t // TB,),
        in_specs=[pl.BlockSpec((1, 1, TB * TOP_K), lambda i: (i, 0, 0), memory_space=pltpu.SMEM),
                  pl.BlockSpec((TB, D_MODEL), lambda i: (i, 0)),
                  pl.BlockSpec(memory_space=pl.ANY)],
        out_specs=pl.BlockSpec(memory_space=pl.ANY),
        out_shape=jax.ShapeDtypeStruct((n_slots, D_MODEL), F32),
        scratch_shapes=[pltpu.SemaphoreType.DMA(())],
        input_output_aliases={2: 0},
        compiler_params=pltpu.CompilerParams(dimension_semantics=("arbitrary",), has_side_effects=True),
        name="dispatch",
    )(dest_blocks, h2, xs0)


def _experts_kernel(be_ref, first_ref, valid_ref, xs_ref, wup_ref, bup_ref, wdn_ref, bdn_ref, ys_ref,
                    wup_bf, wdn_bf):
    del be_ref
    i = pl.program_id(0)
    cast_rows = 128

    @pl.when(first_ref[i] == 1)
    def _():
        def cast_up(r, carry):
            rows = pl.ds(pl.multiple_of(r * cast_rows, cast_rows), cast_rows)
            wup_bf[rows, :] = wup_ref[0, rows, :].astype(BF16)
            return carry

        def cast_dn(r, carry):
            rows = pl.ds(pl.multiple_of(r * cast_rows, cast_rows), cast_rows)
            wdn_bf[rows, :] = wdn_ref[0, rows, :].astype(BF16)
            return carry

        lax.fori_loop(0, D_MODEL // cast_rows, cast_up, 0)
        lax.fori_loop(0, D_FF // cast_rows, cast_dn, 0)

    @pl.when(valid_ref[i] == 1)
    def _():
        xb = xs_ref[...].astype(BF16)
        up = jnp.dot(xb, wup_bf[...], preferred_element_type=F32) + bup_ref[0]
        glu = jnp.minimum(up[:, :D_FF], SWIGLU_LIMIT)
        lin = jnp.clip(up[:, D_FF:], -SWIGLU_LIMIT, SWIGLU_LIMIT)
        act = glu * jax.nn.sigmoid(SWIGLU_ALPHA * glu) * (lin + 1.0)
        ys_ref[...] = jnp.dot(act.astype(BF16), wdn_bf[...], preferred_element_type=F32) + bdn_ref[0]

    @pl.when(valid_ref[i] == 0)
    def _():
        ys_ref[...] = jnp.zeros_like(ys_ref)


def _experts(xs, blk_expert, blk_first, blk_valid, w_up, b_up, w_down, b_down):
    n_slots = xs.shape[0]
    grid_spec = pltpu.PrefetchScalarGridSpec(
        num_scalar_prefetch=3,
        grid=(n_slots // BM,),
        in_specs=[
            pl.BlockSpec((BM, D_MODEL), lambda i, be, fi, va: (i, 0)),
            pl.BlockSpec((1, D_MODEL, 2 * D_FF), lambda i, be, fi, va: (be[i], 0, 0)),
            pl.BlockSpec((1, 1, 2 * D_FF), lambda i, be, fi, va: (be[i], 0, 0)),
            pl.BlockSpec((1, D_FF, D_MODEL), lambda i, be, fi, va: (be[i], 0, 0)),
            pl.BlockSpec((1, 1, D_MODEL), lambda i, be, fi, va: (be[i], 0, 0)),
        ],
        out_specs=pl.BlockSpec((BM, D_MODEL), lambda i, be, fi, va: (i, 0)),
        scratch_shapes=[pltpu.VMEM((D_MODEL, 2 * D_FF), BF16), pltpu.VMEM((D_FF, D_MODEL), BF16)],
    )
    return pl.pallas_call(
        _experts_kernel,
        grid_spec=grid_spec,
        out_shape=jax.ShapeDtypeStruct((n_slots, D_MODEL), F32),
        compiler_params=pltpu.CompilerParams(dimension_semantics=("arbitrary",), vmem_limit_bytes=MOE_VMEM),
        name="experts",
    )(blk_expert, blk_first, blk_valid, xs, w_up, b_up.reshape(N_EXPERTS, 1, -1), w_down,
      b_down.reshape(N_EXPERTS, 1, -1))


def _combine_kernel(dest_ref, x1_ref, gate_ref, gf_ref, ys_ref, out_ref, buf, sem):
    for r in range(TB):
        for k in range(TOP_K):
            _row_copy(ys_ref, dest_ref[0, 0, r * TOP_K + k], buf.at[k], r, sem).start()
    for r in range(TB):
        for k in range(TOP_K):
            _row_copy(ys_ref, dest_ref[0, 0, r * TOP_K + k], buf.at[k], r, sem).wait()
    acc = x1_ref[...]
    for k in range(TOP_K):
        acc = acc + gate_ref[:, k:k + 1] * buf[k]
    out_ref[...] = _rmsnorm(acc, gf_ref[...])


def _combine(x1, gates, ys, dest_blocks, gf):
    t = x1.shape[0]
    return pl.pallas_call(
        _combine_kernel,
        grid=(t // TB,),
        in_specs=[pl.BlockSpec((1, 1, TB * TOP_K), lambda i: (i, 0, 0), memory_space=pltpu.SMEM),
                  pl.BlockSpec((TB, D_MODEL), lambda i: (i, 0)),
                  pl.BlockSpec((TB, LANES), lambda i: (i, 0)),
                  _const_spec((1, D_MODEL)),
                  pl.BlockSpec(memory_space=pl.ANY)],
        out_specs=pl.BlockSpec((TB, D_MODEL), lambda i: (i, 0)),
        out_shape=jax.ShapeDtypeStruct((t, D_MODEL), F32),
        scratch_shapes=[pltpu.VMEM((TOP_K, TB, D_MODEL), F32), pltpu.SemaphoreType.DMA(())],
        compiler_params=pltpu.CompilerParams(dimension_semantics=("arbitrary",)),
        name="combine",
    )(dest_blocks, x1, gates, gf.reshape(1, -1), ys)


def _routing(top_idx):
    t = top_idx.shape[0]
    n_slots = t * TOP_K + N_EXPERTS * BM
    n_blocks = n_slots // BM
    picked = jnp.sum(top_idx[:, :, None] == jnp.arange(N_EXPERTS, dtype=jnp.int32)[None, None, :], axis=1,
                     dtype=jnp.int32)
    before = jnp.cumsum(picked, axis=0) - picked
    counts = before[-1] + picked[-1]
    padded = (counts + BM - 1) // BM * BM
    pend = jnp.cumsum(padded)
    pstart = pend - padded
    dest = pstart[top_idx] + jnp.take_along_axis(before, top_idx, axis=1)
    blk_start = jnp.arange(n_blocks, dtype=jnp.int32) * BM
    blk_expert = jnp.minimum(jnp.searchsorted(pend, blk_start, side="right"), N_EXPERTS - 1).astype(jnp.int32)
    blk_valid = (blk_start < pend[-1]).astype(jnp.int32)
    changed = jnp.concatenate([jnp.ones((1,), jnp.int32), (blk_expert[1:] != blk_expert[:-1]).astype(jnp.int32)])
    blk_first = changed * blk_valid
    return dest.astype(jnp.int32), blk_expert, blk_first, blk_valid, n_slots


def kernel(x, positions, norm1_g, w_in, b_in, attn_sinks, w_attn_br, pool_w, pool_scale, w_pool_br, w_out,
           norm2_g, w_router, b_router, w_up, b_up, w_down, b_down, normf_g):
    b, s, d = x.shape
    t = b * s
    assert norm1_g.shape[0] == 1, "single-layer problem: the combine kernel applies the final RMSNorm"
    assert d == D_MODEL and s % TM == 0 and t % ROPE_TN == 0
    inv_freq = jnp.power(jnp.float32(ROPE_THETA), -jnp.arange(ROPE_HALF, dtype=F32) * (2.0 / ROPE_DIM))
    cs = _rope_table(positions.reshape(1, t), inv_freq).T
    cs = jnp.pad(cs, ((0, 0), (0, LANES - 2 * ROPE_HALF)))
    x1, h2, idx_pad, gate_pad = _mixer(
        x.reshape(t, d), cs, attn_sinks[0], norm1_g[0], w_in[0], b_in[0], w_attn_br[0], pool_w[0],
        pool_scale[0], w_pool_br[0], w_out[0], norm2_g[0], w_router[0], b_router[0], s)
    dest, blk_expert, blk_first, blk_valid, n_slots = _routing(idx_pad[:, :TOP_K])
    dest_blocks = dest.reshape(t // TB, 1, TB * TOP_K)
    xs = _dispatch(h2, dest_blocks, n_slots)
    ys = _experts(xs, blk_expert, blk_first, blk_valid, w_up[0], b_up[0], w_down[0], b_down[0])
    out = _combine(x1, gate_pad, ys, dest_blocks, normf_g)
    return out.reshape(b, s, d)
```

```python
import functools

import numpy as np
import jax
import jax.numpy as jnp
from jax import lax
from jax.experimental import pallas as pl
from jax.experimental.pallas import tpu as pltpu

F32 = jnp.float32
BF16 = jnp.bfloat16

D_MODEL = 1024
HEAD_DIM = 64
N_Q_HEADS = 16
N_KV_HEADS = 2
GROUP = N_Q_HEADS // N_KV_HEADS
WINDOW = 128
ROPE_THETA = 500000.0
ROPE_DIM = HEAD_DIM // 4
ROPE_HALF = ROPE_DIM // 2
Q_WIDTH = N_Q_HEADS * HEAD_DIM
KV_WIDTH = N_KV_HEADS * HEAD_DIM
NEG_INF = -1e30
POOL_WINDOWS = (2, 4, 8, 16)
POOL_WIDTH = D_MODEL // 2
POOL_GROUP_WIDTH = POOL_WIDTH // len(POOL_WINDOWS)
POOL_HALO = 16
IN_WIDTH = Q_WIDTH + 2 * KV_WIDTH + POOL_WIDTH + 2 * D_MODEL
O_K = Q_WIDTH
O_V = O_K + KV_WIDTH
O_P = O_V + KV_WIDTH
O_GA = O_P + POOL_WIDTH
O_GP = O_GA + D_MODEL
N_EXPERTS = 32
TOP_K = 4
D_FF = D_MODEL
SWIGLU_ALPHA = 1.702
SWIGLU_LIMIT = 7.0
RMS_EPS = 1e-5

LANES = 128
ROW_TILE = D_MODEL // LANES
TM = 256
TD = 512
TB = 128
BM = 256
ROPE_TN = 2048
MIX_VMEM = 56 * 1024 * 1024
MOE_VMEM = 56 * 1024 * 1024


def _const_spec(shape):
    return pl.BlockSpec(shape, lambda *_: (0,) * len(shape))


def _rope_table_kernel(pos_ref, invf_ref, out_ref):
    ang = invf_ref[...] * pos_ref[...].astype(F32)
    out_ref[0:ROPE_HALF, :] = jnp.cos(ang)
    out_ref[ROPE_HALF:, :] = jnp.sin(ang)


def _rope_table(pos_row, inv_freq):
    t = pos_row.shape[1]
    return pl.pallas_call(
        _rope_table_kernel,
        grid=(t // ROPE_TN,),
        in_specs=[pl.BlockSpec((1, ROPE_TN), lambda i: (0, i)), _const_spec((ROPE_HALF, 1))],
        out_specs=pl.BlockSpec((2 * ROPE_HALF, ROPE_TN), lambda i: (0, i)),
        out_shape=jax.ShapeDtypeStruct((2 * ROPE_HALF, t), F32),
        name="rope_table",
    )(pos_row, inv_freq.reshape(ROPE_HALF, 1))


def _rope_expand_matrix():
    e = np.zeros((LANES, 2 * LANES), np.float32)
    ones = np.zeros((1, LANES), np.float32)
    for l in range(LANES):
        hl = l % HEAD_DIM
        if hl < ROPE_DIM:
            j = hl % ROPE_HALF
            e[j, l] = 1.0
            e[ROPE_HALF + j, LANES + l] = -1.0 if hl < ROPE_HALF else 1.0
        else:
            ones[0, l] = 1.0
    return e, ones


def _rmsnorm(x, g):
    ms = jnp.mean(x * x, axis=-1, keepdims=True)
    return x * lax.rsqrt(ms + RMS_EPS) * g


def _split3(x):
    a = x.astype(BF16)
    r = x - a.astype(F32)
    b = r.astype(BF16)
    c = (r - b.astype(F32)).astype(BF16)
    return a, b, c


def _mixer_kernel(sinks_ref, x_ref, cs_ref, erope_ref, ones_ref, g1_ref, win_ref, bin_ref, wab_ref,
                  pw_ref, ps_ref, wpb_ref, wout_ref, g2_ref, wrh_ref, wrl_ref, br_ref,
                  x1_ref, h2_ref, idx_ref, gate_ref, cnt_ref,
                  kprev, vprev, uprev, q_scr, attn_scr, *, blocks_per_seq):
    i = pl.program_id(0)
    blk_in_seq = lax.rem(i, blocks_per_seq)
    seq_start = blk_in_seq == 0

    @pl.when(seq_start)
    def _():
        kprev[...] = jnp.zeros_like(kprev)
        vprev[...] = jnp.zeros_like(vprev)
        uprev[...] = jnp.zeros_like(uprev)

    x = x_ref[...]
    h = _rmsnorm(x, g1_ref[...]).astype(BF16)

    c1, c2, c3 = _split3(cs_ref[...])
    er = erope_ref[...]
    tab = (jnp.dot(c1, er, preferred_element_type=F32) + jnp.dot(c2, er, preferred_element_type=F32)
           + jnp.dot(c3, er, preferred_element_type=F32))
    cos_t = tab[:, :LANES] + ones_ref[...]
    sin_t = tab[:, LANES:]
    lane = lax.broadcasted_iota(jnp.int32, (TM, LANES), 1)
    first_half = (lane & (HEAD_DIM - 1)) < ROPE_HALF

    def rope(t):
        partner = jnp.where(first_half, pltpu.roll(t, LANES - ROPE_HALF, 1), pltpu.roll(t, ROPE_HALF, 1))
        return t * cos_t + partner * sin_t

    q = jnp.dot(h, win_ref[:, 0:Q_WIDTH], preferred_element_type=F32) + bin_ref[:, 0:Q_WIDTH]
    for c in range(Q_WIDTH // LANES):
        qc = rope(q[:, c * LANES:(c + 1) * LANES]) * (HEAD_DIM ** -0.5)
        q_scr[:, c * LANES:(c + 1) * LANES] = qc.astype(BF16)

    kvu = jnp.dot(h, win_ref[:, O_K:O_GA], preferred_element_type=F32) + bin_ref[:, O_K:O_GA]
    k = rope(kvu[:, 0:KV_WIDTH])
    v = kvu[:, KV_WIDTH:2 * KV_WIDTH]
    u = kvu[:, 2 * KV_WIDTH:]

    lane_b = lax.broadcasted_iota(jnp.int32, (TM + WINDOW, LANES), 1)
    low = lane_b < HEAD_DIM

    def head_bands(prev_ref, cur):
        band = jnp.concatenate([prev_ref[...], cur], axis=0)
        swapped = pltpu.roll(band, HEAD_DIM, 1)
        zero = jnp.zeros_like(band)
        a0 = jnp.where(low, band, zero).astype(BF16)
        b0 = jnp.where(low, zero, swapped).astype(BF16)
        a1 = jnp.where(low, swapped, zero).astype(BF16)
        b1 = jnp.where(low, zero, band).astype(BF16)
        return ((a0, b0), (a1, b1))

    kb = head_bands(kprev, k)
    vb = head_bands(vprev, v)
    kprev[...] = k[TM - WINDOW:, :]
    vprev[...] = v[TM - WINDOW:, :]

    qi = lax.broadcasted_iota(jnp.int32, (WINDOW, 2 * WINDOW), 0)
    kj = lax.broadcasted_iota(jnp.int32, (WINDOW, 2 * WINDOW), 1)
    mask_mid = jnp.logical_or(jnp.logical_and(kj < WINDOW, kj > qi),
                              jnp.logical_and(kj >= WINDOW, (kj - WINDOW) <= qi))
    first_key = jnp.where(seq_start, WINDOW, 0)
    mask_first = jnp.logical_and(mask_mid, kj >= first_key)

    low_w = lax.broadcasted_iota(jnp.int32, (WINDOW, LANES), 1) < HEAD_DIM
    n_chunks = GROUP // 2
    for j in range(TM // WINDOW):
        mask = mask_first if j == 0 else mask_mid
        rows = slice(j * WINDOW, (j + 1) * WINDOW)
        band_rows = slice(j * WINDOW, j * WINDOW + 2 * WINDOW)
        for hkv in range(N_KV_HEADS):
            q4 = jnp.concatenate(
                [q_scr[rows, (hkv * n_chunks + c) * LANES:(hkv * n_chunks + c + 1) * LANES]
                 for c in range(n_chunks)], axis=0)
            kk = jnp.concatenate([kb[hkv][0][band_rows], kb[hkv][1][band_rows]], axis=0)
            vv = jnp.concatenate([vb[hkv][0][band_rows], vb[hkv][1][band_rows]], axis=0)
            s = lax.dot_general(q4, kk, (((1,), (1,)), ((), ())), preferred_element_type=F32)
            p_rows = []
            rden = []
            for c in range(n_chunks):
                p_cols = []
                rd = []
                for par in range(2):
                    sink = sinks_ref[hkv * GROUP + 2 * c + par]
                    sc = s[c * WINDOW:(c + 1) * WINDOW, par * 2 * WINDOW:(par + 1) * 2 * WINDOW]
                    sc = jnp.where(mask, sc, NEG_INF)
                    m = jnp.maximum(jnp.max(sc, axis=-1, keepdims=True), sink)
                    e = jnp.exp(sc - m)
                    den = jnp.sum(e, axis=-1, keepdims=True) + jnp.exp(sink - m)
                    p_cols.append(e.astype(BF16))
                    rd.append(1.0 / den)
                p_rows.append(jnp.concatenate(p_cols, axis=1))
                rden.append(rd)
            p = jnp.concatenate(p_rows, axis=0)
            o = jnp.dot(p, vv, preferred_element_type=F32)
            for c in range(n_chunks):
                scale = jnp.where(low_w, rden[c][0], rden[c][1])
                oc = o[c * WINDOW:(c + 1) * WINDOW, :] * scale
                col = (hkv * n_chunks + c) * LANES
                attn_scr[rows, col:col + LANES] = oc.astype(BF16)

    y_attn = jnp.dot(attn_scr[...], wab_ref[...], preferred_element_type=F32)

    ext = jnp.concatenate([uprev[...], u], axis=0)
    uprev[...] = u[TM - POOL_HALO:, :]
    pos_in_seq = blk_in_seq * TM + lax.broadcasted_iota(jnp.int32, (TM, 1), 0)
    mixed = []
    for g, w in enumerate(POOL_WINDOWS):
        cols = slice(g * POOL_GROUP_WIDTH, (g + 1) * POOL_GROUP_WIDTH)
        sg = ext[:, cols]
        step = 1
        while step < w:
            sg = sg + pltpu.roll(sg, step, 0)
            step *= 2
        count = jnp.minimum(pos_in_seq + 1, w).astype(F32)
        pooled = sg[POOL_HALO:, :] / count - u[:, cols]
        mg = jnp.dot(pooled.astype(BF16), pw_ref[g], preferred_element_type=F32)
        mixed.append((mg * ps_ref[:, cols]).astype(BF16))
    y_pool = jnp.dot(jnp.concatenate(mixed, axis=1), wpb_ref[...], preferred_element_type=F32)

    g_attn = jax.nn.sigmoid(jnp.dot(h, win_ref[:, O_GA:O_GP], preferred_element_type=F32)
                            + bin_ref[:, O_GA:O_GP])
    g_pool = jax.nn.sigmoid(jnp.dot(h, win_ref[:, O_GP:], preferred_element_type=F32)
                            + bin_ref[:, O_GP:])
    merged = (g_attn * y_attn + g_pool * y_pool).astype(BF16)
    x1 = x + jnp.dot(merged, wout_ref[...], preferred_element_type=F32)
    x1_ref[...] = x1

    h2 = _rmsnorm(x1, g2_ref[...])
    for c in range(D_MODEL // LANES):
        h2_ref[:, c, :] = h2[:, c * LANES:(c + 1) * LANES]

    ha = h2.astype(BF16)
    hb = (h2 - ha.astype(F32)).astype(BF16)
    wrh = wrh_ref[...]
    logits = (jnp.dot(ha, wrh, preferred_element_type=F32) + jnp.dot(ha, wrl_ref[...], preferred_element_type=F32)
              + jnp.dot(hb, wrh, preferred_element_type=F32) + br_ref[...])
    e_iota = lax.broadcasted_iota(jnp.int32, (TM, N_EXPERTS), 1)
    vals, idxs = [], []
    work = logits
    for _ in range(TOP_K):
        m = jnp.max(work, axis=-1, keepdims=True)
        ix = jnp.min(jnp.where(work == m, e_iota, N_EXPERTS), axis=-1, keepdims=True)
        vals.append(m)
        idxs.append(ix)
        work = jnp.where(e_iota == ix, -jnp.inf, work)
    exps = [jnp.exp(vk - vals[0]) for vk in vals]
    den = exps[0] + exps[1] + exps[2] + exps[3]

    hots = [e_iota == ix for ix in idxs]
    picked = jnp.zeros((TM, N_EXPERTS), F32)
    for hot in hots:
        picked = jnp.where(hot, 1.0, picked)
    t_row = lax.broadcasted_iota(jnp.int32, (TM, TM), 0)
    t_col = lax.broadcasted_iota(jnp.int32, (TM, TM), 1)
    earlier = jnp.where(t_col < t_row, 1.0, 0.0).astype(BF16)
    before = jnp.dot(earlier, picked.astype(BF16), preferred_element_type=F32)
    cnt_ref[0] = jnp.sum(picked, axis=0, keepdims=True).astype(jnp.int32)

    idx_out = jnp.zeros((TM, LANES), jnp.int32)
    gate_out = jnp.zeros((TM, LANES), F32)
    for kk_ in range(TOP_K):
        local_rank = jnp.sum(jnp.where(hots[kk_], before, 0.0), axis=-1, keepdims=True).astype(jnp.int32)
        idx_out = jnp.where(lane == kk_, idxs[kk_], idx_out)
        idx_out = jnp.where(lane == TOP_K + kk_, local_rank, idx_out)
        gate_out = jnp.where(lane == kk_, exps[kk_] / den, gate_out)
    idx_ref[...] = idx_out
    gate_ref[...] = gate_out


def _mixer(x2d, cs, sinks, g1, w_in, b_in, w_ab, pool_w, pool_scale, w_pb, w_out, g2, w_r, b_r, seq_len):
    t = x2d.shape[0]
    erope, ones = _rope_expand_matrix()
    w_rh = w_r.astype(BF16)
    w_rl = (w_r - w_rh.astype(F32)).astype(BF16)
    row = lambda i: (i, 0)
    in_specs = [
        pl.BlockSpec(memory_space=pltpu.SMEM),
        pl.BlockSpec((TM, D_MODEL), row),
        pl.BlockSpec((TM, LANES), row),
        _const_spec((LANES, 2 * LANES)),
        _const_spec((1, LANES)),
        _const_spec((1, D_MODEL)),
        _const_spec((D_MODEL, IN_WIDTH)),
        _const_spec((1, IN_WIDTH)),
        _const_spec((Q_WIDTH, D_MODEL)),
        _const_spec((len(POOL_WINDOWS), POOL_GROUP_WIDTH, POOL_GROUP_WIDTH)),
        _const_spec((1, POOL_WIDTH)),
        _const_spec((POOL_WIDTH, D_MODEL)),
        _const_spec((D_MODEL, D_MODEL)),
        _const_spec((1, D_MODEL)),
        _const_spec((D_MODEL, N_EXPERTS)),
        _const_spec((D_MODEL, N_EXPERTS)),
        _const_spec((1, N_EXPERTS)),
    ]
    tile3 = lambda i: (i, 0, 0)
    out_specs = [pl.BlockSpec((TM, D_MODEL), row), pl.BlockSpec((TM, ROW_TILE, LANES), tile3),
                 pl.BlockSpec((TM, LANES), row), pl.BlockSpec((TM, LANES), row),
                 pl.BlockSpec((1, 1, N_EXPERTS), tile3)]
    out_shape = [jax.ShapeDtypeStruct((t, D_MODEL), F32), jax.ShapeDtypeStruct((t, ROW_TILE, LANES), F32),
                 jax.ShapeDtypeStruct((t, LANES), jnp.int32), jax.ShapeDtypeStruct((t, LANES), F32),
                 jax.ShapeDtypeStruct((t // TM, 1, N_EXPERTS), jnp.int32)]
    scratch = [pltpu.VMEM((WINDOW, KV_WIDTH), F32), pltpu.VMEM((WINDOW, KV_WIDTH), F32),
               pltpu.VMEM((POOL_HALO, POOL_WIDTH), F32),
               pltpu.VMEM((TM, Q_WIDTH), BF16), pltpu.VMEM((TM, Q_WIDTH), BF16)]
    return pl.pallas_call(
        functools.partial(_mixer_kernel, blocks_per_seq=seq_len // TM),
        grid=(t // TM,),
        in_specs=in_specs, out_specs=out_specs, out_shape=out_shape, scratch_shapes=scratch,
        compiler_params=pltpu.CompilerParams(dimension_semantics=("arbitrary",), vmem_limit_bytes=MIX_VMEM),
        name="mixer",
    )(sinks, x2d, cs, jnp.asarray(erope, BF16), jnp.asarray(ones), g1.reshape(1, -1), w_in.astype(BF16),
      b_in.reshape(1, -1), w_ab.astype(BF16), pool_w.astype(BF16), pool_scale.reshape(1, -1),
      w_pb.astype(BF16), w_out.astype(BF16), g2.reshape(1, -1), w_rh, w_rl, b_r.reshape(1, -1))


def _tile_copy(src, src_row, dst, dst_row, sem):
    return pltpu.make_async_copy(src.at[src_row], dst.at[dst_row], sem)


def _dispatch_kernel(dest_ref, pad_ref, h2_ref, xs_ref, zero_tile, sem, *, pads_per_step):
    zero_tile[...] = jnp.zeros_like(zero_tile)
    copies = [_tile_copy(h2_ref, a // TOP_K, xs_ref, dest_ref[0, 0, a], sem) for a in range(TD * TOP_K)]
    copies += [pltpu.make_async_copy(zero_tile, xs_ref.at[pad_ref[0, 0, p]], sem) for p in range(pads_per_step)]
    for a, cp in enumerate(copies):
        cp.start(priority=a % 2)
    for cp in copies:
        cp.wait()


def _dispatch(h2, dest, pad_slots, n_slots):
    t = h2.shape[0]
    n_steps = t // TD
    pads_per_step = pad_slots.shape[0] // n_steps
    assert pads_per_step * n_steps == pad_slots.shape[0]
    return pl.pallas_call(
        functools.partial(_dispatch_kernel, pads_per_step=pads_per_step),
        grid=(n_steps,),
        in_specs=[pl.BlockSpec((1, 1, TD * TOP_K), lambda i: (i, 0, 0), memory_space=pltpu.SMEM),
                  pl.BlockSpec((1, 1, pads_per_step), lambda i: (i, 0, 0), memory_space=pltpu.SMEM),
                  pl.BlockSpec((TD, ROW_TILE, LANES), lambda i: (i, 0, 0))],
        out_specs=pl.BlockSpec(memory_space=pl.ANY),
        out_shape=jax.ShapeDtypeStruct((n_slots, ROW_TILE, LANES), F32),
        scratch_shapes=[pltpu.VMEM((ROW_TILE, LANES), F32), pltpu.SemaphoreType.DMA(())],
        compiler_params=pltpu.CompilerParams(dimension_semantics=("arbitrary",), has_side_effects=True),
        name="dispatch",
    )(dest.reshape(n_steps, 1, TD * TOP_K), pad_slots.reshape(n_steps, 1, pads_per_step), h2)


def _experts_kernel(be_ref, first_ref, rows_ref, xs_ref, wup_ref, bup_ref, wdn_ref, bdn_ref, ys_ref,
                    wup_bf, wdn_bf):
    del be_ref
    i = pl.program_id(0)
    cast_rows = 128
    n_rows = rows_ref[i]

    @pl.when(first_ref[i] == 1)
    def _():
        def cast_up(r, carry):
            rows = pl.ds(pl.multiple_of(r * cast_rows, cast_rows), cast_rows)
            wup_bf[rows, :] = wup_ref[0, rows, :].astype(BF16)
            return carry

        def cast_dn(r, carry):
            rows = pl.ds(pl.multiple_of(r * cast_rows, cast_rows), cast_rows)
            wdn_bf[rows, :] = wdn_ref[0, rows, :].astype(BF16)
            return carry

        lax.fori_loop(0, D_MODEL // cast_rows, cast_up, 0)
        lax.fori_loop(0, D_FF // cast_rows, cast_dn, 0)

    @pl.when(n_rows > 0)
    def _():
        xb = jnp.concatenate([xs_ref[:, c, :].astype(BF16) for c in range(ROW_TILE)], axis=1)
        up = jnp.dot(xb, wup_bf[...], preferred_element_type=F32) + bup_ref[0]
        glu = jnp.minimum(up[:, :D_FF], SWIGLU_LIMIT)
        lin = jnp.clip(up[:, D_FF:], -SWIGLU_LIMIT, SWIGLU_LIMIT)
        act = glu * jax.nn.sigmoid(SWIGLU_ALPHA * glu) * (lin + 1.0)
        y = jnp.dot(act.astype(BF16), wdn_bf[...], preferred_element_type=F32) + bdn_ref[0]
        for c in range(ROW_TILE):
            ys_ref[:, c, :] = y[:, c * LANES:(c + 1) * LANES]

    @pl.when(n_rows == 0)
    def _():
        ys_ref[...] = jnp.zeros_like(ys_ref)


def _experts(xs, blk_expert, blk_first, blk_rows, w_up, b_up, w_down, b_down):
    n_slots = xs.shape[0]
    slot_blk = lambda i, be, fi, ro: (i, 0, 0)
    expert_blk = lambda i, be, fi, ro: (be[i], 0, 0)
    grid_spec = pltpu.PrefetchScalarGridSpec(
        num_scalar_prefetch=3,
        grid=(n_slots // BM,),
        in_specs=[
            pl.BlockSpec((BM, ROW_TILE, LANES), slot_blk),
            pl.BlockSpec((1, D_MODEL, 2 * D_FF), expert_blk),
            pl.BlockSpec((1, 1, 2 * D_FF), expert_blk),
            pl.BlockSpec((1, D_FF, D_MODEL), expert_blk),
            pl.BlockSpec((1, 1, D_MODEL), expert_blk),
        ],
        out_specs=pl.BlockSpec((BM, ROW_TILE, LANES), slot_blk),
        scratch_shapes=[pltpu.VMEM((D_MODEL, 2 * D_FF), BF16), pltpu.VMEM((D_FF, D_MODEL), BF16)],
    )
    return pl.pallas_call(
        _experts_kernel,
        grid_spec=grid_spec,
        out_shape=jax.ShapeDtypeStruct((n_slots, ROW_TILE, LANES), F32),
        compiler_params=pltpu.CompilerParams(dimension_semantics=("arbitrary",), vmem_limit_bytes=MOE_VMEM),
        name="experts",
    )(blk_expert, blk_first, blk_rows, xs, w_up, b_up.reshape(N_EXPERTS, 1, -1), w_down,
      b_down.reshape(N_EXPERTS, 1, -1))


def _combine_kernel(dest_ref, dest_next_ref, x1_ref, gate_ref, gf_ref, ys_ref, out_ref, buf, sems):
    i = pl.program_id(0)
    slot = lax.rem(i, 2)

    def gather(d_ref, s):
        return [_tile_copy(ys_ref, d_ref[0, 0, a], buf.at[s, a % TOP_K], a // TOP_K, sems.at[s])
                for a in range(TB * TOP_K)]

    @pl.when(i == 0)
    def _():
        for a, cp in enumerate(gather(dest_ref, slot)):
            cp.start(priority=a % 2)

    @pl.when(i + 1 < pl.num_programs(0))
    def _():
        for a, cp in enumerate(gather(dest_next_ref, 1 - slot)):
            cp.start(priority=a % 2)

    for cp in gather(dest_ref, slot):
        cp.wait()

    cols = []
    for c in range(ROW_TILE):
        acc = x1_ref[:, c * LANES:(c + 1) * LANES]
        for k in range(TOP_K):
            acc = acc + gate_ref[:, k:k + 1] * buf[slot, k, :, c, :]
        cols.append(acc)
    out_ref[...] = _rmsnorm(jnp.concatenate(cols, axis=1), gf_ref[...])


def _combine(x1, gates, ys, dest, gf):
    t = x1.shape[0]
    n_blk = t // TB
    dest_blocks = dest.reshape(n_blk, 1, TB * TOP_K)
    return pl.pallas_call(
        _combine_kernel,
        grid=(n_blk,),
        in_specs=[pl.BlockSpec((1, 1, TB * TOP_K), lambda i: (i, 0, 0), memory_space=pltpu.SMEM),
                  pl.BlockSpec((1, 1, TB * TOP_K), lambda i: (jnp.minimum(i + 1, n_blk - 1), 0, 0),
                               memory_space=pltpu.SMEM),
                  pl.BlockSpec((TB, D_MODEL), lambda i: (i, 0)),
                  pl.BlockSpec((TB, LANES), lambda i: (i, 0)),
                  _const_spec((1, D_MODEL)),
                  pl.BlockSpec(memory_space=pl.ANY)],
        out_specs=pl.BlockSpec((TB, D_MODEL), lambda i: (i, 0)),
        out_shape=jax.ShapeDtypeStruct((t, D_MODEL), F32),
        scratch_shapes=[pltpu.VMEM((2, TOP_K, TB, ROW_TILE, LANES), F32), pltpu.SemaphoreType.DMA((2,))],
        compiler_params=pltpu.CompilerParams(dimension_semantics=("arbitrary",)),
        name="combine",
    )(dest_blocks, dest_blocks, x1, gates, gf.reshape(1, -1), ys)


def _routing(top_idx, local_rank, blk_counts):
    t = top_idx.shape[0]
    n_slots = t * TOP_K + N_EXPERTS * BM
    n_blocks = n_slots // BM
    counts = jnp.sum(blk_counts, axis=0)
    padded = (counts + BM - 1) // BM * BM
    pend = jnp.cumsum(padded)
    pstart = pend - padded
    base = pstart[None, :] + jnp.cumsum(blk_counts, axis=0) - blk_counts
    base_tok = jnp.repeat(base, TM, axis=0)
    experts = jnp.arange(N_EXPERTS, dtype=jnp.int32)
    dest = local_rank + jnp.sum(jnp.where(top_idx[:, :, None] == experts, base_tok[:, None, :], 0), axis=-1)
    blk_start = jnp.arange(n_blocks, dtype=jnp.int32) * BM
    blk_expert = jnp.minimum(jnp.searchsorted(pend, blk_start, side="right"), N_EXPERTS - 1).astype(jnp.int32)
    blk_rows = jnp.clip((pstart + counts)[blk_expert] - blk_start, 0, BM).astype(jnp.int32)
    changed = jnp.concatenate([jnp.ones((1,), jnp.int32), (blk_expert[1:] != blk_expert[:-1]).astype(jnp.int32)])
    blk_first = changed * (blk_rows > 0)
    seg_size = jnp.concatenate([padded - counts, n_slots - pend[-1:]])
    seg_slot = jnp.concatenate([pstart + counts, pend[-1:]])
    seg_end = jnp.cumsum(seg_size)
    ordinal = jnp.arange(n_slots - t * TOP_K, dtype=jnp.int32)
    seg = jnp.searchsorted(seg_end, ordinal, side="right")
    pad_slots = seg_slot[seg] + ordinal - (seg_end - seg_size)[seg]
    return (dest.astype(jnp.int32), pad_slots.astype(jnp.int32), blk_expert, blk_first.astype(jnp.int32),
            blk_rows, n_slots)


def kernel(x, positions, norm1_g, w_in, b_in, attn_sinks, w_attn_br, pool_w, pool_scale, w_pool_br, w_out,
           norm2_g, w_router, b_router, w_up, b_up, w_down, b_down, normf_g):
    b, s, d = x.shape
    t = b * s
    assert norm1_g.shape[0] == 1, "single-layer problem: the combine kernel applies the final RMSNorm"
    assert d == D_MODEL and s % TM == 0 and t % ROPE_TN == 0 and t % TD == 0 and t % TB == 0
    inv_freq = jnp.power(jnp.float32(ROPE_THETA), -jnp.arange(ROPE_HALF, dtype=F32) * (2.0 / ROPE_DIM))
    cs = _rope_table(positions.reshape(1, t), inv_freq).T
    cs = jnp.pad(cs, ((0, 0), (0, LANES - 2 * ROPE_HALF)))
    x1, h2, idx_pad, gate_pad, blk_counts = _mixer(
        x.reshape(t, d), cs, attn_sinks[0], norm1_g[0], w_in[0], b_in[0], w_attn_br[0], pool_w[0],
        pool_scale[0], w_pool_br[0], w_out[0], norm2_g[0], w_router[0], b_router[0], s)
    dest, pad_slots, blk_expert, blk_first, blk_rows, n_slots = _routing(
        idx_pad[:, :TOP_K], idx_pad[:, TOP_K:2 * TOP_K], blk_counts.reshape(t // TM, N_EXPERTS))
    xs = _dispatch(h2, dest, pad_slots, n_slots)
    ys = _experts(xs, blk_expert, blk_first, blk_rows, w_up[0], b_up[0], w_down[0], b_down[0])
    out = _combine(x1, gate_pad, ys, dest, normf_g)
    return out.reshape(b, s, d)
```

```python
import functools

import numpy as np
import jax
import jax.numpy as jnp
from jax import lax
from jax.experimental import pallas as pl
from jax.experimental.pallas import tpu as pltpu

F32 = jnp.float32
BF16 = jnp.bfloat16

D_MODEL = 1024
HEAD_DIM = 64
N_Q_HEADS = 16
N_KV_HEADS = 2
GROUP = N_Q_HEADS // N_KV_HEADS
WINDOW = 128
ROPE_THETA = 500000.0
ROPE_DIM = HEAD_DIM // 4
ROPE_HALF = ROPE_DIM // 2
Q_WIDTH = N_Q_HEADS * HEAD_DIM
KV_WIDTH = N_KV_HEADS * HEAD_DIM
NEG_INF = -1e30
POOL_WINDOWS = (2, 4, 8, 16)
POOL_WIDTH = D_MODEL // 2
POOL_GROUP_WIDTH = POOL_WIDTH // len(POOL_WINDOWS)
POOL_HALO = 16
IN_WIDTH = Q_WIDTH + 2 * KV_WIDTH + POOL_WIDTH + 2 * D_MODEL
O_K = Q_WIDTH
O_V = O_K + KV_WIDTH
O_P = O_V + KV_WIDTH
O_GA = O_P + POOL_WIDTH
O_GP = O_GA + D_MODEL
N_EXPERTS = 32
TOP_K = 4
D_FF = D_MODEL
SWIGLU_ALPHA = 1.702
SWIGLU_LIMIT = 7.0
RMS_EPS = 1e-5

LANES = 128
ROW_TILE = D_MODEL // LANES
TM = 256
TD = 512
TB = 128
BM = 256
ROPE_TN = 2048
MIX_VMEM = 56 * 1024 * 1024
MOE_VMEM = 56 * 1024 * 1024


def _const_spec(shape):
    return pl.BlockSpec(shape, lambda *_: (0,) * len(shape))


def _chunk_of_tokens(c, n_tokens):
    return pl.ds(c, n_tokens, stride=ROW_TILE)


def _tile_of_token(token):
    if isinstance(token, int):
        return pl.ds(token * ROW_TILE, ROW_TILE)
    return pl.ds(pl.multiple_of(token * ROW_TILE, ROW_TILE), ROW_TILE)


def _rope_table_kernel(pos_ref, invf_ref, out_ref):
    ang = invf_ref[...] * pos_ref[...].astype(F32)
    out_ref[0:ROPE_HALF, :] = jnp.cos(ang)
    out_ref[ROPE_HALF:, :] = jnp.sin(ang)


def _rope_table(pos_row, inv_freq):
    t = pos_row.shape[1]
    return pl.pallas_call(
        _rope_table_kernel,
        grid=(t // ROPE_TN,),
        in_specs=[pl.BlockSpec((1, ROPE_TN), lambda i: (0, i)), _const_spec((ROPE_HALF, 1))],
        out_specs=pl.BlockSpec((2 * ROPE_HALF, ROPE_TN), lambda i: (0, i)),
        out_shape=jax.ShapeDtypeStruct((2 * ROPE_HALF, t), F32),
        name="rope_table",
    )(pos_row, inv_freq.reshape(ROPE_HALF, 1))


def _rope_expand_matrix():
    e = np.zeros((LANES, 2 * LANES), np.float32)
    ones = np.zeros((1, LANES), np.float32)
    for l in range(LANES):
        hl = l % HEAD_DIM
        if hl < ROPE_DIM:
            j = hl % ROPE_HALF
            e[j, l] = 1.0
            e[ROPE_HALF + j, LANES + l] = -1.0 if hl < ROPE_HALF else 1.0
        else:
            ones[0, l] = 1.0
    return e, ones


def _rmsnorm(x, g):
    ms = jnp.mean(x * x, axis=-1, keepdims=True)
    return x * lax.rsqrt(ms + RMS_EPS) * g


def _split3(x):
    a = x.astype(BF16)
    r = x - a.astype(F32)
    b = r.astype(BF16)
    c = (r - b.astype(F32)).astype(BF16)
    return a, b, c


def _mixer_kernel(sinks_ref, x_ref, cs_ref, erope_ref, ones_ref, g1_ref, win_ref, bin_ref, wab_ref,
                  pw_ref, ps_ref, wpb_ref, wout_ref, g2_ref, wrh_ref, wrl_ref, br_ref,
                  x1_ref, h2_ref, idx_ref, gate_ref, cnt_ref,
                  kprev, vprev, uprev, q_scr, attn_scr, *, blocks_per_seq):
    i = pl.program_id(0)
    blk_in_seq = lax.rem(i, blocks_per_seq)
    seq_start = blk_in_seq == 0

    @pl.when(seq_start)
    def _():
        kprev[...] = jnp.zeros_like(kprev)
        vprev[...] = jnp.zeros_like(vprev)
        uprev[...] = jnp.zeros_like(uprev)

    x = x_ref[...]
    h = _rmsnorm(x, g1_ref[...]).astype(BF16)

    c1, c2, c3 = _split3(cs_ref[...])
    er = erope_ref[...]
    tab = (jnp.dot(c1, er, preferred_element_type=F32) + jnp.dot(c2, er, preferred_element_type=F32)
           + jnp.dot(c3, er, preferred_element_type=F32))
    cos_t = tab[:, :LANES] + ones_ref[...]
    sin_t = tab[:, LANES:]
    lane = lax.broadcasted_iota(jnp.int32, (TM, LANES), 1)
    first_half = (lane & (HEAD_DIM - 1)) < ROPE_HALF

    def rope(t):
        partner = jnp.where(first_half, pltpu.roll(t, LANES - ROPE_HALF, 1), pltpu.roll(t, ROPE_HALF, 1))
        return t * cos_t + partner * sin_t

    q = jnp.dot(h, win_ref[:, 0:Q_WIDTH], preferred_element_type=F32) + bin_ref[:, 0:Q_WIDTH]
    for c in range(Q_WIDTH // LANES):
        qc = rope(q[:, c * LANES:(c + 1) * LANES]) * (HEAD_DIM ** -0.5)
        q_scr[:, c * LANES:(c + 1) * LANES] = qc.astype(BF16)

    kvu = jnp.dot(h, win_ref[:, O_K:O_GA], preferred_element_type=F32) + bin_ref[:, O_K:O_GA]
    k = rope(kvu[:, 0:KV_WIDTH])
    v = kvu[:, KV_WIDTH:2 * KV_WIDTH]
    u = kvu[:, 2 * KV_WIDTH:]

    lane_b = lax.broadcasted_iota(jnp.int32, (TM + WINDOW, LANES), 1)
    low = lane_b < HEAD_DIM

    def head_bands(prev_ref, cur):
        band = jnp.concatenate([prev_ref[...], cur], axis=0)
        swapped = pltpu.roll(band, HEAD_DIM, 1)
        zero = jnp.zeros_like(band)
        a0 = jnp.where(low, band, zero).astype(BF16)
        b0 = jnp.where(low, zero, swapped).astype(BF16)
        a1 = jnp.where(low, swapped, zero).astype(BF16)
        b1 = jnp.where(low, zero, band).astype(BF16)
        return ((a0, b0), (a1, b1))

    kb = head_bands(kprev, k)
    vb = head_bands(vprev, v)
    kprev[...] = k[TM - WINDOW:, :]
    vprev[...] = v[TM - WINDOW:, :]

    qi = lax.broadcasted_iota(jnp.int32, (WINDOW, 2 * WINDOW), 0)
    kj = lax.broadcasted_iota(jnp.int32, (WINDOW, 2 * WINDOW), 1)
    mask_mid = jnp.logical_or(jnp.logical_and(kj < WINDOW, kj > qi),
                              jnp.logical_and(kj >= WINDOW, (kj - WINDOW) <= qi))
    first_key = jnp.where(seq_start, WINDOW, 0)
    mask_first = jnp.logical_and(mask_mid, kj >= first_key)

    low_w = lax.broadcasted_iota(jnp.int32, (WINDOW, LANES), 1) < HEAD_DIM
    n_chunks = GROUP // 2
    for j in range(TM // WINDOW):
        mask = mask_first if j == 0 else mask_mid
        rows = slice(j * WINDOW, (j + 1) * WINDOW)
        band_rows = slice(j * WINDOW, j * WINDOW + 2 * WINDOW)
        for hkv in range(N_KV_HEADS):
            q4 = jnp.concatenate(
                [q_scr[rows, (hkv * n_chunks + c) * LANES:(hkv * n_chunks + c + 1) * LANES]
                 for c in range(n_chunks)], axis=0)
            kk = jnp.concatenate([kb[hkv][0][band_rows], kb[hkv][1][band_rows]], axis=0)
            vv = jnp.concatenate([vb[hkv][0][band_rows], vb[hkv][1][band_rows]], axis=0)
            s = lax.dot_general(q4, kk, (((1,), (1,)), ((), ())), preferred_element_type=F32)
            p_rows = []
            rden = []
            for c in range(n_chunks):
                p_cols = []
                rd = []
                for par in range(2):
                    sink = sinks_ref[hkv * GROUP + 2 * c + par]
                    sc = s[c * WINDOW:(c + 1) * WINDOW, par * 2 * WINDOW:(par + 1) * 2 * WINDOW]
                    sc = jnp.where(mask, sc, NEG_INF)
                    m = jnp.maximum(jnp.max(sc, axis=-1, keepdims=True), sink)
                    e = jnp.exp(sc - m)
                    den = jnp.sum(e, axis=-1, keepdims=True) + jnp.exp(sink - m)
                    p_cols.append(e.astype(BF16))
                    rd.append(1.0 / den)
                p_rows.append(jnp.concatenate(p_cols, axis=1))
                rden.append(rd)
            p = jnp.concatenate(p_rows, axis=0)
            o = jnp.dot(p, vv, preferred_element_type=F32)
            for c in range(n_chunks):
                scale = jnp.where(low_w, rden[c][0], rden[c][1])
                oc = o[c * WINDOW:(c + 1) * WINDOW, :] * scale
                col = (hkv * n_chunks + c) * LANES
                attn_scr[rows, col:col + LANES] = oc.astype(BF16)

    y_attn = jnp.dot(attn_scr[...], wab_ref[...], preferred_element_type=F32)

    ext = jnp.concatenate([uprev[...], u], axis=0)
    uprev[...] = u[TM - POOL_HALO:, :]
    pos_in_seq = blk_in_seq * TM + lax.broadcasted_iota(jnp.int32, (TM, 1), 0)
    mixed = []
    for g, w in enumerate(POOL_WINDOWS):
        cols = slice(g * POOL_GROUP_WIDTH, (g + 1) * POOL_GROUP_WIDTH)
        sg = ext[:, cols]
        step = 1
        while step < w:
            sg = sg + pltpu.roll(sg, step, 0)
            step *= 2
        count = jnp.minimum(pos_in_seq + 1, w).astype(F32)
        pooled = sg[POOL_HALO:, :] / count - u[:, cols]
        mg = jnp.dot(pooled.astype(BF16), pw_ref[g], preferred_element_type=F32)
        mixed.append((mg * ps_ref[:, cols]).astype(BF16))
    y_pool = jnp.dot(jnp.concatenate(mixed, axis=1), wpb_ref[...], preferred_element_type=F32)

    g_attn = jax.nn.sigmoid(jnp.dot(h, win_ref[:, O_GA:O_GP], preferred_element_type=F32)
                            + bin_ref[:, O_GA:O_GP])
    g_pool = jax.nn.sigmoid(jnp.dot(h, win_ref[:, O_GP:], preferred_element_type=F32)
                            + bin_ref[:, O_GP:])
    merged = (g_attn * y_attn + g_pool * y_pool).astype(BF16)
    x1 = x + jnp.dot(merged, wout_ref[...], preferred_element_type=F32)
    x1_ref[...] = x1

    h2 = _rmsnorm(x1, g2_ref[...])
    for c in range(ROW_TILE):
        h2_ref[_chunk_of_tokens(c, TM), :] = h2[:, c * LANES:(c + 1) * LANES]

    ha = h2.astype(BF16)
    hb = (h2 - ha.astype(F32)).astype(BF16)
    wrh = wrh_ref[...]
    logits = (jnp.dot(ha, wrh, preferred_element_type=F32) + jnp.dot(ha, wrl_ref[...], preferred_element_type=F32)
              + jnp.dot(hb, wrh, preferred_element_type=F32) + br_ref[...])
    e_iota = lax.broadcasted_iota(jnp.int32, (TM, N_EXPERTS), 1)
    vals, idxs = [], []
    work = logits
    for _ in range(TOP_K):
        m = jnp.max(work, axis=-1, keepdims=True)
        ix = jnp.min(jnp.where(work == m, e_iota, N_EXPERTS), axis=-1, keepdims=True)
        vals.append(m)
        idxs.append(ix)
        work = jnp.where(e_iota == ix, -jnp.inf, work)
    exps = [jnp.exp(vk - vals[0]) for vk in vals]
    den = exps[0] + exps[1] + exps[2] + exps[3]

    hots = [e_iota == ix for ix in idxs]
    picked = jnp.zeros((TM, N_EXPERTS), F32)
    for hot in hots:
        picked = jnp.where(hot, 1.0, picked)
    t_row = lax.broadcasted_iota(jnp.int32, (TM, TM), 0)
    t_col = lax.broadcasted_iota(jnp.int32, (TM, TM), 1)
    earlier = jnp.where(t_col < t_row, 1.0, 0.0).astype(BF16)
    before = jnp.dot(earlier, picked.astype(BF16), preferred_element_type=F32)
    cnt_ref[0] = jnp.sum(picked, axis=0, keepdims=True).astype(jnp.int32)

    idx_out = jnp.zeros((TM, LANES), jnp.int32)
    gate_out = jnp.zeros((TM, LANES), F32)
    for kk_ in range(TOP_K):
        local_rank = jnp.sum(jnp.where(hots[kk_], before, 0.0), axis=-1, keepdims=True).astype(jnp.int32)
        idx_out = jnp.where(lane == kk_, idxs[kk_], idx_out)
        idx_out = jnp.where(lane == TOP_K + kk_, local_rank, idx_out)
        gate_out = jnp.where(lane == kk_, exps[kk_] / den, gate_out)
    idx_ref[...] = idx_out
    gate_ref[...] = gate_out


def _mixer(x2d, cs, sinks, g1, w_in, b_in, w_ab, pool_w, pool_scale, w_pb, w_out, g2, w_r, b_r, seq_len):
    t = x2d.shape[0]
    erope, ones = _rope_expand_matrix()
    w_rh = w_r.astype(BF16)
    w_rl = (w_r - w_rh.astype(F32)).astype(BF16)
    row = lambda i: (i, 0)
    in_specs = [
        pl.BlockSpec(memory_space=pltpu.SMEM),
        pl.BlockSpec((TM, D_MODEL), row),
        pl.BlockSpec((TM, LANES), row),
        _const_spec((LANES, 2 * LANES)),
        _const_spec((1, LANES)),
        _const_spec((1, D_MODEL)),
        _const_spec((D_MODEL, IN_WIDTH)),
        _const_spec((1, IN_WIDTH)),
        _const_spec((Q_WIDTH, D_MODEL)),
        _const_spec((len(POOL_WINDOWS), POOL_GROUP_WIDTH, POOL_GROUP_WIDTH)),
        _const_spec((1, POOL_WIDTH)),
        _const_spec((POOL_WIDTH, D_MODEL)),
        _const_spec((D_MODEL, D_MODEL)),
        _const_spec((1, D_MODEL)),
        _const_spec((D_MODEL, N_EXPERTS)),
        _const_spec((D_MODEL, N_EXPERTS)),
        _const_spec((1, N_EXPERTS)),
    ]
    tile3 = lambda i: (i, 0, 0)
    out_specs = [pl.BlockSpec((TM, D_MODEL), row), pl.BlockSpec((TM * ROW_TILE, LANES), row),
                 pl.BlockSpec((TM, LANES), row), pl.BlockSpec((TM, LANES), row),
                 pl.BlockSpec((1, 1, N_EXPERTS), tile3)]
    out_shape = [jax.ShapeDtypeStruct((t, D_MODEL), F32), jax.ShapeDtypeStruct((t * ROW_TILE, LANES), F32),
                 jax.ShapeDtypeStruct((t, LANES), jnp.int32), jax.ShapeDtypeStruct((t, LANES), F32),
                 jax.ShapeDtypeStruct((t // TM, 1, N_EXPERTS), jnp.int32)]
    scratch = [pltpu.VMEM((WINDOW, KV_WIDTH), F32), pltpu.VMEM((WINDOW, KV_WIDTH), F32),
               pltpu.VMEM((POOL_HALO, POOL_WIDTH), F32),
               pltpu.VMEM((TM, Q_WIDTH), BF16), pltpu.VMEM((TM, Q_WIDTH), BF16)]
    return pl.pallas_call(
        functools.partial(_mixer_kernel, blocks_per_seq=seq_len // TM),
        grid=(t // TM,),
        in_specs=in_specs, out_specs=out_specs, out_shape=out_shape, scratch_shapes=scratch,
        compiler_params=pltpu.CompilerParams(dimension_semantics=("arbitrary",), vmem_limit_bytes=MIX_VMEM),
        name="mixer",
    )(sinks, x2d, cs, jnp.asarray(erope, BF16), jnp.asarray(ones), g1.reshape(1, -1), w_in.astype(BF16),
      b_in.reshape(1, -1), w_ab.astype(BF16), pool_w.astype(BF16), pool_scale.reshape(1, -1),
      w_pb.astype(BF16), w_out.astype(BF16), g2.reshape(1, -1), w_rh, w_rl, b_r.reshape(1, -1))


def _tile_copy(src, src_token, dst, dst_token, sem):
    return pltpu.make_async_copy(src.at[_tile_of_token(src_token)], dst.at[_tile_of_token(dst_token)], sem)


def _dispatch_kernel(dest_ref, pad_ref, h2_ref, xs_ref, zero_tile, sem, *, pads_per_step):
    zero_tile[...] = jnp.zeros_like(zero_tile)
    copies = [_tile_copy(h2_ref, a // TOP_K, xs_ref, dest_ref[0, 0, a], sem) for a in range(TD * TOP_K)]
    copies += [pltpu.make_async_copy(zero_tile, xs_ref.at[_tile_of_token(pad_ref[0, 0, p])], sem)
               for p in range(pads_per_step)]
    for a, cp in enumerate(copies):
        cp.start(priority=a % 2)
    for cp in copies:
        cp.wait()


def _dispatch(h2, dest, pad_slots, n_slots):
    t = h2.shape[0] // ROW_TILE
    n_steps = t // TD
    pads_per_step = pad_slots.shape[0] // n_steps
    assert pads_per_step * n_steps == pad_slots.shape[0]
    return pl.pallas_call(
        functools.partial(_dispatch_kernel, pads_per_step=pads_per_step),
        grid=(n_steps,),
        in_specs=[pl.BlockSpec((1, 1, TD * TOP_K), lambda i: (i, 0, 0), memory_space=pltpu.SMEM),
                  pl.BlockSpec((1, 1, pads_per_step), lambda i: (i, 0, 0), memory_space=pltpu.SMEM),
                  pl.BlockSpec((TD * ROW_TILE, LANES), lambda i: (i, 0))],
        out_specs=pl.BlockSpec(memory_space=pl.ANY),
        out_shape=jax.ShapeDtypeStruct((n_slots * ROW_TILE, LANES), F32),
        scratch_shapes=[pltpu.VMEM((ROW_TILE, LANES), F32), pltpu.SemaphoreType.DMA(())],
        compiler_params=pltpu.CompilerParams(dimension_semantics=("arbitrary",), has_side_effects=True),
        name="dispatch",
    )(dest.reshape(n_steps, 1, TD * TOP_K), pad_slots.reshape(n_steps, 1, pads_per_step), h2)


def _experts_kernel(be_ref, first_ref, rows_ref, xs_ref, wup_ref, bup_ref, wdn_ref, bdn_ref, ys_ref,
                    wup_bf, wdn_bf):
    del be_ref
    i = pl.program_id(0)
    cast_rows = 128
    n_rows = rows_ref[i]

    @pl.when(first_ref[i] == 1)
    def _():
        def cast_up(r, carry):
            rows = pl.ds(pl.multiple_of(r * cast_rows, cast_rows), cast_rows)
            wup_bf[rows, :] = wup_ref[0, rows, :].astype(BF16)
            return carry

        def cast_dn(r, carry):
            rows = pl.ds(pl.multiple_of(r * cast_rows, cast_rows), cast_rows)
            wdn_bf[rows, :] = wdn_ref[0, rows, :].astype(BF16)
            return carry

        lax.fori_loop(0, D_MODEL // cast_rows, cast_up, 0)
        lax.fori_loop(0, D_FF // cast_rows, cast_dn, 0)

    @pl.when(n_rows > 0)
    def _():
        xb = jnp.concatenate([xs_ref[_chunk_of_tokens(c, BM), :].astype(BF16) for c in range(ROW_TILE)],
                             axis=1)
        up = jnp.dot(xb, wup_bf[...], preferred_element_type=F32) + bup_ref[0]
        glu = jnp.minimum(up[:, :D_FF], SWIGLU_LIMIT)
        lin = jnp.clip(up[:, D_FF:], -SWIGLU_LIMIT, SWIGLU_LIMIT)
        act = glu * jax.nn.sigmoid(SWIGLU_ALPHA * glu) * (lin + 1.0)
        y = jnp.dot(act.astype(BF16), wdn_bf[...], preferred_element_type=F32) + bdn_ref[0]
        for c in range(ROW_TILE):
            ys_ref[_chunk_of_tokens(c, BM), :] = y[:, c * LANES:(c + 1) * LANES]

    @pl.when(n_rows == 0)
    def _():
        ys_ref[...] = jnp.zeros_like(ys_ref)


def _experts(xs, blk_expert, blk_first, blk_rows, w_up, b_up, w_down, b_down):
    n_slots = xs.shape[0] // ROW_TILE
    slot_blk = lambda i, be, fi, ro: (i, 0)
    expert_blk = lambda i, be, fi, ro: (be[i], 0, 0)
    grid_spec = pltpu.PrefetchScalarGridSpec(
        num_scalar_prefetch=3,
        grid=(n_slots // BM,),
        in_specs=[
            pl.BlockSpec((BM * ROW_TILE, LANES), slot_blk),
            pl.BlockSpec((1, D_MODEL, 2 * D_FF), expert_blk),
            pl.BlockSpec((1, 1, 2 * D_FF), expert_blk),
            pl.BlockSpec((1, D_FF, D_MODEL), expert_blk),
            pl.BlockSpec((1, 1, D_MODEL), expert_blk),
        ],
        out_specs=pl.BlockSpec((BM * ROW_TILE, LANES), slot_blk),
        scratch_shapes=[pltpu.VMEM((D_MODEL, 2 * D_FF), BF16), pltpu.VMEM((D_FF, D_MODEL), BF16)],
    )
    return pl.pallas_call(
        _experts_kernel,
        grid_spec=grid_spec,
        out_shape=jax.ShapeDtypeStruct((n_slots * ROW_TILE, LANES), F32),
        compiler_params=pltpu.CompilerParams(dimension_semantics=("arbitrary",), vmem_limit_bytes=MOE_VMEM),
        name="experts",
    )(blk_expert, blk_first, blk_rows, xs, w_up, b_up.reshape(N_EXPERTS, 1, -1), w_down,
      b_down.reshape(N_EXPERTS, 1, -1))


def _combine_kernel(dest_ref, dest_next_ref, x1_ref, gate_ref, gf_ref, ys_ref, out_ref, buf, sems):
    i = pl.program_id(0)
    slot = lax.rem(i, 2)

    def gather(d_ref, s):
        return [_tile_copy(ys_ref, d_ref[0, 0, a], buf.at[s, a % TOP_K], a // TOP_K, sems.at[s])
                for a in range(TB * TOP_K)]

    @pl.when(i == 0)
    def _():
        for a, cp in enumerate(gather(dest_ref, slot)):
            cp.start(priority=a % 2)

    @pl.when(i + 1 < pl.num_programs(0))
    def _():
        for a, cp in enumerate(gather(dest_next_ref, 1 - slot)):
            cp.start(priority=a % 2)

    for cp in gather(dest_ref, slot):
        cp.wait()

    cols = []
    for c in range(ROW_TILE):
        acc = x1_ref[:, c * LANES:(c + 1) * LANES]
        for k in range(TOP_K):
            acc = acc + gate_ref[:, k:k + 1] * buf[slot, k, _chunk_of_tokens(c, TB), :]
        cols.append(acc)
    out_ref[...] = _rmsnorm(jnp.concatenate(cols, axis=1), gf_ref[...])


def _combine(x1, gates, ys, dest, gf):
    t = x1.shape[0]
    n_blk = t // TB
    dest_blocks = dest.reshape(n_blk, 1, TB * TOP_K)
    return pl.pallas_call(
        _combine_kernel,
        grid=(n_blk,),
        in_specs=[pl.BlockSpec((1, 1, TB * TOP_K), lambda i: (i, 0, 0), memory_space=pltpu.SMEM),
                  pl.BlockSpec((1, 1, TB * TOP_K), lambda i: (jnp.minimum(i + 1, n_blk - 1), 0, 0),
                               memory_space=pltpu.SMEM),
                  pl.BlockSpec((TB, D_MODEL), lambda i: (i, 0)),
                  pl.BlockSpec((TB, LANES), lambda i: (i, 0)),
                  _const_spec((1, D_MODEL)),
                  pl.BlockSpec(memory_space=pl.ANY)],
        out_specs=pl.BlockSpec((TB, D_MODEL), lambda i: (i, 0)),
        out_shape=jax.ShapeDtypeStruct((t, D_MODEL), F32),
        scratch_shapes=[pltpu.VMEM((2, TOP_K, TB * ROW_TILE, LANES), F32), pltpu.SemaphoreType.DMA((2,))],
        compiler_params=pltpu.CompilerParams(dimension_semantics=("arbitrary",)),
        name="combine",
    )(dest_blocks, dest_blocks, x1, gates, gf.reshape(1, -1), ys)


def _count_le(sorted_ends, queries):
    return jnp.sum(sorted_ends[None, :] <= queries[:, None], axis=1, dtype=jnp.int32)


def _routing(top_idx, local_rank, blk_counts):
    t = top_idx.shape[0]
    n_slots = t * TOP_K + N_EXPERTS * BM
    n_blocks = n_slots // BM
    counts = jnp.sum(blk_counts, axis=0)
    padded = (counts + BM - 1) // BM * BM
    pend = jnp.cumsum(padded)
    pstart = pend - padded
    base = pstart[None, :] + jnp.cumsum(blk_counts, axis=0) - blk_counts
    base_tok = jnp.repeat(base, TM, axis=0)
    experts = jnp.arange(N_EXPERTS, dtype=jnp.int32)
    dest = local_rank + jnp.sum(jnp.where(top_idx[:, :, None] == experts, base_tok[:, None, :], 0), axis=-1)
    blk_start = jnp.arange(n_blocks, dtype=jnp.int32) * BM
    blk_expert = jnp.minimum(_count_le(pend, blk_start), N_EXPERTS - 1)
    blk_rows = jnp.clip((pstart + counts)[blk_expert] - blk_start, 0, BM).astype(jnp.int32)
    changed = jnp.concatenate([jnp.ones((1,), jnp.int32), (blk_expert[1:] != blk_expert[:-1]).astype(jnp.int32)])
    blk_first = changed * (blk_rows > 0)
    seg_size = jnp.concatenate([padded - counts, n_slots - pend[-1:]])
    seg_slot = jnp.concatenate([pstart + counts, pend[-1:]])
    seg_end = jnp.cumsum(seg_size)
    ordinal = jnp.arange(n_slots - t * TOP_K, dtype=jnp.int32)
    seg = _count_le(seg_end, ordinal)
    pad_slots = seg_slot[seg] + ordinal - (seg_end - seg_size)[seg]
    return (dest.astype(jnp.int32), pad_slots.astype(jnp.int32), blk_expert, blk_first.astype(jnp.int32),
            blk_rows, n_slots)


def kernel(x, positions, norm1_g, w_in, b_in, attn_sinks, w_attn_br, pool_w, pool_scale, w_pool_br, w_out,
           norm2_g, w_router, b_router, w_up, b_up, w_down, b_down, normf_g):
    b, s, d = x.shape
    t = b * s
    assert norm1_g.shape[0] == 1, "single-layer problem: the combine kernel applies the final RMSNorm"
    assert d == D_MODEL and s % TM == 0 and t % ROPE_TN == 0 and t % TD == 0 and t % TB == 0
    inv_freq = jnp.power(jnp.float32(ROPE_THETA), -jnp.arange(ROPE_HALF, dtype=F32) * (2.0 / ROPE_DIM))
    cs = _rope_table(positions.reshape(1, t), inv_freq).T
    cs = jnp.pad(cs, ((0, 0), (0, LANES - 2 * ROPE_HALF)))
    x1, h2, idx_pad, gate_pad, blk_counts = _mixer(
        x.reshape(t, d), cs, attn_sinks[0], norm1_g[0], w_in[0], b_in[0], w_attn_br[0], pool_w[0],
        pool_scale[0], w_pool_br[0], w_out[0], norm2_g[0], w_router[0], b_router[0], s)
    dest, pad_slots, blk_expert, blk_first, blk_rows, n_slots = _routing(
        idx_pad[:, :TOP_K], idx_pad[:, TOP_K:2 * TOP_K], blk_counts.reshape(t // TM, N_EXPERTS))
    xs = _dispatch(h2, dest, pad_slots, n_slots)
    ys = _experts(xs, blk_expert, blk_first, blk_rows, w_up[0], b_up[0], w_down[0], b_down[0])
    out = _combine(x1, gate_pad, ys, dest, normf_g)
    return out.reshape(b, s, d)
```

```python
import functools

import numpy as np
import jax
import jax.numpy as jnp
from jax import lax
from jax.experimental import pallas as pl
from jax.experimental.pallas import tpu as pltpu

F32 = jnp.float32
BF16 = jnp.bfloat16

D_MODEL = 1024
HEAD_DIM = 64
N_Q_HEADS = 16
N_KV_HEADS = 2
GROUP = N_Q_HEADS // N_KV_HEADS
WINDOW = 128
ROPE_THETA = 500000.0
ROPE_DIM = HEAD_DIM // 4
ROPE_HALF = ROPE_DIM // 2
Q_WIDTH = N_Q_HEADS * HEAD_DIM
KV_WIDTH = N_KV_HEADS * HEAD_DIM
NEG_INF = -1e30
POOL_WINDOWS = (2, 4, 8, 16)
POOL_WIDTH = D_MODEL // 2
POOL_GROUP_WIDTH = POOL_WIDTH // len(POOL_WINDOWS)
POOL_HALO = 16
IN_WIDTH = Q_WIDTH + 2 * KV_WIDTH + POOL_WIDTH + 2 * D_MODEL
O_K = Q_WIDTH
O_V = O_K + KV_WIDTH
O_P = O_V + KV_WIDTH
O_GA = O_P + POOL_WIDTH
O_GP = O_GA + D_MODEL
N_EXPERTS = 32
TOP_K = 4
D_FF = D_MODEL
SWIGLU_ALPHA = 1.702
SWIGLU_LIMIT = 7.0
RMS_EPS = 1e-5

LANES = 128
ROW_TILE = D_MODEL // LANES
ROUTE_ROWS = 16
TM = 256
TD = 512
TB = 128
BM = 256
ROPE_TN = 2048
MIX_VMEM = 56 * 1024 * 1024
MOE_VMEM = 58 * 1024 * 1024
NT_DIMS = (((1,), (1,)), ((), ()))


def _const_spec(shape):
    return pl.BlockSpec(shape, lambda *_: (0,) * len(shape))


def _chunk_of_tokens(c, n_tokens):
    return pl.ds(c, n_tokens, stride=ROW_TILE)


def _tile_of_token(token):
    if isinstance(token, int):
        return pl.ds(token * ROW_TILE, ROW_TILE)
    return pl.ds(pl.multiple_of(token * ROW_TILE, ROW_TILE), ROW_TILE)


def _rope_table_kernel(pos_ref, invf_ref, out_ref):
    ang = invf_ref[...] * pos_ref[...].astype(F32)
    out_ref[0:ROPE_HALF, :] = jnp.cos(ang)
    out_ref[ROPE_HALF:, :] = jnp.sin(ang)


def _rope_table(pos_row, inv_freq):
    t = pos_row.shape[1]
    return pl.pallas_call(
        _rope_table_kernel,
        grid=(t // ROPE_TN,),
        in_specs=[pl.BlockSpec((1, ROPE_TN), lambda i: (0, i)), _const_spec((ROPE_HALF, 1))],
        out_specs=pl.BlockSpec((2 * ROPE_HALF, ROPE_TN), lambda i: (0, i)),
        out_shape=jax.ShapeDtypeStruct((2 * ROPE_HALF, t), F32),
        name="rope_table",
    )(pos_row, inv_freq.reshape(ROPE_HALF, 1))


def _rope_expand_matrix():
    e = np.zeros((LANES, 2 * LANES), np.float32)
    ones = np.zeros((1, LANES), np.float32)
    for l in range(LANES):
        hl = l % HEAD_DIM
        if hl < ROPE_DIM:
            j = hl % ROPE_HALF
            e[j, l] = 1.0
            e[ROPE_HALF + j, LANES + l] = -1.0 if hl < ROPE_HALF else 1.0
        else:
            ones[0, l] = 1.0
    return e, ones


def _rmsnorm(x, g):
    ms = jnp.mean(x * x, axis=-1, keepdims=True)
    return x * lax.rsqrt(ms + RMS_EPS) * g


def _split3(x):
    a = x.astype(BF16)
    r = x - a.astype(F32)
    b = r.astype(BF16)
    c = (r - b.astype(F32)).astype(BF16)
    return a, b, c


def _mixer_kernel(sinks_ref, x_ref, cs_ref, erope_ref, ones_ref, g1_ref, win_ref, bin_ref, wab_ref,
                  pw_ref, ps_ref, wpb_ref, wout_ref, g2_ref, wr2_ref, brc_ref,
                  x1_ref, h2_ref, route_ref, cnt_ref,
                  kprev, vprev, uprev, q_scr, attn_scr, ga_scr, gp_scr, *, blocks_per_seq):
    i = pl.program_id(0)
    blk_in_seq = lax.rem(i, blocks_per_seq)
    seq_start = blk_in_seq == 0

    @pl.when(seq_start)
    def _():
        kprev[...] = jnp.zeros_like(kprev)
        vprev[...] = jnp.zeros_like(vprev)
        uprev[...] = jnp.zeros_like(uprev)

    x = x_ref[...]
    h = _rmsnorm(x, g1_ref[...]).astype(BF16)

    c1, c2, c3 = _split3(cs_ref[...])
    er = erope_ref[...]
    tab = (jnp.dot(c1, er, preferred_element_type=F32) + jnp.dot(c2, er, preferred_element_type=F32)
           + jnp.dot(c3, er, preferred_element_type=F32))
    cos_t = tab[:, :LANES] + ones_ref[...]
    sin_t = tab[:, LANES:]
    lane = lax.broadcasted_iota(jnp.int32, (TM, LANES), 1)
    first_half = (lane & (HEAD_DIM - 1)) < ROPE_HALF

    def rope(t):
        partner = jnp.where(first_half, pltpu.roll(t, LANES - ROPE_HALF, 1), pltpu.roll(t, ROPE_HALF, 1))
        return t * cos_t + partner * sin_t

    q = jnp.dot(h, win_ref[:, 0:Q_WIDTH], preferred_element_type=F32) + bin_ref[:, 0:Q_WIDTH]
    for c in range(Q_WIDTH // LANES):
        qc = rope(q[:, c * LANES:(c + 1) * LANES]) * (HEAD_DIM ** -0.5)
        q_scr[:, c * LANES:(c + 1) * LANES] = qc.astype(BF16)

    kvu = jnp.dot(h, win_ref[:, O_K:O_GA], preferred_element_type=F32) + bin_ref[:, O_K:O_GA]
    k = rope(kvu[:, 0:KV_WIDTH])
    v = kvu[:, KV_WIDTH:2 * KV_WIDTH]
    u = kvu[:, 2 * KV_WIDTH:]

    lane_b = lax.broadcasted_iota(jnp.int32, (TM + WINDOW, LANES), 1)
    low = lane_b < HEAD_DIM

    def head_bands(prev_ref, cur):
        band = jnp.concatenate([prev_ref[...], cur], axis=0)
        swapped = pltpu.roll(band, HEAD_DIM, 1)
        zero = jnp.zeros_like(band)
        a0 = jnp.where(low, band, zero).astype(BF16)
        b0 = jnp.where(low, zero, swapped).astype(BF16)
        a1 = jnp.where(low, swapped, zero).astype(BF16)
        b1 = jnp.where(low, zero, band).astype(BF16)
        return ((a0, b0), (a1, b1))

    kb = head_bands(kprev, k)
    vb = head_bands(vprev, v)
    kprev[...] = k[TM - WINDOW:, :]
    vprev[...] = v[TM - WINDOW:, :]

    qi = lax.broadcasted_iota(jnp.int32, (WINDOW, 2 * WINDOW), 0)
    kj = lax.broadcasted_iota(jnp.int32, (WINDOW, 2 * WINDOW), 1)
    mask_mid = jnp.logical_or(jnp.logical_and(kj < WINDOW, kj > qi),
                              jnp.logical_and(kj >= WINDOW, (kj - WINDOW) <= qi))
    first_key = jnp.where(seq_start, WINDOW, 0)
    mask_first = jnp.logical_and(mask_mid, kj >= first_key)

    low_w = lax.broadcasted_iota(jnp.int32, (WINDOW, LANES), 1) < HEAD_DIM
    n_chunks = GROUP // 2
    ones_rows = lax.broadcasted_iota(jnp.int32, (4 * WINDOW, LANES), 0)
    ones_lanes = lax.broadcasted_iota(jnp.int32, (4 * WINDOW, LANES), 1)
    sum_cols = jnp.where((ones_rows < 2 * WINDOW) == (ones_lanes < HEAD_DIM), 1.0, 0.0).astype(BF16)

    pairs = [(j, hkv) for j in range(TM // WINDOW) for hkv in range(N_KV_HEADS)]

    def scores(j, hkv):
        rows = slice(j * WINDOW, (j + 1) * WINDOW)
        band_rows = slice(j * WINDOW, j * WINDOW + 2 * WINDOW)
        q4 = jnp.concatenate(
            [q_scr[rows, (hkv * n_chunks + c) * LANES:(hkv * n_chunks + c + 1) * LANES]
             for c in range(n_chunks)], axis=0)
        kk = jnp.concatenate([kb[hkv][0][band_rows], kb[hkv][1][band_rows]], axis=0)
        return lax.dot_general(q4, kk, NT_DIMS, preferred_element_type=F32)

    def attend(j, hkv, s):
        mask = mask_first if j == 0 else mask_mid
        rows = slice(j * WINDOW, (j + 1) * WINDOW)
        band_rows = slice(j * WINDOW, j * WINDOW + 2 * WINDOW)
        vv = jnp.concatenate([vb[hkv][0][band_rows], vb[hkv][1][band_rows]], axis=0)
        p_rows, sink_terms = [], []
        for c in range(n_chunks):
            p_cols, st = [], []
            for par in range(2):
                sink = sinks_ref[hkv * GROUP + 2 * c + par]
                sc = s[c * WINDOW:(c + 1) * WINDOW, par * 2 * WINDOW:(par + 1) * 2 * WINDOW]
                sc = jnp.where(mask, sc, NEG_INF)
                m = jnp.maximum(jnp.max(sc, axis=-1, keepdims=True), sink)
                p_cols.append(jnp.exp(sc - m).astype(BF16))
                st.append(jnp.exp(sink - m))
            p_rows.append(jnp.concatenate(p_cols, axis=1))
            sink_terms.append(st)
        p = jnp.concatenate(p_rows, axis=0)
        o2 = jnp.dot(p, jnp.concatenate([vv, sum_cols], axis=1), preferred_element_type=F32)
        for c in range(n_chunks):
            den = o2[c * WINDOW:(c + 1) * WINDOW, LANES:] + jnp.where(low_w, sink_terms[c][0], sink_terms[c][1])
            oc = o2[c * WINDOW:(c + 1) * WINDOW, :LANES] / den
            col = (hkv * n_chunks + c) * LANES
            attn_scr[rows, col:col + LANES] = oc.astype(BF16)

    def gate_attn():
        ga_scr[...] = jax.nn.sigmoid(jnp.dot(h, win_ref[:, O_GA:O_GP], preferred_element_type=F32)
                                     + bin_ref[:, O_GA:O_GP])

    def gate_pool():
        gp_scr[...] = jax.nn.sigmoid(jnp.dot(h, win_ref[:, O_GP:], preferred_element_type=F32)
                                     + bin_ref[:, O_GP:])

    def pooling():
        ext = jnp.concatenate([uprev[...], u], axis=0)
        uprev[...] = u[TM - POOL_HALO:, :]
        pos_in_seq = blk_in_seq * TM + lax.broadcasted_iota(jnp.int32, (TM, 1), 0)
        mixed = []
        for g, w in enumerate(POOL_WINDOWS):
            cols = slice(g * POOL_GROUP_WIDTH, (g + 1) * POOL_GROUP_WIDTH)
            sg = ext[:, cols]
            step = 1
            while step < w:
                sg = sg + pltpu.roll(sg, step, 0)
                step *= 2
            count = jnp.minimum(pos_in_seq + 1, w).astype(F32)
            pooled = sg[POOL_HALO:, :] / count - u[:, cols]
            mg = jnp.dot(pooled.astype(BF16), pw_ref[g], preferred_element_type=F32)
            mixed.append((mg * ps_ref[:, cols]).astype(BF16))
        return jnp.dot(jnp.concatenate(mixed, axis=1), wpb_ref[...], preferred_element_type=F32)

    fillers = [gate_attn, gate_pool, pooling, None]
    y_pool = None
    s_next = scores(*pairs[0])
    for n, (j, hkv) in enumerate(pairs):
        s_cur = s_next
        if n + 1 < len(pairs):
            s_next = scores(*pairs[n + 1])
        if fillers[n] is not None:
            out = fillers[n]()
            y_pool = out if out is not None else y_pool
        attend(j, hkv, s_cur)

    y_attn = jnp.dot(attn_scr[...], wab_ref[...], preferred_element_type=F32)
    merged = (ga_scr[...] * y_attn + gp_scr[...] * y_pool).astype(BF16)
    x1 = x + jnp.dot(merged, wout_ref[...], preferred_element_type=F32)
    x1_ref[...] = x1

    h2 = _rmsnorm(x1, g2_ref[...])
    for c in range(ROW_TILE):
        h2_ref[_chunk_of_tokens(c, TM), :] = h2[:, c * LANES:(c + 1) * LANES]

    ha = h2.astype(BF16)
    hb = (h2 - ha.astype(F32)).astype(BF16)
    both = lax.dot_general(wr2_ref[...], ha, NT_DIMS, preferred_element_type=F32)
    logits = (both[:N_EXPERTS] + both[N_EXPERTS:]
              + lax.dot_general(wr2_ref[0:N_EXPERTS, :], hb, NT_DIMS, preferred_element_type=F32)
              + brc_ref[...])
    e_iota = lax.broadcasted_iota(jnp.int32, (N_EXPERTS, TM), 0)
    vals, idxs, hots = [], [], []
    work = logits
    for _ in range(TOP_K):
        m = jnp.max(work, axis=0, keepdims=True)
        ix = jnp.min(jnp.where(work == m, e_iota, N_EXPERTS), axis=0, keepdims=True)
        hot = e_iota == ix
        vals.append(m)
        idxs.append(ix)
        hots.append(hot)
        work = jnp.where(hot, -jnp.inf, work)
    exps = [jnp.exp(vk - vals[0]) for vk in vals]
    den = exps[0] + exps[1] + exps[2] + exps[3]

    picked = jnp.zeros((N_EXPERTS, TM), F32)
    for hot in hots:
        picked = jnp.where(hot, 1.0, picked)
    picked = picked.astype(BF16)
    t_row = lax.broadcasted_iota(jnp.int32, (TM, TM), 0)
    t_col = lax.broadcasted_iota(jnp.int32, (TM, TM), 1)
    earlier = jnp.where(t_row < t_col, 1.0, 0.0).astype(BF16)
    before = jnp.dot(picked, earlier, preferred_element_type=F32)
    cnt_ref[0] = jnp.dot(picked, jnp.ones((TM, LANES), BF16), preferred_element_type=F32).astype(jnp.int32)

    for kk_ in range(TOP_K):
        local_rank = jnp.sum(jnp.where(hots[kk_], before, 0.0), axis=0, keepdims=True)
        route_ref[kk_:kk_ + 1, :] = idxs[kk_].astype(F32)
        route_ref[TOP_K + kk_:TOP_K + kk_ + 1, :] = local_rank
        route_ref[2 * TOP_K + kk_:2 * TOP_K + kk_ + 1, :] = exps[kk_] / den
    route_ref[3 * TOP_K:, :] = jnp.zeros((ROUTE_ROWS - 3 * TOP_K, TM), F32)


def _mixer(x2d, cs, sinks, g1, w_in, b_in, w_ab, pool_w, pool_scale, w_pb, w_out, g2, w_r, b_r, seq_len):
    t = x2d.shape[0]
    erope, ones = _rope_expand_matrix()
    w_rh = w_r.astype(BF16)
    w_rl = (w_r - w_rh.astype(F32)).astype(BF16)
    w_r2 = jnp.concatenate([w_rh.T, w_rl.T], axis=0)
    row = lambda i: (i, 0)
    in_specs = [
        pl.BlockSpec(memory_space=pltpu.SMEM),
        pl.BlockSpec((TM, D_MODEL), row),
        pl.BlockSpec((TM, LANES), row),
        _const_spec((LANES, 2 * LANES)),
        _const_spec((1, LANES)),
        _const_spec((1, D_MODEL)),
        _const_spec((D_MODEL, IN_WIDTH)),
        _const_spec((1, IN_WIDTH)),
        _const_spec((Q_WIDTH, D_MODEL)),
        _const_spec((len(POOL_WINDOWS), POOL_GROUP_WIDTH, POOL_GROUP_WIDTH)),
        _const_spec((1, POOL_WIDTH)),
        _const_spec((POOL_WIDTH, D_MODEL)),
        _const_spec((D_MODEL, D_MODEL)),
        _const_spec((1, D_MODEL)),
        _const_spec((2 * N_EXPERTS, D_MODEL)),
        _const_spec((N_EXPERTS, 1)),
    ]
    out_specs = [pl.BlockSpec((TM, D_MODEL), row), pl.BlockSpec((TM * ROW_TILE, LANES), row),
                 pl.BlockSpec((ROUTE_ROWS, TM), lambda i: (0, i)),
                 pl.BlockSpec((1, N_EXPERTS, LANES), lambda i: (i, 0, 0))]
    out_shape = [jax.ShapeDtypeStruct((t, D_MODEL), F32), jax.ShapeDtypeStruct((t * ROW_TILE, LANES), F32),
                 jax.ShapeDtypeStruct((ROUTE_ROWS, t), F32),
                 jax.ShapeDtypeStruct((t // TM, N_EXPERTS, LANES), jnp.int32)]
    scratch = [pltpu.VMEM((WINDOW, KV_WIDTH), F32), pltpu.VMEM((WINDOW, KV_WIDTH), F32),
               pltpu.VMEM((POOL_HALO, POOL_WIDTH), F32),
               pltpu.VMEM((TM, Q_WIDTH), BF16), pltpu.VMEM((TM, Q_WIDTH), BF16),
               pltpu.VMEM((TM, D_MODEL), F32), pltpu.VMEM((TM, D_MODEL), F32)]
    return pl.pallas_call(
        functools.partial(_mixer_kernel, blocks_per_seq=seq_len // TM),
        grid=(t // TM,),
        in_specs=in_specs, out_specs=out_specs, out_shape=out_shape, scratch_shapes=scratch,
        compiler_params=pltpu.CompilerParams(dimension_semantics=("arbitrary",), vmem_limit_bytes=MIX_VMEM),
        name="mixer",
    )(sinks, x2d, cs, jnp.asarray(erope, BF16), jnp.asarray(ones), g1.reshape(1, -1), w_in.astype(BF16),
      b_in.reshape(1, -1), w_ab.astype(BF16), pool_w.astype(BF16), pool_scale.reshape(1, -1),
      w_pb.astype(BF16), w_out.astype(BF16), g2.reshape(1, -1), w_r2, b_r.reshape(-1, 1))


def _tile_copy(src, src_token, dst, dst_token, sem):
    return pltpu.make_async_copy(src.at[_tile_of_token(src_token)], dst.at[_tile_of_token(dst_token)], sem)


def _dispatch_kernel(dest_ref, pad_ref, h2_ref, xs_ref, zero_tile, sem, *, pads_per_step):
    zero_tile[...] = jnp.zeros_like(zero_tile)
    copies = [_tile_copy(h2_ref, a // TOP_K, xs_ref, dest_ref[0, 0, a], sem) for a in range(TD * TOP_K)]
    copies += [pltpu.make_async_copy(zero_tile, xs_ref.at[_tile_of_token(pad_ref[0, 0, p])], sem)
               for p in range(pads_per_step)]
    for a, cp in enumerate(copies):
        cp.start(priority=a % 2)
    for cp in copies:
        cp.wait()


def _dispatch(h2, dest, pad_slots, n_slots):
    t = h2.shape[0] // ROW_TILE
    n_steps = t // TD
    pads_per_step = pad_slots.shape[0] // n_steps
    assert pads_per_step * n_steps == pad_slots.shape[0]
    return pl.pallas_call(
        functools.partial(_dispatch_kernel, pads_per_step=pads_per_step),
        grid=(n_steps,),
        in_specs=[pl.BlockSpec((1, 1, TD * TOP_K), lambda i: (i, 0, 0), memory_space=pltpu.SMEM),
                  pl.BlockSpec((1, 1, pads_per_step), lambda i: (i, 0, 0), memory_space=pltpu.SMEM),
                  pl.BlockSpec((TD * ROW_TILE, LANES), lambda i: (i, 0))],
        out_specs=pl.BlockSpec(memory_space=pl.ANY),
        out_shape=jax.ShapeDtypeStruct((n_slots * ROW_TILE, LANES), F32),
        scratch_shapes=[pltpu.VMEM((ROW_TILE, LANES), F32), pltpu.SemaphoreType.DMA(())],
        compiler_params=pltpu.CompilerParams(dimension_semantics=("arbitrary",), has_side_effects=True),
        name="dispatch",
    )(dest.reshape(n_steps, 1, TD * TOP_K), pad_slots.reshape(n_steps, 1, pads_per_step), h2)


def _experts_kernel(be_ref, first_ref, rows_ref, next_ref, par_ref, xs_ref, bup_ref, bdn_ref, wup_hbm, wdn_hbm,
                    ys_ref, wup_f32, wdn_f32, wup_bf, wdn_bf, sems):
    i = pl.program_id(0)
    cast_rows = 128
    n_rows = rows_ref[i]

    def weight_copies(expert, par):
        return (pltpu.make_async_copy(wup_hbm.at[expert], wup_f32.at[par], sems.at[0, par]),
                pltpu.make_async_copy(wdn_hbm.at[expert], wdn_f32.at[par], sems.at[1, par]))

    @pl.when(first_ref[i] == 1)
    def _():
        expert, par, nxt = be_ref[i], par_ref[i], next_ref[i]

        @pl.when(i == 0)
        def _():
            for cp in weight_copies(expert, par):
                cp.start()

        for cp in weight_copies(expert, par):
            cp.wait()

        @pl.when(nxt >= 0)
        def _():
            for cp in weight_copies(nxt, 1 - par):
                cp.start()

        def cast_up(r, carry):
            rows = pl.ds(pl.multiple_of(r * cast_rows, cast_rows), cast_rows)
            wup_bf[rows, :] = wup_f32[par, rows, :].astype(BF16)
            return carry

        def cast_dn(r, carry):
            rows = pl.ds(pl.multiple_of(r * cast_rows, cast_rows), cast_rows)
            wdn_bf[rows, :] = wdn_f32[par, rows, :].astype(BF16)
            return carry

        lax.fori_loop(0, D_MODEL // cast_rows, cast_up, 0)
        lax.fori_loop(0, D_FF // cast_rows, cast_dn, 0)

    @pl.when(n_rows > 0)
    def _():
        xb = jnp.concatenate([xs_ref[_chunk_of_tokens(c, BM), :].astype(BF16) for c in range(ROW_TILE)],
                             axis=1)
        up = jnp.dot(xb, wup_bf[...], preferred_element_type=F32) + bup_ref[0]
        glu = jnp.minimum(up[:, :D_FF], SWIGLU_LIMIT)
        lin = jnp.clip(up[:, D_FF:], -SWIGLU_LIMIT, SWIGLU_LIMIT)
        act = glu * jax.nn.sigmoid(SWIGLU_ALPHA * glu) * (lin + 1.0)
        y = jnp.dot(act.astype(BF16), wdn_bf[...], preferred_element_type=F32) + bdn_ref[0]
        for c in range(ROW_TILE):
            ys_ref[_chunk_of_tokens(c, BM), :] = y[:, c * LANES:(c + 1) * LANES]

    @pl.when(n_rows == 0)
    def _():
        ys_ref[...] = jnp.zeros_like(ys_ref)


def _experts(xs, blk_expert, blk_first, blk_rows, blk_next, blk_par, w_up, b_up, w_down, b_down):
    n_slots = xs.shape[0] // ROW_TILE
    n_pref = 5
    slot_blk = lambda i, *_: (i, 0)
    expert_blk = lambda i, be, *_: (be[i], 0, 0)
    grid_spec = pltpu.PrefetchScalarGridSpec(
        num_scalar_prefetch=n_pref,
        grid=(n_slots // BM,),
        in_specs=[
            pl.BlockSpec((BM * ROW_TILE, LANES), slot_blk),
            pl.BlockSpec((1, 1, 2 * D_FF), expert_blk),
            pl.BlockSpec((1, 1, D_MODEL), expert_blk),
            pl.BlockSpec(memory_space=pl.ANY),
            pl.BlockSpec(memory_space=pl.ANY),
        ],
        out_specs=pl.BlockSpec((BM * ROW_TILE, LANES), slot_blk),
        scratch_shapes=[pltpu.VMEM((2, D_MODEL, 2 * D_FF), F32), pltpu.VMEM((2, D_FF, D_MODEL), F32),
                        pltpu.VMEM((D_MODEL, 2 * D_FF), BF16), pltpu.VMEM((D_FF, D_MODEL), BF16),
                        pltpu.SemaphoreType.DMA((2, 2))],
    )
    return pl.pallas_call(
        _experts_kernel,
        grid_spec=grid_spec,
        out_shape=jax.ShapeDtypeStruct((n_slots * ROW_TILE, LANES), F32),
        compiler_params=pltpu.CompilerParams(dimension_semantics=("arbitrary",), vmem_limit_bytes=MOE_VMEM),
        name="experts",
    )(blk_expert, blk_first, blk_rows, blk_next, blk_par, xs, b_up.reshape(N_EXPERTS, 1, -1),
      b_down.reshape(N_EXPERTS, 1, -1), w_up, w_down)


def _combine_kernel(dest_ref, dest_next_ref, x1_ref, gate_ref, gf_ref, ys_ref, out_ref, buf, sems):
    i = pl.program_id(0)
    slot = lax.rem(i, 2)

    def gather(d_ref, s):
        return [_tile_copy(ys_ref, d_ref[0, 0, a], buf.at[s, a % TOP_K], a // TOP_K, sems.at[s])
                for a in range(TB * TOP_K)]

    @pl.when(i == 0)
    def _():
        for a, cp in enumerate(gather(dest_ref, slot)):
            cp.start(priority=a % 2)

    @pl.when(i + 1 < pl.num_programs(0))
    def _():
        for a, cp in enumerate(gather(dest_next_ref, 1 - slot)):
            cp.start(priority=a % 2)

    for cp in gather(dest_ref, slot):
        cp.wait()

    cols = []
    for c in range(ROW_TILE):
        acc = x1_ref[:, c * LANES:(c + 1) * LANES]
        for k in range(TOP_K):
            acc = acc + gate_ref[:, k:k + 1] * buf[slot, k, _chunk_of_tokens(c, TB), :]
        cols.append(acc)
    out_ref[...] = _rmsnorm(jnp.concatenate(cols, axis=1), gf_ref[...])


def _combine(x1, gates, ys, dest, gf):
    t = x1.shape[0]
    n_blk = t // TB
    dest_blocks = dest.reshape(n_blk, 1, TB * TOP_K)
    return pl.pallas_call(
        _combine_kernel,
        grid=(n_blk,),
        in_specs=[pl.BlockSpec((1, 1, TB * TOP_K), lambda i: (i, 0, 0), memory_space=pltpu.SMEM),
                  pl.BlockSpec((1, 1, TB * TOP_K), lambda i: (jnp.minimum(i + 1, n_blk - 1), 0, 0),
                               memory_space=pltpu.SMEM),
                  pl.BlockSpec((TB, D_MODEL), lambda i: (i, 0)),
                  pl.BlockSpec((TB, TOP_K), lambda i: (i, 0)),
                  _const_spec((1, D_MODEL)),
                  pl.BlockSpec(memory_space=pl.ANY)],
        out_specs=pl.BlockSpec((TB, D_MODEL), lambda i: (i, 0)),
        out_shape=jax.ShapeDtypeStruct((t, D_MODEL), F32),
        scratch_shapes=[pltpu.VMEM((2, TOP_K, TB * ROW_TILE, LANES), F32), pltpu.SemaphoreType.DMA((2,))],
        compiler_params=pltpu.CompilerParams(dimension_semantics=("arbitrary",)),
        name="combine",
    )(dest_blocks, dest_blocks, x1, gates, gf.reshape(1, -1), ys)


def _count_le(sorted_ends, queries):
    return jnp.sum(sorted_ends[None, :] <= queries[:, None], axis=1, dtype=jnp.int32)


def _routing(top_idx_t, local_rank_t, blk_counts):
    t = top_idx_t.shape[1]
    n_slots = t * TOP_K + N_EXPERTS * BM
    n_blocks = n_slots // BM
    counts = jnp.sum(blk_counts, axis=0)
    padded = (counts + BM - 1) // BM * BM
    pend = jnp.cumsum(padded)
    pstart = pend - padded
    base = pstart[None, :] + jnp.cumsum(blk_counts, axis=0) - blk_counts
    base_tok_t = jnp.repeat(base.T, TM, axis=1)
    experts = jnp.arange(N_EXPERTS, dtype=jnp.int32)[None, :, None]
    dest_t = local_rank_t + jnp.sum(jnp.where(top_idx_t[:, None, :] == experts, base_tok_t[None], 0), axis=1)
    blk_start = jnp.arange(n_blocks, dtype=jnp.int32) * BM
    blk_expert = jnp.minimum(_count_le(pend, blk_start), N_EXPERTS - 1)
    blk_rows = jnp.clip((pstart + counts)[blk_expert] - blk_start, 0, BM).astype(jnp.int32)
    changed = jnp.concatenate([jnp.ones((1,), jnp.int32), (blk_expert[1:] != blk_expert[:-1]).astype(jnp.int32)])
    blk_first = (changed * (blk_rows > 0)).astype(jnp.int32)
    blk_par = (jnp.cumsum(blk_first) - 1) % 2
    ids = jnp.arange(N_EXPERTS, dtype=jnp.int32)
    later_used = (counts > 0)[None, :] & (ids[None, :] > ids[:, None])
    next_used = jnp.min(jnp.where(later_used, ids[None, :], N_EXPERTS), axis=1)
    next_used = jnp.where(next_used == N_EXPERTS, -1, next_used)
    blk_next = next_used[blk_expert]
    seg_size = jnp.concatenate([padded - counts, n_slots - pend[-1:]])
    seg_slot = jnp.concatenate([pstart + counts, pend[-1:]])
    seg_end = jnp.cumsum(seg_size)
    ordinal = jnp.arange(n_slots - t * TOP_K, dtype=jnp.int32)
    seg = _count_le(seg_end, ordinal)
    pad_slots = seg_slot[seg] + ordinal - (seg_end - seg_size)[seg]
    i32 = lambda a: a.astype(jnp.int32)
    return (i32(dest_t.T), i32(pad_slots), i32(blk_expert), blk_first, blk_rows, i32(blk_next), i32(blk_par),
            n_slots)


def kernel(x, positions, norm1_g, w_in, b_in, attn_sinks, w_attn_br, pool_w, pool_scale, w_pool_br, w_out,
           norm2_g, w_router, b_router, w_up, b_up, w_down, b_down, normf_g):
    b, s, d = x.shape
    t = b * s
    assert norm1_g.shape[0] == 1, "single-layer problem: the combine kernel applies the final RMSNorm"
    assert d == D_MODEL and s % TM == 0 and t % ROPE_TN == 0 and t % TD == 0 and t % TB == 0
    inv_freq = jnp.power(jnp.float32(ROPE_THETA), -jnp.arange(ROPE_HALF, dtype=F32) * (2.0 / ROPE_DIM))
    cs = _rope_table(positions.reshape(1, t), inv_freq).T
    cs = jnp.pad(cs, ((0, 0), (0, LANES - 2 * ROPE_HALF)))
    x1, h2, route, blk_counts = _mixer(
        x.reshape(t, d), cs, attn_sinks[0], norm1_g[0], w_in[0], b_in[0], w_attn_br[0], pool_w[0],
        pool_scale[0], w_pool_br[0], w_out[0], norm2_g[0], w_router[0], b_router[0], s)
    top_idx_t = route[0:TOP_K].astype(jnp.int32)
    local_rank_t = route[TOP_K:2 * TOP_K].astype(jnp.int32)
    gates = route[2 * TOP_K:3 * TOP_K].T
    dest, pad_slots, blk_expert, blk_first, blk_rows, blk_next, blk_par, n_slots = _routing(
        top_idx_t, local_rank_t, blk_counts[:, :, 0])
    xs = _dispatch(h2, dest, pad_slots, n_slots)
    ys = _experts(xs, blk_expert, blk_first, blk_rows, blk_next, blk_par, w_up[0], b_up[0], w_down[0], b_down[0])
    out = _combine(x1, gates, ys, dest, normf_g)
    return out.reshape(b, s, d)
```

```python
import functools

import numpy as np
import jax
import jax.numpy as jnp
from jax import lax
from jax.experimental import pallas as pl
from jax.experimental.pallas import tpu as pltpu

F32 = jnp.float32
BF16 = jnp.bfloat16

D_MODEL = 1024
HEAD_DIM = 64
N_Q_HEADS = 16
N_KV_HEADS = 2
GROUP = N_Q_HEADS // N_KV_HEADS
WINDOW = 128
ROPE_THETA = 500000.0
ROPE_DIM = HEAD_DIM // 4
ROPE_HALF = ROPE_DIM // 2
Q_WIDTH = N_Q_HEADS * HEAD_DIM
KV_WIDTH = N_KV_HEADS * HEAD_DIM
NEG_INF = -1e30
POOL_WINDOWS = (2, 4, 8, 16)
POOL_WIDTH = D_MODEL // 2
POOL_GROUP_WIDTH = POOL_WIDTH // len(POOL_WINDOWS)
POOL_HALO = 16
IN_WIDTH = Q_WIDTH + 2 * KV_WIDTH + POOL_WIDTH + 2 * D_MODEL
O_K = Q_WIDTH
O_V = O_K + KV_WIDTH
O_P = O_V + KV_WIDTH
O_GA = O_P + POOL_WIDTH
O_GP = O_GA + D_MODEL
N_EXPERTS = 32
TOP_K = 4
D_FF = D_MODEL
SWIGLU_ALPHA = 1.702
SWIGLU_LIMIT = 7.0
RMS_EPS = 1e-5

LANES = 128
ROW_TILE = D_MODEL // LANES
ROUTE_ROWS = 16
TM = 256
TD = 512
TB = 128
BM = 256
OUT_TILE = 256
ROPE_TN = 2048
MIX_VMEM = 56 * 1024 * 1024
MOE_VMEM = 58 * 1024 * 1024
NT_DIMS = (((1,), (1,)), ((), ()))


def _const_spec(shape):
    return pl.BlockSpec(shape, lambda *_: (0,) * len(shape))


def _chunk_of_tokens(c, n_tokens):
    return pl.ds(c, n_tokens, stride=ROW_TILE)


def _tile_of_token(token):
    if isinstance(token, int):
        return pl.ds(token * ROW_TILE, ROW_TILE)
    return pl.ds(pl.multiple_of(token * ROW_TILE, ROW_TILE), ROW_TILE)


def _rope_table_kernel(pos_ref, invf_ref, out_ref):
    ang = invf_ref[...] * pos_ref[...].astype(F32)
    out_ref[0:ROPE_HALF, :] = jnp.cos(ang)
    out_ref[ROPE_HALF:, :] = jnp.sin(ang)


def _rope_table(pos_row, inv_freq):
    t = pos_row.shape[1]
    return pl.pallas_call(
        _rope_table_kernel,
        grid=(t // ROPE_TN,),
        in_specs=[pl.BlockSpec((1, ROPE_TN), lambda i: (0, i)), _const_spec((ROPE_HALF, 1))],
        out_specs=pl.BlockSpec((2 * ROPE_HALF, ROPE_TN), lambda i: (0, i)),
        out_shape=jax.ShapeDtypeStruct((2 * ROPE_HALF, t), F32),
        name="rope_table",
    )(pos_row, inv_freq.reshape(ROPE_HALF, 1))


def _rope_expand_matrix():
    e = np.zeros((LANES, 2 * LANES), np.float32)
    ones = np.zeros((1, LANES), np.float32)
    for l in range(LANES):
        hl = l % HEAD_DIM
        if hl < ROPE_DIM:
            j = hl % ROPE_HALF
            e[j, l] = 1.0
            e[ROPE_HALF + j, LANES + l] = -1.0 if hl < ROPE_HALF else 1.0
        else:
            ones[0, l] = 1.0
    return e, ones


def _rmsnorm(x, g):
    ms = jnp.mean(x * x, axis=-1, keepdims=True)
    return x * lax.rsqrt(ms + RMS_EPS) * g


def _split3(x):
    a = x.astype(BF16)
    r = x - a.astype(F32)
    b = r.astype(BF16)
    c = (r - b.astype(F32)).astype(BF16)
    return a, b, c


def _mixer_kernel(sinks_ref, x_ref, cs_ref, erope_ref, ones_ref, g1_ref, win_ref, bin_ref, wab_ref,
                  pw_ref, ps_ref, wpb_ref, wout_ref, g2_ref, wr2_ref, brc_ref,
                  x1_ref, h2_ref, route_ref, cnt_ref,
                  kprev, vprev, uprev, q_scr, attn_scr, ga_scr, gp_scr, *, blocks_per_seq):
    i = pl.program_id(0)
    blk_in_seq = lax.rem(i, blocks_per_seq)
    seq_start = blk_in_seq == 0

    @pl.when(seq_start)
    def _():
        kprev[...] = jnp.zeros_like(kprev)
        vprev[...] = jnp.zeros_like(vprev)
        uprev[...] = jnp.zeros_like(uprev)

    x = x_ref[...]
    h = _rmsnorm(x, g1_ref[...]).astype(BF16)

    c1, c2, c3 = _split3(cs_ref[...])
    er = erope_ref[...]
    tab = (jnp.dot(c1, er, preferred_element_type=F32) + jnp.dot(c2, er, preferred_element_type=F32)
           + jnp.dot(c3, er, preferred_element_type=F32))
    cos_t = tab[:, :LANES] + ones_ref[...]
    sin_t = tab[:, LANES:]
    lane = lax.broadcasted_iota(jnp.int32, (TM, LANES), 1)
    first_half = (lane & (HEAD_DIM - 1)) < ROPE_HALF

    def rope(t):
        partner = jnp.where(first_half, pltpu.roll(t, LANES - ROPE_HALF, 1), pltpu.roll(t, ROPE_HALF, 1))
        return t * cos_t + partner * sin_t

    q = jnp.dot(h, win_ref[:, 0:Q_WIDTH], preferred_element_type=F32) + bin_ref[:, 0:Q_WIDTH]
    for c in range(Q_WIDTH // LANES):
        qc = rope(q[:, c * LANES:(c + 1) * LANES]) * (HEAD_DIM ** -0.5)
        q_scr[:, c * LANES:(c + 1) * LANES] = qc.astype(BF16)

    kvu = jnp.dot(h, win_ref[:, O_K:O_GA], preferred_element_type=F32) + bin_ref[:, O_K:O_GA]
    k = rope(kvu[:, 0:KV_WIDTH])
    v = kvu[:, KV_WIDTH:2 * KV_WIDTH]
    u = kvu[:, 2 * KV_WIDTH:]

    lane_b = lax.broadcasted_iota(jnp.int32, (TM + WINDOW, LANES), 1)
    low = lane_b < HEAD_DIM

    def head_bands(prev_ref, cur):
        band = jnp.concatenate([prev_ref[...], cur], axis=0)
        swapped = pltpu.roll(band, HEAD_DIM, 1)
        zero = jnp.zeros_like(band)
        a0 = jnp.where(low, band, zero).astype(BF16)
        b0 = jnp.where(low, zero, swapped).astype(BF16)
        a1 = jnp.where(low, swapped, zero).astype(BF16)
        b1 = jnp.where(low, zero, band).astype(BF16)
        return ((a0, b0), (a1, b1))

    kb = head_bands(kprev, k)
    vb = head_bands(vprev, v)
    kprev[...] = k[TM - WINDOW:, :]
    vprev[...] = v[TM - WINDOW:, :]

    qi = lax.broadcasted_iota(jnp.int32, (WINDOW, 2 * WINDOW), 0)
    kj = lax.broadcasted_iota(jnp.int32, (WINDOW, 2 * WINDOW), 1)
    mask_mid = jnp.logical_or(jnp.logical_and(kj < WINDOW, kj > qi),
                              jnp.logical_and(kj >= WINDOW, (kj - WINDOW) <= qi))
    first_key = jnp.where(seq_start, WINDOW, 0)
    mask_first = jnp.logical_and(mask_mid, kj >= first_key)

    low_w = lax.broadcasted_iota(jnp.int32, (WINDOW, LANES), 1) < HEAD_DIM
    n_chunks = GROUP // 2
    ones_rows = lax.broadcasted_iota(jnp.int32, (4 * WINDOW, LANES), 0)
    ones_lanes = lax.broadcasted_iota(jnp.int32, (4 * WINDOW, LANES), 1)
    sum_cols = jnp.where((ones_rows < 2 * WINDOW) == (ones_lanes < HEAD_DIM), 1.0, 0.0).astype(BF16)

    pairs = [(j, hkv) for j in range(TM // WINDOW) for hkv in range(N_KV_HEADS)]

    def scores(j, hkv):
        rows = slice(j * WINDOW, (j + 1) * WINDOW)
        band_rows = slice(j * WINDOW, j * WINDOW + 2 * WINDOW)
        q4 = jnp.concatenate(
            [q_scr[rows, (hkv * n_chunks + c) * LANES:(hkv * n_chunks + c + 1) * LANES]
             for c in range(n_chunks)], axis=0)
        kk = jnp.concatenate([kb[hkv][0][band_rows], kb[hkv][1][band_rows]], axis=0)
        return lax.dot_general(q4, kk, NT_DIMS, preferred_element_type=F32)

    def attend(j, hkv, s):
        mask = mask_first if j == 0 else mask_mid
        rows = slice(j * WINDOW, (j + 1) * WINDOW)
        band_rows = slice(j * WINDOW, j * WINDOW + 2 * WINDOW)
        vv = jnp.concatenate([vb[hkv][0][band_rows], vb[hkv][1][band_rows]], axis=0)
        p_rows, sink_terms = [], []
        for c in range(n_chunks):
            p_cols, st = [], []
            for par in range(2):
                sink = sinks_ref[hkv * GROUP + 2 * c + par]
                sc = s[c * WINDOW:(c + 1) * WINDOW, par * 2 * WINDOW:(par + 1) * 2 * WINDOW]
                sc = jnp.where(mask, sc, NEG_INF)
                m = jnp.maximum(jnp.max(sc, axis=-1, keepdims=True), sink)
                p_cols.append(jnp.exp(sc - m).astype(BF16))
                st.append(jnp.exp(sink - m))
            p_rows.append(jnp.concatenate(p_cols, axis=1))
            sink_terms.append(st)
        p = jnp.concatenate(p_rows, axis=0)
        o2 = jnp.dot(p, jnp.concatenate([vv, sum_cols], axis=1), preferred_element_type=F32)
        for c in range(n_chunks):
            den = o2[c * WINDOW:(c + 1) * WINDOW, LANES:] + jnp.where(low_w, sink_terms[c][0], sink_terms[c][1])
            oc = o2[c * WINDOW:(c + 1) * WINDOW, :LANES] / den
            col = (hkv * n_chunks + c) * LANES
            attn_scr[rows, col:col + LANES] = oc.astype(BF16)

    def gate_attn():
        ga_scr[...] = jax.nn.sigmoid(jnp.dot(h, win_ref[:, O_GA:O_GP], preferred_element_type=F32)
                                     + bin_ref[:, O_GA:O_GP])

    def gate_pool():
        gp_scr[...] = jax.nn.sigmoid(jnp.dot(h, win_ref[:, O_GP:], preferred_element_type=F32)
                                     + bin_ref[:, O_GP:])

    def pooling():
        ext = jnp.concatenate([uprev[...], u], axis=0)
        uprev[...] = u[TM - POOL_HALO:, :]
        pos_in_seq = blk_in_seq * TM + lax.broadcasted_iota(jnp.int32, (TM, 1), 0)
        mixed = []
        for g, w in enumerate(POOL_WINDOWS):
            cols = slice(g * POOL_GROUP_WIDTH, (g + 1) * POOL_GROUP_WIDTH)
            sg = ext[:, cols]
            step = 1
            while step < w:
                sg = sg + pltpu.roll(sg, step, 0)
                step *= 2
            count = jnp.minimum(pos_in_seq + 1, w).astype(F32)
            pooled = sg[POOL_HALO:, :] / count - u[:, cols]
            mg = jnp.dot(pooled.astype(BF16), pw_ref[g], preferred_element_type=F32)
            mixed.append((mg * ps_ref[:, cols]).astype(BF16))
        return jnp.dot(jnp.concatenate(mixed, axis=1), wpb_ref[...], preferred_element_type=F32)

    fillers = [gate_attn, gate_pool, pooling, None]
    y_pool = None
    s_next = scores(*pairs[0])
    for n, (j, hkv) in enumerate(pairs):
        s_cur = s_next
        if n + 1 < len(pairs):
            s_next = scores(*pairs[n + 1])
        if fillers[n] is not None:
            out = fillers[n]()
            y_pool = out if out is not None else y_pool
        attend(j, hkv, s_cur)

    y_attn = jnp.dot(attn_scr[...], wab_ref[...], preferred_element_type=F32)
    merged = (ga_scr[...] * y_attn + gp_scr[...] * y_pool).astype(BF16)
    x1 = x + jnp.dot(merged, wout_ref[...], preferred_element_type=F32)
    x1_ref[...] = x1

    h2 = _rmsnorm(x1, g2_ref[...])
    for c in range(ROW_TILE):
        h2_ref[_chunk_of_tokens(c, TM), :] = h2[:, c * LANES:(c + 1) * LANES]

    ha = h2.astype(BF16)
    hb = (h2 - ha.astype(F32)).astype(BF16)
    both = lax.dot_general(wr2_ref[...], ha, NT_DIMS, preferred_element_type=F32)
    logits = (both[:N_EXPERTS] + both[N_EXPERTS:]
              + lax.dot_general(wr2_ref[0:N_EXPERTS, :], hb, NT_DIMS, preferred_element_type=F32)
              + brc_ref[...])
    e_iota = lax.broadcasted_iota(jnp.int32, (N_EXPERTS, TM), 0)
    vals, idxs, hots = [], [], []
    work = logits
    for _ in range(TOP_K):
        m = jnp.max(work, axis=0, keepdims=True)
        ix = jnp.min(jnp.where(work == m, e_iota, N_EXPERTS), axis=0, keepdims=True)
        hot = e_iota == ix
        vals.append(m)
        idxs.append(ix)
        hots.append(hot)
        work = jnp.where(hot, -jnp.inf, work)
    exps = [jnp.exp(vk - vals[0]) for vk in vals]
    den = exps[0] + exps[1] + exps[2] + exps[3]

    picked = jnp.zeros((N_EXPERTS, TM), F32)
    for hot in hots:
        picked = jnp.where(hot, 1.0, picked)
    picked = picked.astype(BF16)
    t_row = lax.broadcasted_iota(jnp.int32, (TM, TM), 0)
    t_col = lax.broadcasted_iota(jnp.int32, (TM, TM), 1)
    earlier = jnp.where(t_row < t_col, 1.0, 0.0).astype(BF16)
    before = jnp.dot(picked, earlier, preferred_element_type=F32)
    cnt_ref[0] = jnp.dot(picked, jnp.ones((TM, LANES), BF16), preferred_element_type=F32).astype(jnp.int32)

    for kk_ in range(TOP_K):
        local_rank = jnp.sum(jnp.where(hots[kk_], before, 0.0), axis=0, keepdims=True)
        route_ref[kk_:kk_ + 1, :] = idxs[kk_].astype(F32)
        route_ref[TOP_K + kk_:TOP_K + kk_ + 1, :] = local_rank
        route_ref[2 * TOP_K + kk_:2 * TOP_K + kk_ + 1, :] = exps[kk_] / den
    route_ref[3 * TOP_K:, :] = jnp.zeros((ROUTE_ROWS - 3 * TOP_K, TM), F32)


def _mixer(x2d, cs, sinks, g1, w_in, b_in, w_ab, pool_w, pool_scale, w_pb, w_out, g2, w_r, b_r, seq_len):
    t = x2d.shape[0]
    erope, ones = _rope_expand_matrix()
    w_rh = w_r.astype(BF16)
    w_rl = (w_r - w_rh.astype(F32)).astype(BF16)
    w_r2 = jnp.concatenate([w_rh.T, w_rl.T], axis=0)
    row = lambda i: (i, 0)
    in_specs = [
        pl.BlockSpec(memory_space=pltpu.SMEM),
        pl.BlockSpec((TM, D_MODEL), row),
        pl.BlockSpec((TM, LANES), row),
        _const_spec((LANES, 2 * LANES)),
        _const_spec((1, LANES)),
        _const_spec((1, D_MODEL)),
        _const_spec((D_MODEL, IN_WIDTH)),
        _const_spec((1, IN_WIDTH)),
        _const_spec((Q_WIDTH, D_MODEL)),
        _const_spec((len(POOL_WINDOWS), POOL_GROUP_WIDTH, POOL_GROUP_WIDTH)),
        _const_spec((1, POOL_WIDTH)),
        _const_spec((POOL_WIDTH, D_MODEL)),
        _const_spec((D_MODEL, D_MODEL)),
        _const_spec((1, D_MODEL)),
        _const_spec((2 * N_EXPERTS, D_MODEL)),
        _const_spec((N_EXPERTS, 1)),
    ]
    out_specs = [pl.BlockSpec((TM, D_MODEL), row), pl.BlockSpec((TM * ROW_TILE, LANES), row),
                 pl.BlockSpec((ROUTE_ROWS, TM), lambda i: (0, i)),
                 pl.BlockSpec((1, N_EXPERTS, LANES), lambda i: (i, 0, 0))]
    out_shape = [jax.ShapeDtypeStruct((t, D_MODEL), F32), jax.ShapeDtypeStruct((t * ROW_TILE, LANES), F32),
                 jax.ShapeDtypeStruct((ROUTE_ROWS, t), F32),
                 jax.ShapeDtypeStruct((t // TM, N_EXPERTS, LANES), jnp.int32)]
    scratch = [pltpu.VMEM((WINDOW, KV_WIDTH), F32), pltpu.VMEM((WINDOW, KV_WIDTH), F32),
               pltpu.VMEM((POOL_HALO, POOL_WIDTH), F32),
               pltpu.VMEM((TM, Q_WIDTH), BF16), pltpu.VMEM((TM, Q_WIDTH), BF16),
               pltpu.VMEM((TM, D_MODEL), F32), pltpu.VMEM((TM, D_MODEL), F32)]
    return pl.pallas_call(
        functools.partial(_mixer_kernel, blocks_per_seq=seq_len // TM),
        grid=(t // TM,),
        in_specs=in_specs, out_specs=out_specs, out_shape=out_shape, scratch_shapes=scratch,
        compiler_params=pltpu.CompilerParams(dimension_semantics=("arbitrary",), vmem_limit_bytes=MIX_VMEM),
        name="mixer",
    )(sinks, x2d, cs, jnp.asarray(erope, BF16), jnp.asarray(ones), g1.reshape(1, -1), w_in.astype(BF16),
      b_in.reshape(1, -1), w_ab.astype(BF16), pool_w.astype(BF16), pool_scale.reshape(1, -1),
      w_pb.astype(BF16), w_out.astype(BF16), g2.reshape(1, -1), w_r2, b_r.reshape(-1, 1))


def _tile_copy(src, src_token, dst, dst_token, sem):
    return pltpu.make_async_copy(src.at[_tile_of_token(src_token)], dst.at[_tile_of_token(dst_token)], sem)


def _dispatch_kernel(dest_ref, pad_ref, h2_ref, xs_ref, zero_tile, sem, *, pads_per_step):
    zero_tile[...] = jnp.zeros_like(zero_tile)
    copies = [_tile_copy(h2_ref, r, xs_ref, dest_ref[k, r], sem) for r in range(TD) for k in range(TOP_K)]
    copies += [pltpu.make_async_copy(zero_tile, xs_ref.at[_tile_of_token(pad_ref[0, p])], sem)
               for p in range(pads_per_step)]
    for a, cp in enumerate(copies):
        cp.start(priority=a % 2)
    for cp in copies:
        cp.wait()


def _dispatch(h2, dest_t, pad_slots, n_slots):
    t = h2.shape[0] // ROW_TILE
    n_steps = t // TD
    pads_per_step = pad_slots.shape[0] // n_steps
    assert pads_per_step * n_steps == pad_slots.shape[0]
    return pl.pallas_call(
        functools.partial(_dispatch_kernel, pads_per_step=pads_per_step),
        grid=(n_steps,),
        in_specs=[pl.BlockSpec((TOP_K, TD), lambda i: (0, i), memory_space=pltpu.SMEM),
                  pl.BlockSpec((1, pads_per_step), lambda i: (0, i), memory_space=pltpu.SMEM),
                  pl.BlockSpec((TD * ROW_TILE, LANES), lambda i: (i, 0))],
        out_specs=pl.BlockSpec(memory_space=pl.ANY),
        out_shape=jax.ShapeDtypeStruct((n_slots * ROW_TILE, LANES), F32),
        scratch_shapes=[pltpu.VMEM((ROW_TILE, LANES), F32), pltpu.SemaphoreType.DMA(())],
        compiler_params=pltpu.CompilerParams(dimension_semantics=("arbitrary",), has_side_effects=True),
        name="dispatch",
    )(dest_t, pad_slots.reshape(1, -1), h2)


def _experts_kernel(be_ref, first_ref, rows_ref, next_ref, par_ref, xs_ref, bup_ref, bdn_ref, wup_hbm, wdn_hbm,
                    ys_ref, wup_f32, wdn_f32, wup_bf, wdn_bf, sems):
    i = pl.program_id(0)
    cast_rows = 128
    n_rows = rows_ref[i]

    def weight_copies(expert, par):
        return (pltpu.make_async_copy(wup_hbm.at[expert], wup_f32.at[par], sems.at[0, par]),
                pltpu.make_async_copy(wdn_hbm.at[expert], wdn_f32.at[par], sems.at[1, par]))

    @pl.when(first_ref[i] == 1)
    def _():
        expert, par, nxt = be_ref[i], par_ref[i], next_ref[i]

        @pl.when(i == 0)
        def _():
            for cp in weight_copies(expert, par):
                cp.start()

        for cp in weight_copies(expert, par):
            cp.wait()

        @pl.when(nxt >= 0)
        def _():
            for cp in weight_copies(nxt, 1 - par):
                cp.start()

        def cast_up(r, carry):
            rows = pl.ds(pl.multiple_of(r * cast_rows, cast_rows), cast_rows)
            wup_bf[rows, :] = wup_f32[par, rows, :].astype(BF16)
            return carry

        def cast_dn(r, carry):
            rows = pl.ds(pl.multiple_of(r * cast_rows, cast_rows), cast_rows)
            wdn_bf[rows, :] = wdn_f32[par, rows, :].astype(BF16)
            return carry

        lax.fori_loop(0, D_MODEL // cast_rows, cast_up, 0)
        lax.fori_loop(0, D_FF // cast_rows, cast_dn, 0)

    @pl.when(n_rows > 0)
    def _():
        xb = jnp.concatenate([xs_ref[_chunk_of_tokens(c, BM), :].astype(BF16) for c in range(ROW_TILE)],
                             axis=1)
        up = jnp.dot(xb, wup_bf[...], preferred_element_type=F32) + bup_ref[0]
        glu = jnp.minimum(up[:, :D_FF], SWIGLU_LIMIT)
        lin = jnp.clip(up[:, D_FF:], -SWIGLU_LIMIT, SWIGLU_LIMIT)
        act = (glu * jax.nn.sigmoid(SWIGLU_ALPHA * glu) * (lin + 1.0)).astype(BF16)
        chunks_per_tile = OUT_TILE // LANES
        for n in range(D_MODEL // OUT_TILE):
            cols = slice(n * OUT_TILE, (n + 1) * OUT_TILE)
            y = jnp.dot(act, wdn_bf[:, cols], preferred_element_type=F32) + bdn_ref[0, :, cols]
            for c in range(chunks_per_tile):
                ys_ref[_chunk_of_tokens(n * chunks_per_tile + c, BM), :] = y[:, c * LANES:(c + 1) * LANES]

    @pl.when(n_rows == 0)
    def _():
        ys_ref[...] = jnp.zeros_like(ys_ref)


def _experts(xs, blk_expert, blk_first, blk_rows, blk_next, blk_par, w_up, b_up, w_down, b_down):
    n_blocks = xs.shape[0] // (ROW_TILE * BM)
    slot_blk = lambda i, *_: (i, 0)
    expert_blk = lambda i, be, *_: (be[i], 0, 0)
    grid_spec = pltpu.PrefetchScalarGridSpec(
        num_scalar_prefetch=5,
        grid=(n_blocks,),
        in_specs=[
            pl.BlockSpec((BM * ROW_TILE, LANES), slot_blk),
            pl.BlockSpec((1, 1, 2 * D_FF), expert_blk),
            pl.BlockSpec((1, 1, D_MODEL), expert_blk),
            pl.BlockSpec(memory_space=pl.ANY),
            pl.BlockSpec(memory_space=pl.ANY),
        ],
        out_specs=pl.BlockSpec((BM * ROW_TILE, LANES), slot_blk),
        scratch_shapes=[pltpu.VMEM((2, D_MODEL, 2 * D_FF), F32), pltpu.VMEM((2, D_FF, D_MODEL), F32),
                        pltpu.VMEM((D_MODEL, 2 * D_FF), BF16), pltpu.VMEM((D_FF, D_MODEL), BF16),
                        pltpu.SemaphoreType.DMA((2, 2))],
    )
    return pl.pallas_call(
        _experts_kernel,
        grid_spec=grid_spec,
        out_shape=jax.ShapeDtypeStruct(xs.shape, F32),
        compiler_params=pltpu.CompilerParams(dimension_semantics=("arbitrary",), vmem_limit_bytes=MOE_VMEM),
        name="experts",
    )(blk_expert, blk_first, blk_rows, blk_next, blk_par, xs, b_up.reshape(N_EXPERTS, 1, -1),
      b_down.reshape(N_EXPERTS, 1, -1), w_up, w_down)


def _combine_kernel(dest_ref, dest_next_ref, x1_ref, gate_ref, gf_ref, ys_ref, out_ref, buf, sems):
    i = pl.program_id(0)
    slot = lax.rem(i, 2)

    def gather(d_ref, s):
        return [_tile_copy(ys_ref, d_ref[k, r], buf.at[s, k], r, sems.at[s])
                for r in range(TB) for k in range(TOP_K)]

    @pl.when(i == 0)
    def _():
        for a, cp in enumerate(gather(dest_ref, slot)):
            cp.start(priority=a % 2)

    @pl.when(i + 1 < pl.num_programs(0))
    def _():
        for a, cp in enumerate(gather(dest_next_ref, 1 - slot)):
            cp.start(priority=a % 2)

    for cp in gather(dest_ref, slot):
        cp.wait()

    cols = []
    for c in range(ROW_TILE):
        acc = x1_ref[:, c * LANES:(c + 1) * LANES]
        for k in range(TOP_K):
            acc = acc + gate_ref[:, k:k + 1] * buf[slot, k, _chunk_of_tokens(c, TB), :]
        cols.append(acc)
    out_ref[...] = _rmsnorm(jnp.concatenate(cols, axis=1), gf_ref[...])


def _combine(x1, gates, ys, dest_t, gf):
    t = x1.shape[0]
    n_blk = t // TB
    return pl.pallas_call(
        _combine_kernel,
        grid=(n_blk,),
        in_specs=[pl.BlockSpec((TOP_K, TB), lambda i: (0, i), memory_space=pltpu.SMEM),
                  pl.BlockSpec((TOP_K, TB), lambda i: (0, jnp.minimum(i + 1, n_blk - 1)),
                               memory_space=pltpu.SMEM),
                  pl.BlockSpec((TB, D_MODEL), lambda i: (i, 0)),
                  pl.BlockSpec((TB, TOP_K), lambda i: (i, 0)),
                  _const_spec((1, D_MODEL)),
                  pl.BlockSpec(memory_space=pl.ANY)],
        out_specs=pl.BlockSpec((TB, D_MODEL), lambda i: (i, 0)),
        out_shape=jax.ShapeDtypeStruct((t, D_MODEL), F32),
        scratch_shapes=[pltpu.VMEM((2, TOP_K, TB * ROW_TILE, LANES), F32), pltpu.SemaphoreType.DMA((2,))],
        compiler_params=pltpu.CompilerParams(dimension_semantics=("arbitrary",)),
        name="combine",
    )(dest_t, dest_t, x1, gates, gf.reshape(1, -1), ys)


def _pick(member, values):
    return jnp.sum(jnp.where(member, values[:, None], 0), axis=0, dtype=jnp.int32)


def _routing(top_idx_t, local_rank_t, blk_counts):
    t = top_idx_t.shape[1]
    n_slots = t * TOP_K + N_EXPERTS * BM
    n_blocks = n_slots // BM
    ids = jnp.arange(N_EXPERTS, dtype=jnp.int32)
    counts = jnp.sum(blk_counts, axis=0)
    padded = (counts + BM - 1) // BM * BM
    pend = jnp.cumsum(padded)
    pstart = pend - padded
    base = pstart[None, :] + jnp.cumsum(blk_counts, axis=0) - blk_counts
    base_tok_t = jnp.repeat(base.T, TM, axis=1)
    dest_t = local_rank_t + jnp.sum(
        jnp.where(top_idx_t[:, None, :] == ids[None, :, None], base_tok_t[None], 0), axis=1)

    blk_start = jnp.arange(n_blocks, dtype=jnp.int32) * BM
    member = (pstart[:, None] <= blk_start[None, :]) & (blk_start[None, :] < pend[:, None])
    blk_expert = _pick(member, ids)
    blk_rows = jnp.sum(jnp.where(member, jnp.clip((pstart + counts)[:, None] - blk_start[None, :], 0, BM), 0),
                       axis=0, dtype=jnp.int32)
    blk_first = _pick(member & (blk_start[None, :] == pstart[:, None]), jnp.ones_like(ids))
    used = (counts > 0).astype(jnp.int32)
    blk_par = _pick(member, (jnp.cumsum(used) - used) % 2)
    later_used = (used > 0)[None, :] & (ids[None, :] > ids[:, None])
    next_used = jnp.min(jnp.where(later_used, ids[None, :], N_EXPERTS), axis=1)
    blk_next = _pick(member, jnp.where(next_used == N_EXPERTS, -1, next_used))

    seg_size = jnp.concatenate([padded - counts, n_slots - pend[-1:]])
    seg_slot = jnp.concatenate([pstart + counts, pend[-1:]])
    seg_end = jnp.cumsum(seg_size)
    seg_begin = seg_end - seg_size
    ordinal = jnp.arange(n_slots - t * TOP_K, dtype=jnp.int32)
    in_seg = (seg_begin[:, None] <= ordinal[None, :]) & (ordinal[None, :] < seg_end[:, None])
    pad_slots = ordinal + _pick(in_seg, seg_slot - seg_begin)
    return (dest_t.astype(jnp.int32), pad_slots, blk_expert, blk_first, blk_rows, blk_next, blk_par, n_slots)


def kernel(x, positions, norm1_g, w_in, b_in, attn_sinks, w_attn_br, pool_w, pool_scale, w_pool_br, w_out,
           norm2_g, w_router, b_router, w_up, b_up, w_down, b_down, normf_g):
    b, s, d = x.shape
    t = b * s
    assert norm1_g.shape[0] == 1, "single-layer problem: the combine kernel applies the final RMSNorm"
    assert d == D_MODEL and s % TM == 0 and t % ROPE_TN == 0 and t % TD == 0 and t % TB == 0
    inv_freq = jnp.power(jnp.float32(ROPE_THETA), -jnp.arange(ROPE_HALF, dtype=F32) * (2.0 / ROPE_DIM))
    cs = _rope_table(positions.reshape(1, t), inv_freq).T
    cs = jnp.pad(cs, ((0, 0), (0, LANES - 2 * ROPE_HALF)))
    x1, h2, route, blk_counts = _mixer(
        x.reshape(t, d), cs, attn_sinks[0], norm1_g[0], w_in[0], b_in[0], w_attn_br[0], pool_w[0],
        pool_scale[0], w_pool_br[0], w_out[0], norm2_g[0], w_router[0], b_router[0], s)
    top_idx_t = route[0:TOP_K].astype(jnp.int32)
    local_rank_t = route[TOP_K:2 * TOP_K].astype(jnp.int32)
    gates = route[2 * TOP_K:3 * TOP_K].T
    dest_t, pad_slots, blk_expert, blk_first, blk_rows, blk_next, blk_par, n_slots = _routing(
        top_idx_t, local_rank_t, blk_counts[:, :, 0])
    xs = _dispatch(h2, dest_t, pad_slots, n_slots)
    ys = _experts(xs, blk_expert, blk_first, blk_rows, blk_next, blk_par, w_up[0], b_up[0], w_down[0], b_down[0])
    out = _combine(x1, gates, ys, dest_t, normf_g)
    return out.reshape(b, s, d)
```

```python
import functools

import numpy as np
import jax
import jax.numpy as jnp
from jax import lax
from jax.experimental import pallas as pl
from jax.experimental.pallas import tpu as pltpu

F32 = jnp.float32
BF16 = jnp.bfloat16

D_MODEL = 1024
HEAD_DIM = 64
N_Q_HEADS = 16
N_KV_HEADS = 2
GROUP = N_Q_HEADS // N_KV_HEADS
WINDOW = 128
ROPE_THETA = 500000.0
ROPE_DIM = HEAD_DIM // 4
ROPE_HALF = ROPE_DIM // 2
Q_WIDTH = N_Q_HEADS * HEAD_DIM
KV_WIDTH = N_KV_HEADS * HEAD_DIM
NEG_INF = -1e30
POOL_WINDOWS = (2, 4, 8, 16)
POOL_WIDTH = D_MODEL // 2
POOL_GROUP_WIDTH = POOL_WIDTH // len(POOL_WINDOWS)
POOL_HALO = 16
IN_WIDTH = Q_WIDTH + 2 * KV_WIDTH + POOL_WIDTH + 2 * D_MODEL
O_K = Q_WIDTH
O_V = O_K + KV_WIDTH
O_P = O_V + KV_WIDTH
O_GA = O_P + POOL_WIDTH
O_GP = O_GA + D_MODEL
N_EXPERTS = 32
TOP_K = 4
D_FF = D_MODEL
SWIGLU_ALPHA = 1.702
SWIGLU_LIMIT = 7.0
RMS_EPS = 1e-5

LANES = 128
ROW_TILE = D_MODEL // LANES
ROUTE_ROWS = 16
TM = 256
TD = 512
TB = 256
BM = 256
STEP_BLOCKS = 4
OUT_TILE = 256
ROPE_TN = 2048
MIX_VMEM = 56 * 1024 * 1024
MOE_VMEM = 58 * 1024 * 1024
NT_DIMS = (((1,), (1,)), ((), ()))


def _const_spec(shape):
    return pl.BlockSpec(shape, lambda *_: (0,) * len(shape))


def _chunk_of_tokens(c, n_tokens, first_token=0):
    return pl.ds(first_token * ROW_TILE + c, n_tokens, stride=ROW_TILE)


def _tile_of_token(token):
    if isinstance(token, int):
        return pl.ds(token * ROW_TILE, ROW_TILE)
    return pl.ds(pl.multiple_of(token * ROW_TILE, ROW_TILE), ROW_TILE)


def _rope_table_kernel(pos_ref, invf_ref, out_ref):
    ang = invf_ref[...] * pos_ref[...].astype(F32)
    out_ref[0:ROPE_HALF, :] = jnp.cos(ang)
    out_ref[ROPE_HALF:, :] = jnp.sin(ang)


def _rope_table(pos_row, inv_freq):
    t = pos_row.shape[1]
    return pl.pallas_call(
        _rope_table_kernel,
        grid=(t // ROPE_TN,),
        in_specs=[pl.BlockSpec((1, ROPE_TN), lambda i: (0, i)), _const_spec((ROPE_HALF, 1))],
        out_specs=pl.BlockSpec((2 * ROPE_HALF, ROPE_TN), lambda i: (0, i)),
        out_shape=jax.ShapeDtypeStruct((2 * ROPE_HALF, t), F32),
        name="rope_table",
    )(pos_row, inv_freq.reshape(ROPE_HALF, 1))


def _rope_expand_matrix():
    e = np.zeros((LANES, 2 * LANES), np.float32)
    ones = np.zeros((1, LANES), np.float32)
    for l in range(LANES):
        hl = l % HEAD_DIM
        if hl < ROPE_DIM:
            j = hl % ROPE_HALF
            e[j, l] = 1.0
            e[ROPE_HALF + j, LANES + l] = -1.0 if hl < ROPE_HALF else 1.0
        else:
            ones[0, l] = 1.0
    return e, ones


def _rmsnorm(x, g):
    ms = jnp.mean(x * x, axis=-1, keepdims=True)
    return x * lax.rsqrt(ms + RMS_EPS) * g


def _split3(x):
    a = x.astype(BF16)
    r = x - a.astype(F32)
    b = r.astype(BF16)
    c = (r - b.astype(F32)).astype(BF16)
    return a, b, c


def _mixer_kernel(sinks_ref, x_ref, cs_ref, erope_ref, ones_ref, g1_ref, win_ref, bin_ref, wab_ref,
                  pw_ref, ps_ref, wpb_ref, wout_ref, g2_ref, wr2_ref, brc_ref,
                  x1_ref, h2_ref, route_ref, cnt_ref,
                  kprev, vprev, uprev, q_scr, attn_scr, ga_scr, gp_scr, *, blocks_per_seq):
    i = pl.program_id(0)
    blk_in_seq = lax.rem(i, blocks_per_seq)
    seq_start = blk_in_seq == 0

    @pl.when(seq_start)
    def _():
        kprev[...] = jnp.zeros_like(kprev)
        vprev[...] = jnp.zeros_like(vprev)
        uprev[...] = jnp.zeros_like(uprev)

    x = x_ref[...]
    h = _rmsnorm(x, g1_ref[...]).astype(BF16)

    c1, c2, c3 = _split3(cs_ref[...])
    er = erope_ref[...]
    tab = (jnp.dot(c1, er, preferred_element_type=F32) + jnp.dot(c2, er, preferred_element_type=F32)
           + jnp.dot(c3, er, preferred_element_type=F32))
    cos_t = tab[:, :LANES] + ones_ref[...]
    sin_t = tab[:, LANES:]
    lane = lax.broadcasted_iota(jnp.int32, (TM, LANES), 1)
    first_half = (lane & (HEAD_DIM - 1)) < ROPE_HALF

    def rope(t):
        partner = jnp.where(first_half, pltpu.roll(t, LANES - ROPE_HALF, 1), pltpu.roll(t, ROPE_HALF, 1))
        return t * cos_t + partner * sin_t

    q = jnp.dot(h, win_ref[:, 0:Q_WIDTH], preferred_element_type=F32) + bin_ref[:, 0:Q_WIDTH]
    for c in range(Q_WIDTH // LANES):
        qc = rope(q[:, c * LANES:(c + 1) * LANES]) * (HEAD_DIM ** -0.5)
        q_scr[:, c * LANES:(c + 1) * LANES] = qc.astype(BF16)

    kvu = jnp.dot(h, win_ref[:, O_K:O_GA], preferred_element_type=F32) + bin_ref[:, O_K:O_GA]
    k = rope(kvu[:, 0:KV_WIDTH])
    v = kvu[:, KV_WIDTH:2 * KV_WIDTH]
    u = kvu[:, 2 * KV_WIDTH:]

    lane_b = lax.broadcasted_iota(jnp.int32, (TM + WINDOW, LANES), 1)
    low = lane_b < HEAD_DIM

    def head_bands(prev_ref, cur):
        band = jnp.concatenate([prev_ref[...], cur], axis=0)
        swapped = pltpu.roll(band, HEAD_DIM, 1)
        zero = jnp.zeros_like(band)
        a0 = jnp.where(low, band, zero).astype(BF16)
        b0 = jnp.where(low, zero, swapped).astype(BF16)
        a1 = jnp.where(low, swapped, zero).astype(BF16)
        b1 = jnp.where(low, zero, band).astype(BF16)
        return ((a0, b0), (a1, b1))

    kb = head_bands(kprev, k)
    vb = head_bands(vprev, v)
    kprev[...] = k[TM - WINDOW:, :]
    vprev[...] = v[TM - WINDOW:, :]

    qi = lax.broadcasted_iota(jnp.int32, (WINDOW, 2 * WINDOW), 0)
    kj = lax.broadcasted_iota(jnp.int32, (WINDOW, 2 * WINDOW), 1)
    mask_mid = jnp.logical_or(jnp.logical_and(kj < WINDOW, kj > qi),
                              jnp.logical_and(kj >= WINDOW, (kj - WINDOW) <= qi))
    first_key = jnp.where(seq_start, WINDOW, 0)
    mask_first = jnp.logical_and(mask_mid, kj >= first_key)

    low_w = lax.broadcasted_iota(jnp.int32, (WINDOW, LANES), 1) < HEAD_DIM
    n_chunks = GROUP // 2
    ones_rows = lax.broadcasted_iota(jnp.int32, (4 * WINDOW, LANES), 0)
    ones_lanes = lax.broadcasted_iota(jnp.int32, (4 * WINDOW, LANES), 1)
    sum_cols = jnp.where((ones_rows < 2 * WINDOW) == (ones_lanes < HEAD_DIM), 1.0, 0.0).astype(BF16)

    pairs = [(j, hkv) for j in range(TM // WINDOW) for hkv in range(N_KV_HEADS)]

    def scores(j, hkv):
        rows = slice(j * WINDOW, (j + 1) * WINDOW)
        band_rows = slice(j * WINDOW, j * WINDOW + 2 * WINDOW)
        q4 = jnp.concatenate(
            [q_scr[rows, (hkv * n_chunks + c) * LANES:(hkv * n_chunks + c + 1) * LANES]
             for c in range(n_chunks)], axis=0)
        kk = jnp.concatenate([kb[hkv][0][band_rows], kb[hkv][1][band_rows]], axis=0)
        return lax.dot_general(q4, kk, NT_DIMS, preferred_element_type=F32)

    def attend(j, hkv, s):
        mask = mask_first if j == 0 else mask_mid
        rows = slice(j * WINDOW, (j + 1) * WINDOW)
        band_rows = slice(j * WINDOW, j * WINDOW + 2 * WINDOW)
        vv = jnp.concatenate([vb[hkv][0][band_rows], vb[hkv][1][band_rows]], axis=0)
        p_rows, sink_terms = [], []
        for c in range(n_chunks):
            p_cols, st = [], []
            for par in range(2):
                sink = sinks_ref[hkv * GROUP + 2 * c + par]
                sc = s[c * WINDOW:(c + 1) * WINDOW, par * 2 * WINDOW:(par + 1) * 2 * WINDOW]
                sc = jnp.where(mask, sc, NEG_INF)
                m = jnp.maximum(jnp.max(sc, axis=-1, keepdims=True), sink)
                p_cols.append(jnp.exp(sc - m).astype(BF16))
                st.append(jnp.exp(sink - m))
            p_rows.append(jnp.concatenate(p_cols, axis=1))
            sink_terms.append(st)
        p = jnp.concatenate(p_rows, axis=0)
        o2 = jnp.dot(p, jnp.concatenate([vv, sum_cols], axis=1), preferred_element_type=F32)
        for c in range(n_chunks):
            den = o2[c * WINDOW:(c + 1) * WINDOW, LANES:] + jnp.where(low_w, sink_terms[c][0], sink_terms[c][1])
            oc = o2[c * WINDOW:(c + 1) * WINDOW, :LANES] / den
            col = (hkv * n_chunks + c) * LANES
            attn_scr[rows, col:col + LANES] = oc.astype(BF16)

    def gate_attn():
        ga_scr[...] = jax.nn.sigmoid(jnp.dot(h, win_ref[:, O_GA:O_GP], preferred_element_type=F32)
                                     + bin_ref[:, O_GA:O_GP])

    def gate_pool():
        gp_scr[...] = jax.nn.sigmoid(jnp.dot(h, win_ref[:, O_GP:], preferred_element_type=F32)
                                     + bin_ref[:, O_GP:])

    def pooling():
        ext = jnp.concatenate([uprev[...], u], axis=0)
        uprev[...] = u[TM - POOL_HALO:, :]
        pos_in_seq = blk_in_seq * TM + lax.broadcasted_iota(jnp.int32, (TM, 1), 0)
        mixed = []
        for g, w in enumerate(POOL_WINDOWS):
            cols = slice(g * POOL_GROUP_WIDTH, (g + 1) * POOL_GROUP_WIDTH)
            sg = ext[:, cols]
            step = 1
            while step < w:
                sg = sg + pltpu.roll(sg, step, 0)
                step *= 2
            count = jnp.minimum(pos_in_seq + 1, w).astype(F32)
            pooled = sg[POOL_HALO:, :] / count - u[:, cols]
            mg = jnp.dot(pooled.astype(BF16), pw_ref[g], preferred_element_type=F32)
            mixed.append((mg * ps_ref[:, cols]).astype(BF16))
        return jnp.dot(jnp.concatenate(mixed, axis=1), wpb_ref[...], preferred_element_type=F32)

    fillers = [gate_attn, gate_pool, pooling] + [None] * (len(pairs) - 3)
    y_pool = None
    s_next = scores(*pairs[0])
    for n, (j, hkv) in enumerate(pairs):
        s_cur = s_next
        if n + 1 < len(pairs):
            s_next = scores(*pairs[n + 1])
        if fillers[n] is not None:
            out = fillers[n]()
            y_pool = out if out is not None else y_pool
        attend(j, hkv, s_cur)

    y_attn = jnp.dot(attn_scr[...], wab_ref[...], preferred_element_type=F32)
    merged = (ga_scr[...] * y_attn + gp_scr[...] * y_pool).astype(BF16)
    x1 = x + jnp.dot(merged, wout_ref[...], preferred_element_type=F32)
    x1_ref[...] = x1

    h2 = _rmsnorm(x1, g2_ref[...])
    for c in range(ROW_TILE):
        h2_ref[_chunk_of_tokens(c, TM), :] = h2[:, c * LANES:(c + 1) * LANES]

    ha = h2.astype(BF16)
    hb = (h2 - ha.astype(F32)).astype(BF16)
    both = lax.dot_general(wr2_ref[...], ha, NT_DIMS, preferred_element_type=F32)
    logits = (both[:N_EXPERTS] + both[N_EXPERTS:]
              + lax.dot_general(wr2_ref[0:N_EXPERTS, :], hb, NT_DIMS, preferred_element_type=F32)
              + brc_ref[...])
    e_iota = lax.broadcasted_iota(jnp.int32, (N_EXPERTS, TM), 0)
    vals, idxs, hots = [], [], []
    work = logits
    for _ in range(TOP_K):
        m = jnp.max(work, axis=0, keepdims=True)
        ix = jnp.min(jnp.where(work == m, e_iota, N_EXPERTS), axis=0, keepdims=True)
        hot = e_iota == ix
        vals.append(m)
        idxs.append(ix)
        hots.append(hot)
        work = jnp.where(hot, -jnp.inf, work)
    exps = [jnp.exp(vk - vals[0]) for vk in vals]
    den = exps[0] + exps[1] + exps[2] + exps[3]

    picked = jnp.zeros((N_EXPERTS, TM), F32)
    for hot in hots:
        picked = jnp.where(hot, 1.0, picked)
    picked = picked.astype(BF16)
    t_row = lax.broadcasted_iota(jnp.int32, (TM, TM), 0)
    t_col = lax.broadcasted_iota(jnp.int32, (TM, TM), 1)
    earlier = jnp.where(t_row < t_col, 1.0, 0.0).astype(BF16)
    before = jnp.dot(picked, earlier, preferred_element_type=F32)
    cnt_ref[0] = jnp.dot(picked, jnp.ones((TM, LANES), BF16), preferred_element_type=F32).astype(jnp.int32)

    for kk_ in range(TOP_K):
        local_rank = jnp.sum(jnp.where(hots[kk_], before, 0.0), axis=0, keepdims=True)
        route_ref[kk_:kk_ + 1, :] = idxs[kk_].astype(F32)
        route_ref[TOP_K + kk_:TOP_K + kk_ + 1, :] = local_rank
        route_ref[2 * TOP_K + kk_:2 * TOP_K + kk_ + 1, :] = exps[kk_] / den
    route_ref[3 * TOP_K:, :] = jnp.zeros((ROUTE_ROWS - 3 * TOP_K, TM), F32)


def _mixer(x2d, cs, sinks, g1, w_in, b_in, w_ab, pool_w, pool_scale, w_pb, w_out, g2, w_r, b_r, seq_len):
    t = x2d.shape[0]
    erope, ones = _rope_expand_matrix()
    w_rh = w_r.astype(BF16)
    w_rl = (w_r - w_rh.astype(F32)).astype(BF16)
    w_r2 = jnp.concatenate([w_rh.T, w_rl.T], axis=0)
    row = lambda i: (i, 0)
    in_specs = [
        pl.BlockSpec(memory_space=pltpu.SMEM),
        pl.BlockSpec((TM, D_MODEL), row),
        pl.BlockSpec((TM, LANES), row),
        _const_spec((LANES, 2 * LANES)),
        _const_spec((1, LANES)),
        _const_spec((1, D_MODEL)),
        _const_spec((D_MODEL, IN_WIDTH)),
        _const_spec((1, IN_WIDTH)),
        _const_spec((Q_WIDTH, D_MODEL)),
        _const_spec((len(POOL_WINDOWS), POOL_GROUP_WIDTH, POOL_GROUP_WIDTH)),
        _const_spec((1, POOL_WIDTH)),
        _const_spec((POOL_WIDTH, D_MODEL)),
        _const_spec((D_MODEL, D_MODEL)),
        _const_spec((1, D_MODEL)),
        _const_spec((2 * N_EXPERTS, D_MODEL)),
        _const_spec((N_EXPERTS, 1)),
    ]
    out_specs = [pl.BlockSpec((TM, D_MODEL), row), pl.BlockSpec((TM * ROW_TILE, LANES), row),
                 pl.BlockSpec((ROUTE_ROWS, TM), lambda i: (0, i)),
                 pl.BlockSpec((1, N_EXPERTS, LANES), lambda i: (i, 0, 0))]
    out_shape = [jax.ShapeDtypeStruct((t, D_MODEL), F32), jax.ShapeDtypeStruct((t * ROW_TILE, LANES), F32),
                 jax.ShapeDtypeStruct((ROUTE_ROWS, t), F32),
                 jax.ShapeDtypeStruct((t // TM, N_EXPERTS, LANES), jnp.int32)]
    scratch = [pltpu.VMEM((WINDOW, KV_WIDTH), F32), pltpu.VMEM((WINDOW, KV_WIDTH), F32),
               pltpu.VMEM((POOL_HALO, POOL_WIDTH), F32),
               pltpu.VMEM((TM, Q_WIDTH), BF16), pltpu.VMEM((TM, Q_WIDTH), BF16),
               pltpu.VMEM((TM, D_MODEL), F32), pltpu.VMEM((TM, D_MODEL), F32)]
    return pl.pallas_call(
        functools.partial(_mixer_kernel, blocks_per_seq=seq_len // TM),
        grid=(t // TM,),
        in_specs=in_specs, out_specs=out_specs, out_shape=out_shape, scratch_shapes=scratch,
        compiler_params=pltpu.CompilerParams(dimension_semantics=("arbitrary",), vmem_limit_bytes=MIX_VMEM),
        name="mixer",
    )(sinks, x2d, cs, jnp.asarray(erope, BF16), jnp.asarray(ones), g1.reshape(1, -1), w_in.astype(BF16),
      b_in.reshape(1, -1), w_ab.astype(BF16), pool_w.astype(BF16), pool_scale.reshape(1, -1),
      w_pb.astype(BF16), w_out.astype(BF16), g2.reshape(1, -1), w_r2, b_r.reshape(-1, 1))


def _tile_copy(src, src_token, dst, dst_token, sem):
    return pltpu.make_async_copy(src.at[_tile_of_token(src_token)], dst.at[_tile_of_token(dst_token)], sem)


def _dispatch_kernel(dest_ref, pad_ref, h2_ref, xs_ref, zero_tile, sem, *, pads_per_step):
    zero_tile[...] = jnp.zeros_like(zero_tile)
    copies = [_tile_copy(h2_ref, r, xs_ref, dest_ref[k, r], sem) for r in range(TD) for k in range(TOP_K)]
    copies += [pltpu.make_async_copy(zero_tile, xs_ref.at[_tile_of_token(pad_ref[0, p])], sem)
               for p in range(pads_per_step)]
    for a, cp in enumerate(copies):
        cp.start(priority=a % 2)
    for cp in copies:
        cp.wait()


def _dispatch(h2, dest_t, pad_slots, n_slots):
    t = h2.shape[0] // ROW_TILE
    n_steps = t // TD
    pads_per_step = pad_slots.shape[0] // n_steps
    assert pads_per_step * n_steps == pad_slots.shape[0]
    return pl.pallas_call(
        functools.partial(_dispatch_kernel, pads_per_step=pads_per_step),
        grid=(n_steps,),
        in_specs=[pl.BlockSpec((TOP_K, TD), lambda i: (0, i), memory_space=pltpu.SMEM),
                  pl.BlockSpec((1, pads_per_step), lambda i: (0, i), memory_space=pltpu.SMEM),
                  pl.BlockSpec((TD * ROW_TILE, LANES), lambda i: (i, 0))],
        out_specs=pl.BlockSpec(memory_space=pl.ANY),
        out_shape=jax.ShapeDtypeStruct((n_slots * ROW_TILE, LANES), F32),
        scratch_shapes=[pltpu.VMEM((ROW_TILE, LANES), F32), pltpu.SemaphoreType.DMA(())],
        compiler_params=pltpu.CompilerParams(dimension_semantics=("arbitrary",), has_side_effects=True),
        name="dispatch",
    )(dest_t, pad_slots.reshape(1, -1), h2)


def _experts_kernel(be_ref, first_ref, rows_ref, next_ref, par_ref, xs_ref, bup_ref, bdn_ref, wup_hbm, wdn_hbm,
                    ys_ref, wup_f32, wdn_f32, wup_bf, wdn_bf, sems):
    for j in range(STEP_BLOCKS):
        _expert_block(pl.program_id(0) * STEP_BLOCKS + j, j, be_ref, first_ref, rows_ref, next_ref, par_ref, xs_ref,
                      bup_ref, bdn_ref, wup_hbm, wdn_hbm, ys_ref, wup_f32, wdn_f32, wup_bf, wdn_bf, sems)


def _expert_block(i, j, be_ref, first_ref, rows_ref, next_ref, par_ref, xs_ref, bup_ref, bdn_ref, wup_hbm, wdn_hbm,
                  ys_ref, wup_f32, wdn_f32, wup_bf, wdn_bf, sems):
    cast_rows = 128
    n_rows = rows_ref[i]
    expert = be_ref[i]
    first_token = j * BM

    def weight_copies(expert, par):
        return (pltpu.make_async_copy(wup_hbm.at[expert], wup_f32.at[par], sems.at[0, par]),
                pltpu.make_async_copy(wdn_hbm.at[expert], wdn_f32.at[par], sems.at[1, par]))

    @pl.when(first_ref[i] == 1)
    def _():
        par, nxt = par_ref[i], next_ref[i]

        @pl.when(i == 0)
        def _():
            for cp in weight_copies(expert, par):
                cp.start()

        for cp in weight_copies(expert, par):
            cp.wait()

        @pl.when(nxt >= 0)
        def _():
            for cp in weight_copies(nxt, 1 - par):
                cp.start()

        def cast_up(r, carry):
            rows = pl.ds(pl.multiple_of(r * cast_rows, cast_rows), cast_rows)
            wup_bf[rows, :] = wup_f32[par, rows, :].astype(BF16)
            return carry

        def cast_dn(r, carry):
            rows = pl.ds(pl.multiple_of(r * cast_rows, cast_rows), cast_rows)
            wdn_bf[rows, :] = wdn_f32[par, rows, :].astype(BF16)
            return carry

        lax.fori_loop(0, D_MODEL // cast_rows, cast_up, 0)
        lax.fori_loop(0, D_FF // cast_rows, cast_dn, 0)

    @pl.when(n_rows > 0)
    def _():
        xb = jnp.concatenate([xs_ref[_chunk_of_tokens(c, BM, first_token), :].astype(BF16)
                              for c in range(ROW_TILE)], axis=1)
        b_up = bup_ref[pl.ds(expert, 1), :]
        b_dn = bdn_ref[pl.ds(expert, 1), :]
        up = jnp.dot(xb, wup_bf[...], preferred_element_type=F32) + b_up
        glu = jnp.minimum(up[:, :D_FF], SWIGLU_LIMIT)
        lin = jnp.clip(up[:, D_FF:], -SWIGLU_LIMIT, SWIGLU_LIMIT)
        act = (glu * jax.nn.sigmoid(SWIGLU_ALPHA * glu) * (lin + 1.0)).astype(BF16)
        chunks_per_tile = OUT_TILE // LANES
        for n in range(D_MODEL // OUT_TILE):
            cols = slice(n * OUT_TILE, (n + 1) * OUT_TILE)
            y = jnp.dot(act, wdn_bf[:, cols], preferred_element_type=F32) + b_dn[:, cols]
            for c in range(chunks_per_tile):
                ys_ref[_chunk_of_tokens(n * chunks_per_tile + c, BM, first_token), :] = (
                    y[:, c * LANES:(c + 1) * LANES])

    @pl.when(n_rows == 0)
    def _():
        ys_ref[pl.ds(first_token * ROW_TILE, BM * ROW_TILE), :] = jnp.zeros((BM * ROW_TILE, LANES), F32)


def _experts(xs, blk_expert, blk_first, blk_rows, blk_next, blk_par, w_up, b_up, w_down, b_down):
    n_blocks = xs.shape[0] // (ROW_TILE * BM)
    assert n_blocks % STEP_BLOCKS == 0
    slot_blk = lambda i, *_: (i, 0)
    grid_spec = pltpu.PrefetchScalarGridSpec(
        num_scalar_prefetch=5,
        grid=(n_blocks // STEP_BLOCKS,),
        in_specs=[
            pl.BlockSpec((STEP_BLOCKS * BM * ROW_TILE, LANES), slot_blk),
            _const_spec((N_EXPERTS, 2 * D_FF)),
            _const_spec((N_EXPERTS, D_MODEL)),
            pl.BlockSpec(memory_space=pl.ANY),
            pl.BlockSpec(memory_space=pl.ANY),
        ],
        out_specs=pl.BlockSpec((STEP_BLOCKS * BM * ROW_TILE, LANES), slot_blk),
        scratch_shapes=[pltpu.VMEM((2, D_MODEL, 2 * D_FF), F32), pltpu.VMEM((2, D_FF, D_MODEL), F32),
                        pltpu.VMEM((D_MODEL, 2 * D_FF), BF16), pltpu.VMEM((D_FF, D_MODEL), BF16),
                        pltpu.SemaphoreType.DMA((2, 2))],
    )
    return pl.pallas_call(
        _experts_kernel,
        grid_spec=grid_spec,
        out_shape=jax.ShapeDtypeStruct(xs.shape, F32),
        compiler_params=pltpu.CompilerParams(dimension_semantics=("arbitrary",), vmem_limit_bytes=MOE_VMEM),
        name="experts",
    )(blk_expert, blk_first, blk_rows, blk_next, blk_par, xs, b_up, b_down, w_up, w_down)


def _combine_kernel(dest_ref, dest_next_ref, x1_ref, gate_ref, gf_ref, ys_ref, out_ref, buf, sems):
    i = pl.program_id(0)
    slot = lax.rem(i, 2)

    def gather(d_ref, s):
        return [_tile_copy(ys_ref, d_ref[k, r], buf.at[s, k], r, sems.at[s])
                for r in range(TB) for k in range(TOP_K)]

    @pl.when(i == 0)
    def _():
        for a, cp in enumerate(gather(dest_ref, slot)):
            cp.start(priority=a % 2)

    @pl.when(i + 1 < pl.num_programs(0))
    def _():
        for a, cp in enumerate(gather(dest_next_ref, 1 - slot)):
            cp.start(priority=a % 2)

    for cp in gather(dest_ref, slot):
        cp.wait()

    cols = []
    for c in range(ROW_TILE):
        acc = x1_ref[:, c * LANES:(c + 1) * LANES]
        for k in range(TOP_K):
            acc = acc + gate_ref[:, k:k + 1] * buf[slot, k, _chunk_of_tokens(c, TB), :]
        cols.append(acc)
    out_ref[...] = _rmsnorm(jnp.concatenate(cols, axis=1), gf_ref[...])


def _combine(x1, gates, ys, dest_t, gf):
    t = x1.shape[0]
    n_blk = t // TB
    return pl.pallas_call(
        _combine_kernel,
        grid=(n_blk,),
        in_specs=[pl.BlockSpec((TOP_K, TB), lambda i: (0, i), memory_space=pltpu.SMEM),
                  pl.BlockSpec((TOP_K, TB), lambda i: (0, jnp.minimum(i + 1, n_blk - 1)),
                               memory_space=pltpu.SMEM),
                  pl.BlockSpec((TB, D_MODEL), lambda i: (i, 0)),
                  pl.BlockSpec((TB, TOP_K), lambda i: (i, 0)),
                  _const_spec((1, D_MODEL)),
                  pl.BlockSpec(memory_space=pl.ANY)],
        out_specs=pl.BlockSpec((TB, D_MODEL), lambda i: (i, 0)),
        out_shape=jax.ShapeDtypeStruct((t, D_MODEL), F32),
        scratch_shapes=[pltpu.VMEM((2, TOP_K, TB * ROW_TILE, LANES), F32), pltpu.SemaphoreType.DMA((2,))],
        compiler_params=pltpu.CompilerParams(dimension_semantics=("arbitrary",)),
        name="combine",
    )(dest_t, dest_t, x1, gates, gf.reshape(1, -1), ys)


def _pick(member, values):
    return jnp.sum(jnp.where(member, values[:, None], 0), axis=0, dtype=jnp.int32)


def _routing(top_idx_t, local_rank_t, blk_counts):
    t = top_idx_t.shape[1]
    n_slots = t * TOP_K + N_EXPERTS * BM
    n_blocks = n_slots // BM
    ids = jnp.arange(N_EXPERTS, dtype=jnp.int32)
    counts = jnp.sum(blk_counts, axis=0)
    padded = (counts + BM - 1) // BM * BM
    pend = jnp.cumsum(padded)
    pstart = pend - padded
    base = pstart[None, :] + jnp.cumsum(blk_counts, axis=0) - blk_counts
    base_tok_t = jnp.repeat(base.T, TM, axis=1)
    dest_t = local_rank_t + jnp.sum(
        jnp.where(top_idx_t[:, None, :] == ids[None, :, None], base_tok_t[None], 0), axis=1)

    blk_start = jnp.arange(n_blocks, dtype=jnp.int32) * BM
    member = (pstart[:, None] <= blk_start[None, :]) & (blk_start[None, :] < pend[:, None])
    blk_expert = _pick(member, ids)
    blk_rows = jnp.sum(jnp.where(member, jnp.clip((pstart + counts)[:, None] - blk_start[None, :], 0, BM), 0),
                       axis=0, dtype=jnp.int32)
    blk_first = _pick(member & (blk_start[None, :] == pstart[:, None]), jnp.ones_like(ids))
    used = (counts > 0).astype(jnp.int32)
    blk_par = _pick(member, (jnp.cumsum(used) - used) % 2)
    later_used = (used > 0)[None, :] & (ids[None, :] > ids[:, None])
    next_used = jnp.min(jnp.where(later_used, ids[None, :], N_EXPERTS), axis=1)
    blk_next = _pick(member, jnp.where(next_used == N_EXPERTS, -1, next_used))

    seg_size = jnp.concatenate([padded - counts, n_slots - pend[-1:]])
    seg_slot = jnp.concatenate([pstart + counts, pend[-1:]])
    seg_end = jnp.cumsum(seg_size)
    seg_begin = seg_end - seg_size
    ordinal = jnp.arange(n_slots - t * TOP_K, dtype=jnp.int32)
    in_seg = (seg_begin[:, None] <= ordinal[None, :]) & (ordinal[None, :] < seg_end[:, None])
    pad_slots = ordinal + _pick(in_seg, seg_slot - seg_begin)
    return (dest_t.astype(jnp.int32), pad_slots, blk_expert, blk_first, blk_rows, blk_next, blk_par, n_slots)


def kernel(x, positions, norm1_g, w_in, b_in, attn_sinks, w_attn_br, pool_w, pool_scale, w_pool_br, w_out,
           norm2_g, w_router, b_router, w_up, b_up, w_down, b_down, normf_g):
    b, s, d = x.shape
    t = b * s
    assert norm1_g.shape[0] == 1, "single-layer problem: the combine kernel applies the final RMSNorm"
    assert d == D_MODEL and s % TM == 0 and t % ROPE_TN == 0 and t % TD == 0 and t % TB == 0
    inv_freq = jnp.power(jnp.float32(ROPE_THETA), -jnp.arange(ROPE_HALF, dtype=F32) * (2.0 / ROPE_DIM))
    cs = _rope_table(positions.reshape(1, t), inv_freq).T
    cs = jnp.pad(cs, ((0, 0), (0, LANES - 2 * ROPE_HALF)))
    x1, h2, route, blk_counts = _mixer(
        x.reshape(t, d), cs, attn_sinks[0], norm1_g[0], w_in[0], b_in[0], w_attn_br[0], pool_w[0],
        pool_scale[0], w_pool_br[0], w_out[0], norm2_g[0], w_router[0], b_router[0], s)
    top_idx_t = route[0:TOP_K].astype(jnp.int32)
    local_rank_t = route[TOP_K:2 * TOP_K].astype(jnp.int32)
    gates = route[2 * TOP_K:3 * TOP_K].T
    dest_t, pad_slots, blk_expert, blk_first, blk_rows, blk_next, blk_par, n_slots = _routing(
        top_idx_t, local_rank_t, blk_counts[:, :, 0])
    xs = _dispatch(h2, dest_t, pad_slots, n_slots)
    ys = _experts(xs, blk_expert, blk_first, blk_rows, blk_next, blk_par, w_up[0], b_up[0], w_down[0], b_down[0])
    out = _combine(x1, gates, ys, dest_t, normf_g)
    return out.reshape(b, s, d)
```

```python
import functools

import numpy as np
import jax
import jax.numpy as jnp
from jax import lax
from jax.experimental import pallas as pl
from jax.experimental.pallas import tpu as pltpu

F32 = jnp.float32
BF16 = jnp.bfloat16

D_MODEL = 1024
HEAD_DIM = 64
N_Q_HEADS = 16
N_KV_HEADS = 2
GROUP = N_Q_HEADS // N_KV_HEADS
WINDOW = 128
ROPE_THETA = 500000.0
ROPE_DIM = HEAD_DIM // 4
ROPE_HALF = ROPE_DIM // 2
Q_WIDTH = N_Q_HEADS * HEAD_DIM
KV_WIDTH = N_KV_HEADS * HEAD_DIM
NEG_INF = -1e30
POOL_WINDOWS = (2, 4, 8, 16)
POOL_WIDTH = D_MODEL // 2
POOL_GROUP_WIDTH = POOL_WIDTH // len(POOL_WINDOWS)
POOL_HALO = 16
IN_WIDTH = Q_WIDTH + 2 * KV_WIDTH + POOL_WIDTH + 2 * D_MODEL
O_K = Q_WIDTH
O_V = O_K + KV_WIDTH
O_P = O_V + KV_WIDTH
O_GA = O_P + POOL_WIDTH
O_GP = O_GA + D_MODEL
N_EXPERTS = 32
TOP_K = 4
D_FF = D_MODEL
SWIGLU_ALPHA = 1.702
SWIGLU_LIMIT = 7.0
RMS_EPS = 1e-5

LANES = 128
ROW_TILE = D_MODEL // LANES
ROUTE_ROWS = 16
TM = 512
TD = 512
TB = 256
BM = 256
STEP_BLOCKS = 4
OUT_TILE = 256
ROPE_TN = 2048
MIX_VMEM = 56 * 1024 * 1024
MOE_VMEM = 58 * 1024 * 1024
NT_DIMS = (((1,), (1,)), ((), ()))


def _const_spec(shape):
    return pl.BlockSpec(shape, lambda *_: (0,) * len(shape))


def _chunk_of_tokens(c, n_tokens, first_token=0):
    return pl.ds(first_token * ROW_TILE + c, n_tokens, stride=ROW_TILE)


def _tile_of_token(token):
    if isinstance(token, int):
        return pl.ds(token * ROW_TILE, ROW_TILE)
    return pl.ds(pl.multiple_of(token * ROW_TILE, ROW_TILE), ROW_TILE)


def _rope_table_kernel(pos_ref, invf_ref, out_ref):
    ang = invf_ref[...] * pos_ref[...].astype(F32)
    out_ref[0:ROPE_HALF, :] = jnp.cos(ang)
    out_ref[ROPE_HALF:, :] = jnp.sin(ang)


def _rope_table(pos_row, inv_freq):
    t = pos_row.shape[1]
    return pl.pallas_call(
        _rope_table_kernel,
        grid=(t // ROPE_TN,),
        in_specs=[pl.BlockSpec((1, ROPE_TN), lambda i: (0, i)), _const_spec((ROPE_HALF, 1))],
        out_specs=pl.BlockSpec((2 * ROPE_HALF, ROPE_TN), lambda i: (0, i)),
        out_shape=jax.ShapeDtypeStruct((2 * ROPE_HALF, t), F32),
        name="rope_table",
    )(pos_row, inv_freq.reshape(ROPE_HALF, 1))


def _rope_expand_matrix():
    e = np.zeros((LANES, 2 * LANES), np.float32)
    ones = np.zeros((1, LANES), np.float32)
    for l in range(LANES):
        hl = l % HEAD_DIM
        if hl < ROPE_DIM:
            j = hl % ROPE_HALF
            e[j, l] = 1.0
            e[ROPE_HALF + j, LANES + l] = -1.0 if hl < ROPE_HALF else 1.0
        else:
            ones[0, l] = 1.0
    return e, ones


def _rmsnorm(x, g):
    ms = jnp.mean(x * x, axis=-1, keepdims=True)
    return x * lax.rsqrt(ms + RMS_EPS) * g


def _split3(x):
    a = x.astype(BF16)
    r = x - a.astype(F32)
    b = r.astype(BF16)
    c = (r - b.astype(F32)).astype(BF16)
    return a, b, c


def _mixer_kernel(sinks_ref, x_ref, cs_ref, erope_ref, ones_ref, g1_ref, win_ref, bin_ref, wab_ref,
                  pw_ref, ps_ref, wpb_ref, wout_ref, g2_ref, wr2_ref, brc_ref,
                  x1_ref, h2_ref, route_ref, cnt_ref,
                  kprev, vprev, uprev, q_scr, attn_scr, ga_scr, gp_scr, *, blocks_per_seq):
    i = pl.program_id(0)
    blk_in_seq = lax.rem(i, blocks_per_seq)
    seq_start = blk_in_seq == 0

    @pl.when(seq_start)
    def _():
        kprev[...] = jnp.zeros_like(kprev)
        vprev[...] = jnp.zeros_like(vprev)
        uprev[...] = jnp.zeros_like(uprev)

    x = x_ref[...]
    h = _rmsnorm(x, g1_ref[...]).astype(BF16)

    c1, c2, c3 = _split3(cs_ref[...])
    er = erope_ref[...]
    tab = (jnp.dot(c1, er, preferred_element_type=F32) + jnp.dot(c2, er, preferred_element_type=F32)
           + jnp.dot(c3, er, preferred_element_type=F32))
    cos_t = tab[:, :LANES] + ones_ref[...]
    sin_t = tab[:, LANES:]
    lane = lax.broadcasted_iota(jnp.int32, (TM, LANES), 1)
    first_half = (lane & (HEAD_DIM - 1)) < ROPE_HALF

    def rope(t):
        partner = jnp.where(first_half, pltpu.roll(t, LANES - ROPE_HALF, 1), pltpu.roll(t, ROPE_HALF, 1))
        return t * cos_t + partner * sin_t

    q = jnp.dot(h, win_ref[:, 0:Q_WIDTH], preferred_element_type=F32) + bin_ref[:, 0:Q_WIDTH]
    for c in range(Q_WIDTH // LANES):
        qc = rope(q[:, c * LANES:(c + 1) * LANES]) * (HEAD_DIM ** -0.5)
        q_scr[:, c * LANES:(c + 1) * LANES] = qc.astype(BF16)

    kvu = jnp.dot(h, win_ref[:, O_K:O_GA], preferred_element_type=F32) + bin_ref[:, O_K:O_GA]
    k = rope(kvu[:, 0:KV_WIDTH])
    v = kvu[:, KV_WIDTH:2 * KV_WIDTH]
    u = kvu[:, 2 * KV_WIDTH:]

    lane_b = lax.broadcasted_iota(jnp.int32, (TM + WINDOW, LANES), 1)
    low = lane_b < HEAD_DIM

    def head_bands(prev_ref, cur):
        band = jnp.concatenate([prev_ref[...], cur], axis=0)
        swapped = pltpu.roll(band, HEAD_DIM, 1)
        zero = jnp.zeros_like(band)
        a0 = jnp.where(low, band, zero).astype(BF16)
        b0 = jnp.where(low, zero, swapped).astype(BF16)
        a1 = jnp.where(low, swapped, zero).astype(BF16)
        b1 = jnp.where(low, zero, band).astype(BF16)
        return ((a0, b0), (a1, b1))

    kb = head_bands(kprev, k)
    vb = head_bands(vprev, v)
    kprev[...] = k[TM - WINDOW:, :]
    vprev[...] = v[TM - WINDOW:, :]

    qi = lax.broadcasted_iota(jnp.int32, (WINDOW, 2 * WINDOW), 0)
    kj = lax.broadcasted_iota(jnp.int32, (WINDOW, 2 * WINDOW), 1)
    mask_mid = jnp.logical_or(jnp.logical_and(kj < WINDOW, kj > qi),
                              jnp.logical_and(kj >= WINDOW, (kj - WINDOW) <= qi))
    first_key = jnp.where(seq_start, WINDOW, 0)
    mask_first = jnp.logical_and(mask_mid, kj >= first_key)

    low_w = lax.broadcasted_iota(jnp.int32, (WINDOW, LANES), 1) < HEAD_DIM
    n_chunks = GROUP // 2
    ones_rows = lax.broadcasted_iota(jnp.int32, (4 * WINDOW, LANES), 0)
    ones_lanes = lax.broadcasted_iota(jnp.int32, (4 * WINDOW, LANES), 1)
    sum_cols = jnp.where((ones_rows < 2 * WINDOW) == (ones_lanes < HEAD_DIM), 1.0, 0.0).astype(BF16)

    pairs = [(j, hkv) for j in range(TM // WINDOW) for hkv in range(N_KV_HEADS)]

    def scores(j, hkv):
        rows = slice(j * WINDOW, (j + 1) * WINDOW)
        band_rows = slice(j * WINDOW, j * WINDOW + 2 * WINDOW)
        q4 = jnp.concatenate(
            [q_scr[rows, (hkv * n_chunks + c) * LANES:(hkv * n_chunks + c + 1) * LANES]
             for c in range(n_chunks)], axis=0)
        kk = jnp.concatenate([kb[hkv][0][band_rows], kb[hkv][1][band_rows]], axis=0)
        return lax.dot_general(q4, kk, NT_DIMS, preferred_element_type=F32)

    def attend(j, hkv, s):
        mask = mask_first if j == 0 else mask_mid
        rows = slice(j * WINDOW, (j + 1) * WINDOW)
        band_rows = slice(j * WINDOW, j * WINDOW + 2 * WINDOW)
        vv = jnp.concatenate([vb[hkv][0][band_rows], vb[hkv][1][band_rows]], axis=0)
        p_rows, sink_terms = [], []
        for c in range(n_chunks):
            p_cols, st = [], []
            for par in range(2):
                sink = sinks_ref[hkv * GROUP + 2 * c + par]
                sc = s[c * WINDOW:(c + 1) * WINDOW, par * 2 * WINDOW:(par + 1) * 2 * WINDOW]
                sc = jnp.where(mask, sc, NEG_INF)
                m = jnp.maximum(jnp.max(sc, axis=-1, keepdims=True), sink)
                p_cols.append(jnp.exp(sc - m).astype(BF16))
                st.append(jnp.exp(sink - m))
            p_rows.append(jnp.concatenate(p_cols, axis=1))
            sink_terms.append(st)
        p = jnp.concatenate(p_rows, axis=0)
        o2 = jnp.dot(p, jnp.concatenate([vv, sum_cols], axis=1), preferred_element_type=F32)
        for c in range(n_chunks):
            den = o2[c * WINDOW:(c + 1) * WINDOW, LANES:] + jnp.where(low_w, sink_terms[c][0], sink_terms[c][1])
            oc = o2[c * WINDOW:(c + 1) * WINDOW, :LANES] / den
            col = (hkv * n_chunks + c) * LANES
            attn_scr[rows, col:col + LANES] = oc.astype(BF16)

    def gate_attn():
        ga_scr[...] = jax.nn.sigmoid(jnp.dot(h, win_ref[:, O_GA:O_GP], preferred_element_type=F32)
                                     + bin_ref[:, O_GA:O_GP])

    def gate_pool():
        gp_scr[...] = jax.nn.sigmoid(jnp.dot(h, win_ref[:, O_GP:], preferred_element_type=F32)
                                     + bin_ref[:, O_GP:])

    def pooling():
        ext = jnp.concatenate([uprev[...], u], axis=0)
        uprev[...] = u[TM - POOL_HALO:, :]
        pos_in_seq = blk_in_seq * TM + lax.broadcasted_iota(jnp.int32, (TM, 1), 0)
        mixed = []
        for g, w in enumerate(POOL_WINDOWS):
            cols = slice(g * POOL_GROUP_WIDTH, (g + 1) * POOL_GROUP_WIDTH)
            sg = ext[:, cols]
            step = 1
            while step < w:
                sg = sg + pltpu.roll(sg, step, 0)
                step *= 2
            count = jnp.minimum(pos_in_seq + 1, w).astype(F32)
            pooled = sg[POOL_HALO:, :] / count - u[:, cols]
            mg = jnp.dot(pooled.astype(BF16), pw_ref[g], preferred_element_type=F32)
            mixed.append((mg * ps_ref[:, cols]).astype(BF16))
        return jnp.dot(jnp.concatenate(mixed, axis=1), wpb_ref[...], preferred_element_type=F32)

    fillers = [gate_attn, gate_pool, pooling] + [None] * (len(pairs) - 3)
    y_pool = None
    s_next = scores(*pairs[0])
    for n, (j, hkv) in enumerate(pairs):
        s_cur = s_next
        if n + 1 < len(pairs):
            s_next = scores(*pairs[n + 1])
        if fillers[n] is not None:
            out = fillers[n]()
            y_pool = out if out is not None else y_pool
        attend(j, hkv, s_cur)

    y_attn = jnp.dot(attn_scr[...], wab_ref[...], preferred_element_type=F32)
    merged = (ga_scr[...] * y_attn + gp_scr[...] * y_pool).astype(BF16)
    x1 = x + jnp.dot(merged, wout_ref[...], preferred_element_type=F32)
    x1_ref[...] = x1

    h2 = _rmsnorm(x1, g2_ref[...])
    for c in range(ROW_TILE):
        h2_ref[_chunk_of_tokens(c, TM), :] = h2[:, c * LANES:(c + 1) * LANES]

    ha = h2.astype(BF16)
    hb = (h2 - ha.astype(F32)).astype(BF16)
    both = lax.dot_general(wr2_ref[...], ha, NT_DIMS, preferred_element_type=F32)
    logits = (both[:N_EXPERTS] + both[N_EXPERTS:]
              + lax.dot_general(wr2_ref[0:N_EXPERTS, :], hb, NT_DIMS, preferred_element_type=F32)
              + brc_ref[...])
    e_iota = lax.broadcasted_iota(jnp.int32, (N_EXPERTS, TM), 0)
    vals, idxs, hots = [], [], []
    work = logits
    for _ in range(TOP_K):
        m = jnp.max(work, axis=0, keepdims=True)
        ix = jnp.min(jnp.where(work == m, e_iota, N_EXPERTS), axis=0, keepdims=True)
        hot = e_iota == ix
        vals.append(m)
        idxs.append(ix)
        hots.append(hot)
        work = jnp.where(hot, -jnp.inf, work)
    exps = [jnp.exp(vk - vals[0]) for vk in vals]
    den = exps[0] + exps[1] + exps[2] + exps[3]

    picked = jnp.zeros((N_EXPERTS, TM), F32)
    for hot in hots:
        picked = jnp.where(hot, 1.0, picked)
    picked = picked.astype(BF16)
    t_row = lax.broadcasted_iota(jnp.int32, (TM, TM), 0)
    t_col = lax.broadcasted_iota(jnp.int32, (TM, TM), 1)
    earlier = jnp.where(t_row < t_col, 1.0, 0.0).astype(BF16)
    before = jnp.dot(picked, earlier, preferred_element_type=F32)
    cnt_ref[0] = jnp.dot(picked, jnp.ones((TM, LANES), BF16), preferred_element_type=F32).astype(jnp.int32)

    for kk_ in range(TOP_K):
        local_rank = jnp.sum(jnp.where(hots[kk_], before, 0.0), axis=0, keepdims=True)
        route_ref[kk_:kk_ + 1, :] = idxs[kk_].astype(F32)
        route_ref[TOP_K + kk_:TOP_K + kk_ + 1, :] = local_rank
        route_ref[2 * TOP_K + kk_:2 * TOP_K + kk_ + 1, :] = exps[kk_] / den
    route_ref[3 * TOP_K:, :] = jnp.zeros((ROUTE_ROWS - 3 * TOP_K, TM), F32)


def _mixer(x2d, cs, sinks, g1, w_in, b_in, w_ab, pool_w, pool_scale, w_pb, w_out, g2, w_r, b_r, seq_len):
    t = x2d.shape[0]
    erope, ones = _rope_expand_matrix()
    w_rh = w_r.astype(BF16)
    w_rl = (w_r - w_rh.astype(F32)).astype(BF16)
    w_r2 = jnp.concatenate([w_rh.T, w_rl.T], axis=0)
    row = lambda i: (i, 0)
    in_specs = [
        pl.BlockSpec(memory_space=pltpu.SMEM),
        pl.BlockSpec((TM, D_MODEL), row),
        pl.BlockSpec((TM, LANES), row),
        _const_spec((LANES, 2 * LANES)),
        _const_spec((1, LANES)),
        _const_spec((1, D_MODEL)),
        _const_spec((D_MODEL, IN_WIDTH)),
        _const_spec((1, IN_WIDTH)),
        _const_spec((Q_WIDTH, D_MODEL)),
        _const_spec((len(POOL_WINDOWS), POOL_GROUP_WIDTH, POOL_GROUP_WIDTH)),
        _const_spec((1, POOL_WIDTH)),
        _const_spec((POOL_WIDTH, D_MODEL)),
        _const_spec((D_MODEL, D_MODEL)),
        _const_spec((1, D_MODEL)),
        _const_spec((2 * N_EXPERTS, D_MODEL)),
        _const_spec((N_EXPERTS, 1)),
    ]
    out_specs = [pl.BlockSpec((TM, D_MODEL), row), pl.BlockSpec((TM * ROW_TILE, LANES), row),
                 pl.BlockSpec((ROUTE_ROWS, TM), lambda i: (0, i)),
                 pl.BlockSpec((1, N_EXPERTS, LANES), lambda i: (i, 0, 0))]
    out_shape = [jax.ShapeDtypeStruct((t, D_MODEL), F32), jax.ShapeDtypeStruct((t * ROW_TILE, LANES), F32),
                 jax.ShapeDtypeStruct((ROUTE_ROWS, t), F32),
                 jax.ShapeDtypeStruct((t // TM, N_EXPERTS, LANES), jnp.int32)]
    scratch = [pltpu.VMEM((WINDOW, KV_WIDTH), F32), pltpu.VMEM((WINDOW, KV_WIDTH), F32),
               pltpu.VMEM((POOL_HALO, POOL_WIDTH), F32),
               pltpu.VMEM((TM, Q_WIDTH), BF16), pltpu.VMEM((TM, Q_WIDTH), BF16),
               pltpu.VMEM((TM, D_MODEL), F32), pltpu.VMEM((TM, D_MODEL), F32)]
    return pl.pallas_call(
        functools.partial(_mixer_kernel, blocks_per_seq=seq_len // TM),
        grid=(t // TM,),
        in_specs=in_specs, out_specs=out_specs, out_shape=out_shape, scratch_shapes=scratch,
        compiler_params=pltpu.CompilerParams(dimension_semantics=("arbitrary",), vmem_limit_bytes=MIX_VMEM),
        name="mixer",
    )(sinks, x2d, cs, jnp.asarray(erope, BF16), jnp.asarray(ones), g1.reshape(1, -1), w_in.astype(BF16),
      b_in.reshape(1, -1), w_ab.astype(BF16), pool_w.astype(BF16), pool_scale.reshape(1, -1),
      w_pb.astype(BF16), w_out.astype(BF16), g2.reshape(1, -1), w_r2, b_r.reshape(-1, 1))


def _tile_copy(src, src_token, dst, dst_token, sem):
    return pltpu.make_async_copy(src.at[_tile_of_token(src_token)], dst.at[_tile_of_token(dst_token)], sem)


def _dispatch_kernel(dest_ref, pad_ref, h2_ref, h2_hbm, xs_ref, zero_tile, sem, *, pads_per_step):
    zero_tile[...] = jnp.zeros_like(zero_tile)
    first_token = pl.program_id(0) * TD
    copies = []
    for r in range(TD):
        for k in range(TOP_K):
            if k < TOP_K // 2:
                copies.append(_tile_copy(h2_ref, r, xs_ref, dest_ref[k, r], sem))
            else:
                copies.append(_tile_copy(h2_hbm, first_token + r, xs_ref, dest_ref[k, r], sem))
    copies += [pltpu.make_async_copy(zero_tile, xs_ref.at[_tile_of_token(pad_ref[0, p])], sem)
               for p in range(pads_per_step)]
    for a, cp in enumerate(copies):
        cp.start(priority=(a // TOP_K) % 2)
    for cp in copies:
        cp.wait()


def _dispatch(h2, dest_t, pad_slots, n_slots):
    t = h2.shape[0] // ROW_TILE
    n_steps = t // TD
    pads_per_step = pad_slots.shape[0] // n_steps
    assert pads_per_step * n_steps == pad_slots.shape[0]
    return pl.pallas_call(
        functools.partial(_dispatch_kernel, pads_per_step=pads_per_step),
        grid=(n_steps,),
        in_specs=[pl.BlockSpec((TOP_K, TD), lambda i: (0, i), memory_space=pltpu.SMEM),
                  pl.BlockSpec((1, pads_per_step), lambda i: (0, i), memory_space=pltpu.SMEM),
                  pl.BlockSpec((TD * ROW_TILE, LANES), lambda i: (i, 0)),
                  pl.BlockSpec(memory_space=pl.ANY)],
        out_specs=pl.BlockSpec(memory_space=pl.ANY),
        out_shape=jax.ShapeDtypeStruct((n_slots * ROW_TILE, LANES), F32),
        scratch_shapes=[pltpu.VMEM((ROW_TILE, LANES), F32), pltpu.SemaphoreType.DMA(())],
        compiler_params=pltpu.CompilerParams(dimension_semantics=("arbitrary",), has_side_effects=True),
        name="dispatch",
    )(dest_t, pad_slots.reshape(1, -1), h2, h2)


def _experts_kernel(be_ref, first_ref, rows_ref, next_ref, par_ref, xs_ref, bup_ref, bdn_ref, wup_hbm, wdn_hbm,
                    ys_ref, wup_f32, wdn_f32, wup_bf, wdn_bf, sems):
    for j in range(STEP_BLOCKS):
        _expert_block(pl.program_id(0) * STEP_BLOCKS + j, j, be_ref, first_ref, rows_ref, next_ref, par_ref, xs_ref,
                      bup_ref, bdn_ref, wup_hbm, wdn_hbm, ys_ref, wup_f32, wdn_f32, wup_bf, wdn_bf, sems)


def _expert_block(i, j, be_ref, first_ref, rows_ref, next_ref, par_ref, xs_ref, bup_ref, bdn_ref, wup_hbm, wdn_hbm,
                  ys_ref, wup_f32, wdn_f32, wup_bf, wdn_bf, sems):
    cast_rows = 128
    n_rows = rows_ref[i]
    expert = be_ref[i]
    first_token = j * BM

    def weight_copies(expert, par):
        return (pltpu.make_async_copy(wup_hbm.at[expert], wup_f32.at[par], sems.at[0, par]),
                pltpu.make_async_copy(wdn_hbm.at[expert], wdn_f32.at[par], sems.at[1, par]))

    @pl.when(first_ref[i] == 1)
    def _():
        par, nxt = par_ref[i], next_ref[i]

        @pl.when(i == 0)
        def _():
            for cp in weight_copies(expert, par):
                cp.start()

        for cp in weight_copies(expert, par):
            cp.wait()

        @pl.when(nxt >= 0)
        def _():
            for cp in weight_copies(nxt, 1 - par):
                cp.start()

        def cast_up(r, carry):
            rows = pl.ds(pl.multiple_of(r * cast_rows, cast_rows), cast_rows)
            wup_bf[rows, :] = wup_f32[par, rows, :].astype(BF16)
            return carry

        def cast_dn(r, carry):
            rows = pl.ds(pl.multiple_of(r * cast_rows, cast_rows), cast_rows)
            wdn_bf[rows, :] = wdn_f32[par, rows, :].astype(BF16)
            return carry

        lax.fori_loop(0, D_MODEL // cast_rows, cast_up, 0)
        lax.fori_loop(0, D_FF // cast_rows, cast_dn, 0)

    @pl.when(n_rows > 0)
    def _():
        xb = jnp.concatenate([xs_ref[_chunk_of_tokens(c, BM, first_token), :].astype(BF16)
                              for c in range(ROW_TILE)], axis=1)
        b_up = bup_ref[pl.ds(expert, 1), :]
        b_dn = bdn_ref[pl.ds(expert, 1), :]
        up = jnp.dot(xb, wup_bf[...], preferred_element_type=F32) + b_up
        glu = jnp.minimum(up[:, :D_FF], SWIGLU_LIMIT)
        lin = jnp.clip(up[:, D_FF:], -SWIGLU_LIMIT, SWIGLU_LIMIT)
        act = (glu * jax.nn.sigmoid(SWIGLU_ALPHA * glu) * (lin + 1.0)).astype(BF16)
        chunks_per_tile = OUT_TILE // LANES
        for n in range(D_MODEL // OUT_TILE):
            cols = slice(n * OUT_TILE, (n + 1) * OUT_TILE)
            y = jnp.dot(act, wdn_bf[:, cols], preferred_element_type=F32) + b_dn[:, cols]
            for c in range(chunks_per_tile):
                ys_ref[_chunk_of_tokens(n * chunks_per_tile + c, BM, first_token), :] = (
                    y[:, c * LANES:(c + 1) * LANES])

    @pl.when(n_rows == 0)
    def _():
        ys_ref[pl.ds(first_token * ROW_TILE, BM * ROW_TILE), :] = jnp.zeros((BM * ROW_TILE, LANES), F32)


def _experts(xs, blk_expert, blk_first, blk_rows, blk_next, blk_par, w_up, b_up, w_down, b_down):
    n_blocks = xs.shape[0] // (ROW_TILE * BM)
    assert n_blocks % STEP_BLOCKS == 0
    slot_blk = lambda i, *_: (i, 0)
    grid_spec = pltpu.PrefetchScalarGridSpec(
        num_scalar_prefetch=5,
        grid=(n_blocks // STEP_BLOCKS,),
        in_specs=[
            pl.BlockSpec((STEP_BLOCKS * BM * ROW_TILE, LANES), slot_blk),
            _const_spec((N_EXPERTS, 2 * D_FF)),
            _const_spec((N_EXPERTS, D_MODEL)),
            pl.BlockSpec(memory_space=pl.ANY),
            pl.BlockSpec(memory_space=pl.ANY),
        ],
        out_specs=pl.BlockSpec((STEP_BLOCKS * BM * ROW_TILE, LANES), slot_blk),
        scratch_shapes=[pltpu.VMEM((2, D_MODEL, 2 * D_FF), F32), pltpu.VMEM((2, D_FF, D_MODEL), F32),
                        pltpu.VMEM((D_MODEL, 2 * D_FF), BF16), pltpu.VMEM((D_FF, D_MODEL), BF16),
                        pltpu.SemaphoreType.DMA((2, 2))],
    )
    return pl.pallas_call(
        _experts_kernel,
        grid_spec=grid_spec,
        out_shape=jax.ShapeDtypeStruct(xs.shape, F32),
        compiler_params=pltpu.CompilerParams(dimension_semantics=("arbitrary",), vmem_limit_bytes=MOE_VMEM),
        name="experts",
    )(blk_expert, blk_first, blk_rows, blk_next, blk_par, xs, b_up, b_down, w_up, w_down)


def _combine_kernel(dest_ref, dest_next_ref, x1_ref, gate_ref, gf_ref, ys_ref, out_ref, buf, sems):
    i = pl.program_id(0)
    slot = lax.rem(i, 2)

    def gather(d_ref, s):
        return [_tile_copy(ys_ref, d_ref[k, r], buf.at[s, k], r, sems.at[s])
                for r in range(TB) for k in range(TOP_K)]

    @pl.when(i == 0)
    def _():
        for a, cp in enumerate(gather(dest_ref, slot)):
            cp.start(priority=a % 2)

    @pl.when(i + 1 < pl.num_programs(0))
    def _():
        for a, cp in enumerate(gather(dest_next_ref, 1 - slot)):
            cp.start(priority=a % 2)

    for cp in gather(dest_ref, slot):
        cp.wait()

    cols = []
    for c in range(ROW_TILE):
        acc = x1_ref[:, c * LANES:(c + 1) * LANES]
        for k in range(TOP_K):
            acc = acc + gate_ref[:, k:k + 1] * buf[slot, k, _chunk_of_tokens(c, TB), :]
        cols.append(acc)
    out_ref[...] = _rmsnorm(jnp.concatenate(cols, axis=1), gf_ref[...])


def _combine(x1, gates, ys, dest_t, gf):
    t = x1.shape[0]
    n_blk = t // TB
    return pl.pallas_call(
        _combine_kernel,
        grid=(n_blk,),
        in_specs=[pl.BlockSpec((TOP_K, TB), lambda i: (0, i), memory_space=pltpu.SMEM),
                  pl.BlockSpec((TOP_K, TB), lambda i: (0, jnp.minimum(i + 1, n_blk - 1)),
                               memory_space=pltpu.SMEM),
                  pl.BlockSpec((TB, D_MODEL), lambda i: (i, 0)),
                  pl.BlockSpec((TB, TOP_K), lambda i: (i, 0)),
                  _const_spec((1, D_MODEL)),
                  pl.BlockSpec(memory_space=pl.ANY)],
        out_specs=pl.BlockSpec((TB, D_MODEL), lambda i: (i, 0)),
        out_shape=jax.ShapeDtypeStruct((t, D_MODEL), F32),
        scratch_shapes=[pltpu.VMEM((2, TOP_K, TB * ROW_TILE, LANES), F32), pltpu.SemaphoreType.DMA((2,))],
        compiler_params=pltpu.CompilerParams(dimension_semantics=("arbitrary",)),
        name="combine",
    )(dest_t, dest_t, x1, gates, gf.reshape(1, -1), ys)


def _pick(member, values):
    return jnp.sum(jnp.where(member, values[:, None], 0), axis=0, dtype=jnp.int32)


def _routing(top_idx_t, local_rank_t, blk_counts):
    t = top_idx_t.shape[1]
    n_slots = t * TOP_K + N_EXPERTS * BM
    n_blocks = n_slots // BM
    ids = jnp.arange(N_EXPERTS, dtype=jnp.int32)
    counts = jnp.sum(blk_counts, axis=0)
    padded = (counts + BM - 1) // BM * BM
    pend = jnp.cumsum(padded)
    pstart = pend - padded
    base = pstart[None, :] + jnp.cumsum(blk_counts, axis=0) - blk_counts
    base_tok_t = jnp.repeat(base.T, TM, axis=1)
    dest_t = local_rank_t + jnp.sum(
        jnp.where(top_idx_t[:, None, :] == ids[None, :, None], base_tok_t[None], 0), axis=1)

    blk_start = jnp.arange(n_blocks, dtype=jnp.int32) * BM
    member = (pstart[:, None] <= blk_start[None, :]) & (blk_start[None, :] < pend[:, None])
    blk_expert = _pick(member, ids)
    blk_rows = jnp.sum(jnp.where(member, jnp.clip((pstart + counts)[:, None] - blk_start[None, :], 0, BM), 0),
                       axis=0, dtype=jnp.int32)
    blk_first = _pick(member & (blk_start[None, :] == pstart[:, None]), jnp.ones_like(ids))
    used = (counts > 0).astype(jnp.int32)
    blk_par = _pick(member, (jnp.cumsum(used) - used) % 2)
    later_used = (used > 0)[None, :] & (ids[None, :] > ids[:, None])
    next_used = jnp.min(jnp.where(later_used, ids[None, :], N_EXPERTS), axis=1)
    blk_next = _pick(member, jnp.where(next_used == N_EXPERTS, -1, next_used))

    seg_size = jnp.concatenate([padded - counts, n_slots - pend[-1:]])
    seg_slot = jnp.concatenate([pstart + counts, pend[-1:]])
    seg_end = jnp.cumsum(seg_size)
    seg_begin = seg_end - seg_size
    ordinal = jnp.arange(n_slots - t * TOP_K, dtype=jnp.int32)
    in_seg = (seg_begin[:, None] <= ordinal[None, :]) & (ordinal[None, :] < seg_end[:, None])
    pad_slots = ordinal + _pick(in_seg, seg_slot - seg_begin)
    return (dest_t.astype(jnp.int32), pad_slots, blk_expert, blk_first, blk_rows, blk_next, blk_par, n_slots)


def kernel(x, positions, norm1_g, w_in, b_in, attn_sinks, w_attn_br, pool_w, pool_scale, w_pool_br, w_out,
           norm2_g, w_router, b_router, w_up, b_up, w_down, b_down, normf_g):
    b, s, d = x.shape
    t = b * s
    assert norm1_g.shape[0] == 1, "single-layer problem: the combine kernel applies the final RMSNorm"
    assert d == D_MODEL and s % TM == 0 and t % ROPE_TN == 0 and t % TD == 0 and t % TB == 0
    inv_freq = jnp.power(jnp.float32(ROPE_THETA), -jnp.arange(ROPE_HALF, dtype=F32) * (2.0 / ROPE_DIM))
    cs = _rope_table(positions.reshape(1, t), inv_freq).T
    cs = jnp.pad(cs, ((0, 0), (0, LANES - 2 * ROPE_HALF)))
    x1, h2, route, blk_counts = _mixer(
        x.reshape(t, d), cs, attn_sinks[0], norm1_g[0], w_in[0], b_in[0], w_attn_br[0], pool_w[0],
        pool_scale[0], w_pool_br[0], w_out[0], norm2_g[0], w_router[0], b_router[0], s)
    top_idx_t = route[0:TOP_K].astype(jnp.int32)
    local_rank_t = route[TOP_K:2 * TOP_K].astype(jnp.int32)
    gates = route[2 * TOP_K:3 * TOP_K].T
    dest_t, pad_slots, blk_expert, blk_first, blk_rows, blk_next, blk_par, n_slots = _routing(
        top_idx_t, local_rank_t, blk_counts[:, :, 0])
    xs = _dispatch(h2, dest_t, pad_slots, n_slots)
    ys = _experts(xs, blk_expert, blk_first, blk_rows, blk_next, blk_par, w_up[0], b_up[0], w_down[0], b_down[0])
    out = _combine(x1, gates, ys, dest_t, normf_g)
    return out.reshape(b, s, d)
```

```python
import functools

import numpy as np
import jax
import jax.numpy as jnp
from jax import lax
from jax.experimental import pallas as pl
from jax.experimental.pallas import tpu as pltpu

F32 = jnp.float32
BF16 = jnp.bfloat16

D_MODEL = 1024
HEAD_DIM = 64
N_Q_HEADS = 16
N_KV_HEADS = 2
GROUP = N_Q_HEADS // N_KV_HEADS
WINDOW = 128
ROPE_THETA = 500000.0
ROPE_DIM = HEAD_DIM // 4
ROPE_HALF = ROPE_DIM // 2
Q_WIDTH = N_Q_HEADS * HEAD_DIM
KV_WIDTH = N_KV_HEADS * HEAD_DIM
NEG_INF = -1e30
POOL_WINDOWS = (2, 4, 8, 16)
POOL_WIDTH = D_MODEL // 2
POOL_GROUP_WIDTH = POOL_WIDTH // len(POOL_WINDOWS)
POOL_HALO = 16
IN_WIDTH = Q_WIDTH + 2 * KV_WIDTH + POOL_WIDTH + 2 * D_MODEL
O_K = Q_WIDTH
O_V = O_K + KV_WIDTH
O_P = O_V + KV_WIDTH
O_GA = O_P + POOL_WIDTH
O_GP = O_GA + D_MODEL
N_EXPERTS = 32
TOP_K = 4
D_FF = D_MODEL
SWIGLU_ALPHA = 1.702
SWIGLU_LIMIT = 7.0
RMS_EPS = 1e-5

LANES = 128
ROW_TILE = D_MODEL // LANES
ROUTE_ROWS = 16
TM = 512
TD = 512
TB = 256
BM = 256
STEP_BLOCKS = 4
OUT_TILE = 256
ROPE_TN = 2048
MIX_VMEM = 56 * 1024 * 1024
MOE_VMEM = 58 * 1024 * 1024
NT_DIMS = (((1,), (1,)), ((), ()))


def _const_spec(shape):
    return pl.BlockSpec(shape, lambda *_: (0,) * len(shape))


def _chunk_of_tokens(c, n_tokens, first_token=0):
    return pl.ds(first_token * ROW_TILE + c, n_tokens, stride=ROW_TILE)


def _tile_of_token(token):
    if isinstance(token, int):
        return pl.ds(token * ROW_TILE, ROW_TILE)
    return pl.ds(pl.multiple_of(token * ROW_TILE, ROW_TILE), ROW_TILE)


def _rope_table_kernel(pos_ref, invf_ref, out_ref):
    ang = invf_ref[...] * pos_ref[...].astype(F32)
    out_ref[0:ROPE_HALF, :] = jnp.cos(ang)
    out_ref[ROPE_HALF:, :] = jnp.sin(ang)


def _rope_table(pos_row, inv_freq):
    t = pos_row.shape[1]
    return pl.pallas_call(
        _rope_table_kernel,
        grid=(t // ROPE_TN,),
        in_specs=[pl.BlockSpec((1, ROPE_TN), lambda i: (0, i)), _const_spec((ROPE_HALF, 1))],
        out_specs=pl.BlockSpec((2 * ROPE_HALF, ROPE_TN), lambda i: (0, i)),
        out_shape=jax.ShapeDtypeStruct((2 * ROPE_HALF, t), F32),
        name="rope_table",
    )(pos_row, inv_freq.reshape(ROPE_HALF, 1))


def _rope_expand_matrix():
    e = np.zeros((LANES, 2 * LANES), np.float32)
    ones = np.zeros((1, LANES), np.float32)
    for l in range(LANES):
        hl = l % HEAD_DIM
        if hl < ROPE_DIM:
            j = hl % ROPE_HALF
            e[j, l] = 1.0
            e[ROPE_HALF + j, LANES + l] = -1.0 if hl < ROPE_HALF else 1.0
        else:
            ones[0, l] = 1.0
    return e, ones


def _rmsnorm(x, g):
    ms = jnp.mean(x * x, axis=-1, keepdims=True)
    return x * lax.rsqrt(ms + RMS_EPS) * g


def _split3(x):
    a = x.astype(BF16)
    r = x - a.astype(F32)
    b = r.astype(BF16)
    c = (r - b.astype(F32)).astype(BF16)
    return a, b, c


def _mixer_kernel(sinks_ref, x_ref, cs_ref, erope_ref, ones_ref, g1_ref, win_ref, bin_ref, wab_ref,
                  pw_ref, ps_ref, wpb_ref, wout_ref, g2_ref, wr2_ref, brc_ref,
                  x1_ref, h2_ref, route_ref, cnt_ref,
                  kprev, vprev, uprev, q_scr, attn_scr, ga_scr, gp_scr, yat_scr, *, blocks_per_seq):
    i = pl.program_id(0)
    blk_in_seq = lax.rem(i, blocks_per_seq)
    seq_start = blk_in_seq == 0

    @pl.when(seq_start)
    def _():
        kprev[...] = jnp.zeros_like(kprev)
        vprev[...] = jnp.zeros_like(vprev)
        uprev[...] = jnp.zeros_like(uprev)

    x = x_ref[...]
    h = _rmsnorm(x, g1_ref[...]).astype(BF16)

    c1, c2, c3 = _split3(cs_ref[...])
    er = erope_ref[...]
    tab = (jnp.dot(c1, er, preferred_element_type=F32) + jnp.dot(c2, er, preferred_element_type=F32)
           + jnp.dot(c3, er, preferred_element_type=F32))
    cos_t = tab[:, :LANES] + ones_ref[...]
    sin_t = tab[:, LANES:]
    lane = lax.broadcasted_iota(jnp.int32, (TM, LANES), 1)
    first_half = (lane & (HEAD_DIM - 1)) < ROPE_HALF

    def rope(t):
        partner = jnp.where(first_half, pltpu.roll(t, LANES - ROPE_HALF, 1), pltpu.roll(t, ROPE_HALF, 1))
        return t * cos_t + partner * sin_t

    q = jnp.dot(h, win_ref[:, 0:Q_WIDTH], preferred_element_type=F32) + bin_ref[:, 0:Q_WIDTH]
    for c in range(Q_WIDTH // LANES):
        qc = rope(q[:, c * LANES:(c + 1) * LANES]) * (HEAD_DIM ** -0.5)
        q_scr[:, c * LANES:(c + 1) * LANES] = qc.astype(BF16)

    kvu = jnp.dot(h, win_ref[:, O_K:O_GA], preferred_element_type=F32) + bin_ref[:, O_K:O_GA]
    k = rope(kvu[:, 0:KV_WIDTH])
    v = kvu[:, KV_WIDTH:2 * KV_WIDTH]
    u = kvu[:, 2 * KV_WIDTH:]

    lane_b = lax.broadcasted_iota(jnp.int32, (TM + WINDOW, LANES), 1)
    low = lane_b < HEAD_DIM

    def head_bands(prev_ref, cur):
        band = jnp.concatenate([prev_ref[...], cur], axis=0)
        swapped = pltpu.roll(band, HEAD_DIM, 1)
        zero = jnp.zeros_like(band)
        a0 = jnp.where(low, band, zero).astype(BF16)
        b0 = jnp.where(low, zero, swapped).astype(BF16)
        a1 = jnp.where(low, swapped, zero).astype(BF16)
        b1 = jnp.where(low, zero, band).astype(BF16)
        return ((a0, b0), (a1, b1))

    kb = head_bands(kprev, k)
    vb = head_bands(vprev, v)
    kprev[...] = k[TM - WINDOW:, :]
    vprev[...] = v[TM - WINDOW:, :]

    qi = lax.broadcasted_iota(jnp.int32, (WINDOW, 2 * WINDOW), 0)
    kj = lax.broadcasted_iota(jnp.int32, (WINDOW, 2 * WINDOW), 1)
    mask_mid = jnp.logical_or(jnp.logical_and(kj < WINDOW, kj > qi),
                              jnp.logical_and(kj >= WINDOW, (kj - WINDOW) <= qi))
    first_key = jnp.where(seq_start, WINDOW, 0)
    mask_first = jnp.logical_and(mask_mid, kj >= first_key)

    low_w = lax.broadcasted_iota(jnp.int32, (WINDOW, LANES), 1) < HEAD_DIM
    n_chunks = GROUP // 2
    ones_rows = lax.broadcasted_iota(jnp.int32, (4 * WINDOW, LANES), 0)
    ones_lanes = lax.broadcasted_iota(jnp.int32, (4 * WINDOW, LANES), 1)
    sum_cols = jnp.where((ones_rows < 2 * WINDOW) == (ones_lanes < HEAD_DIM), 1.0, 0.0).astype(BF16)

    pairs = [(j, hkv) for j in range(TM // WINDOW) for hkv in range(N_KV_HEADS)]

    def scores(j, hkv):
        rows = slice(j * WINDOW, (j + 1) * WINDOW)
        band_rows = slice(j * WINDOW, j * WINDOW + 2 * WINDOW)
        q4 = jnp.concatenate(
            [q_scr[rows, (hkv * n_chunks + c) * LANES:(hkv * n_chunks + c + 1) * LANES]
             for c in range(n_chunks)], axis=0)
        kk = jnp.concatenate([kb[hkv][0][band_rows], kb[hkv][1][band_rows]], axis=0)
        return lax.dot_general(q4, kk, NT_DIMS, preferred_element_type=F32)

    def attend(j, hkv, s):
        mask = mask_first if j == 0 else mask_mid
        rows = slice(j * WINDOW, (j + 1) * WINDOW)
        band_rows = slice(j * WINDOW, j * WINDOW + 2 * WINDOW)
        vv = jnp.concatenate([vb[hkv][0][band_rows], vb[hkv][1][band_rows]], axis=0)
        p_rows, sink_terms = [], []
        for c in range(n_chunks):
            p_cols, st = [], []
            for par in range(2):
                sink = sinks_ref[hkv * GROUP + 2 * c + par]
                sc = s[c * WINDOW:(c + 1) * WINDOW, par * 2 * WINDOW:(par + 1) * 2 * WINDOW]
                sc = jnp.where(mask, sc, NEG_INF)
                m = jnp.maximum(jnp.max(sc, axis=-1, keepdims=True), sink)
                p_cols.append(jnp.exp(sc - m).astype(BF16))
                st.append(jnp.exp(sink - m))
            p_rows.append(jnp.concatenate(p_cols, axis=1))
            sink_terms.append(st)
        p = jnp.concatenate(p_rows, axis=0)
        o2 = jnp.dot(p, jnp.concatenate([vv, sum_cols], axis=1), preferred_element_type=F32)
        for c in range(n_chunks):
            den = o2[c * WINDOW:(c + 1) * WINDOW, LANES:] + jnp.where(low_w, sink_terms[c][0], sink_terms[c][1])
            oc = o2[c * WINDOW:(c + 1) * WINDOW, :LANES] / den
            col = (hkv * n_chunks + c) * LANES
            attn_scr[rows, col:col + LANES] = oc.astype(BF16)

    half = D_MODEL // 2

    def gate(scr, offset, part):
        cols = slice(offset + part * half, offset + (part + 1) * half)
        scr[:, part * half:(part + 1) * half] = jax.nn.sigmoid(
            jnp.dot(h, win_ref[:, cols], preferred_element_type=F32) + bin_ref[:, cols])

    def attn_branch(j):
        rows = slice(j * WINDOW, (j + 1) * WINDOW)
        yat_scr[rows, :] = jnp.dot(attn_scr[rows, :], wab_ref[...], preferred_element_type=F32)

    def pooling():
        ext = jnp.concatenate([uprev[...], u], axis=0)
        uprev[...] = u[TM - POOL_HALO:, :]
        pos_in_seq = blk_in_seq * TM + lax.broadcasted_iota(jnp.int32, (TM, 1), 0)
        mixed = []
        for g, w in enumerate(POOL_WINDOWS):
            cols = slice(g * POOL_GROUP_WIDTH, (g + 1) * POOL_GROUP_WIDTH)
            sg = ext[:, cols]
            step = 1
            while step < w:
                sg = sg + pltpu.roll(sg, step, 0)
                step *= 2
            count = jnp.minimum(pos_in_seq + 1, w).astype(F32)
            pooled = sg[POOL_HALO:, :] / count - u[:, cols]
            mg = jnp.dot(pooled.astype(BF16), pw_ref[g], preferred_element_type=F32)
            mixed.append((mg * ps_ref[:, cols]).astype(BF16))
        return jnp.dot(jnp.concatenate(mixed, axis=1), wpb_ref[...], preferred_element_type=F32)

    fillers = [functools.partial(gate, ga_scr, O_GA, 0), functools.partial(gate, ga_scr, O_GA, 1),
               functools.partial(gate, gp_scr, O_GP, 0), functools.partial(gate, gp_scr, O_GP, 1), pooling]
    n_row_blocks = TM // WINDOW
    branch_done = 0
    y_pool = None
    s_next = scores(*pairs[0])
    for n, (j, hkv) in enumerate(pairs):
        s_cur = s_next
        if n + 1 < len(pairs):
            s_next = scores(*pairs[n + 1])
        if n < len(fillers):
            out = fillers[n]()
            y_pool = out if out is not None else y_pool
        else:
            while branch_done < n_row_blocks and N_KV_HEADS * (branch_done + 1) <= n:
                attn_branch(branch_done)
                branch_done += 1
        attend(j, hkv, s_cur)
    for jb in range(branch_done, n_row_blocks):
        attn_branch(jb)

    hm = TM // 2
    e_iota = lax.broadcasted_iota(jnp.int32, (N_EXPERTS, hm), 0)

    def out_proj(part):
        rows = slice(part * hm, (part + 1) * hm)
        merged = (ga_scr[rows, :] * yat_scr[rows, :] + gp_scr[rows, :] * y_pool[rows, :]).astype(BF16)
        x1 = x[rows, :] + jnp.dot(merged, wout_ref[...], preferred_element_type=F32)
        x1_ref[rows, :] = x1
        return x1

    def route(part, x1):
        h2 = _rmsnorm(x1, g2_ref[...])
        for c in range(ROW_TILE):
            h2_ref[_chunk_of_tokens(c, hm, part * hm), :] = h2[:, c * LANES:(c + 1) * LANES]
        ha = h2.astype(BF16)
        hb = (h2 - ha.astype(F32)).astype(BF16)
        both = lax.dot_general(wr2_ref[...], ha, NT_DIMS, preferred_element_type=F32)
        logits = (both[:N_EXPERTS] + both[N_EXPERTS:]
                  + lax.dot_general(wr2_ref[0:N_EXPERTS, :], hb, NT_DIMS, preferred_element_type=F32)
                  + brc_ref[...])
        vals, idx, hot = [], [], []
        work = logits
        for _ in range(TOP_K):
            m = jnp.max(work, axis=0, keepdims=True)
            ix = jnp.min(jnp.where(work == m, e_iota, N_EXPERTS), axis=0, keepdims=True)
            is_hot = e_iota == ix
            vals.append(m)
            idx.append(ix)
            hot.append(jnp.where(is_hot, 1.0, 0.0))
            work = jnp.where(is_hot, -jnp.inf, work)
        ex = [jnp.exp(vk - vals[0]) for vk in vals]
        den = ex[0] + ex[1] + ex[2] + ex[3]
        return idx, [e / den for e in ex], hot

    x1_parts = [out_proj(0), out_proj(1)]
    routed = [route(part, x1_parts[part]) for part in range(2)]
    idxs = [jnp.concatenate([routed[0][0][k], routed[1][0][k]], axis=1) for k in range(TOP_K)]
    gates = [jnp.concatenate([routed[0][1][k], routed[1][1][k]], axis=1) for k in range(TOP_K)]
    hots = [jnp.concatenate([routed[0][2][k], routed[1][2][k]], axis=1) for k in range(TOP_K)]

    picked = (hots[0] + hots[1] + hots[2] + hots[3]).astype(BF16)
    t_row = lax.broadcasted_iota(jnp.int32, (TM, TM), 0)
    t_col = lax.broadcasted_iota(jnp.int32, (TM, TM), 1)
    earlier = jnp.where(t_row < t_col, 1.0, 0.0).astype(BF16)
    before = jnp.dot(picked, earlier, preferred_element_type=F32)
    cnt_ref[0] = jnp.dot(picked, jnp.ones((TM, LANES), BF16), preferred_element_type=F32).astype(jnp.int32)

    for kk_ in range(TOP_K):
        local_rank = jnp.sum(hots[kk_] * before, axis=0, keepdims=True)
        route_ref[kk_:kk_ + 1, :] = idxs[kk_].astype(F32)
        route_ref[TOP_K + kk_:TOP_K + kk_ + 1, :] = local_rank
        route_ref[2 * TOP_K + kk_:2 * TOP_K + kk_ + 1, :] = gates[kk_]
    route_ref[3 * TOP_K:, :] = jnp.zeros((ROUTE_ROWS - 3 * TOP_K, TM), F32)


def _mixer(x2d, cs, sinks, g1, w_in, b_in, w_ab, pool_w, pool_scale, w_pb, w_out, g2, w_r, b_r, seq_len):
    t = x2d.shape[0]
    erope, ones = _rope_expand_matrix()
    w_rh = w_r.astype(BF16)
    w_rl = (w_r - w_rh.astype(F32)).astype(BF16)
    w_r2 = jnp.concatenate([w_rh.T, w_rl.T], axis=0)
    row = lambda i: (i, 0)
    in_specs = [
        pl.BlockSpec(memory_space=pltpu.SMEM),
        pl.BlockSpec((TM, D_MODEL), row),
        pl.BlockSpec((TM, LANES), row),
        _const_spec((LANES, 2 * LANES)),
        _const_spec((1, LANES)),
        _const_spec((1, D_MODEL)),
        _const_spec((D_MODEL, IN_WIDTH)),
        _const_spec((1, IN_WIDTH)),
        _const_spec((Q_WIDTH, D_MODEL)),
        _const_spec((len(POOL_WINDOWS), POOL_GROUP_WIDTH, POOL_GROUP_WIDTH)),
        _const_spec((1, POOL_WIDTH)),
        _const_spec((POOL_WIDTH, D_MODEL)),
        _const_spec((D_MODEL, D_MODEL)),
        _const_spec((1, D_MODEL)),
        _const_spec((2 * N_EXPERTS, D_MODEL)),
        _const_spec((N_EXPERTS, 1)),
    ]
    out_specs = [pl.BlockSpec((TM, D_MODEL), row), pl.BlockSpec((TM * ROW_TILE, LANES), row),
                 pl.BlockSpec((ROUTE_ROWS, TM), lambda i: (0, i)),
                 pl.BlockSpec((1, N_EXPERTS, LANES), lambda i: (i, 0, 0))]
    out_shape = [jax.ShapeDtypeStruct((t, D_MODEL), F32), jax.ShapeDtypeStruct((t * ROW_TILE, LANES), F32),
                 jax.ShapeDtypeStruct((ROUTE_ROWS, t), F32),
                 jax.ShapeDtypeStruct((t // TM, N_EXPERTS, LANES), jnp.int32)]
    scratch = [pltpu.VMEM((WINDOW, KV_WIDTH), F32), pltpu.VMEM((WINDOW, KV_WIDTH), F32),
               pltpu.VMEM((POOL_HALO, POOL_WIDTH), F32),
               pltpu.VMEM((TM, Q_WIDTH), BF16), pltpu.VMEM((TM, Q_WIDTH), BF16),
               pltpu.VMEM((TM, D_MODEL), F32), pltpu.VMEM((TM, D_MODEL), F32), pltpu.VMEM((TM, D_MODEL), F32)]
    return pl.pallas_call(
        functools.partial(_mixer_kernel, blocks_per_seq=seq_len // TM),
        grid=(t // TM,),
        in_specs=in_specs, out_specs=out_specs, out_shape=out_shape, scratch_shapes=scratch,
        compiler_params=pltpu.CompilerParams(dimension_semantics=("arbitrary",), vmem_limit_bytes=MIX_VMEM),
        name="mixer",
    )(sinks, x2d, cs, jnp.asarray(erope, BF16), jnp.asarray(ones), g1.reshape(1, -1), w_in.astype(BF16),
      b_in.reshape(1, -1), w_ab.astype(BF16), pool_w.astype(BF16), pool_scale.reshape(1, -1),
      w_pb.astype(BF16), w_out.astype(BF16), g2.reshape(1, -1), w_r2, b_r.reshape(-1, 1))


def _tile_copy(src, src_token, dst, dst_token, sem):
    return pltpu.make_async_copy(src.at[_tile_of_token(src_token)], dst.at[_tile_of_token(dst_token)], sem)


def _dispatch_kernel(dest_ref, pad_ref, h2_ref, xs_ref, zero_tile, sem, *, pads_per_step):
    zero_tile[...] = jnp.zeros_like(zero_tile)
    copies = [_tile_copy(h2_ref, r, xs_ref, dest_ref[k, r], sem) for r in range(TD) for k in range(TOP_K)]
    copies += [pltpu.make_async_copy(zero_tile, xs_ref.at[_tile_of_token(pad_ref[0, p])], sem)
               for p in range(pads_per_step)]
    for a, cp in enumerate(copies):
        cp.start(priority=a % 2)
    for cp in copies:
        cp.wait()


def _dispatch(h2, dest_t, pad_slots, n_slots):
    t = h2.shape[0] // ROW_TILE
    n_steps = t // TD
    pads_per_step = pad_slots.shape[0] // n_steps
    assert pads_per_step * n_steps == pad_slots.shape[0]
    return pl.pallas_call(
        functools.partial(_dispatch_kernel, pads_per_step=pads_per_step),
        grid=(n_steps,),
        in_specs=[pl.BlockSpec((TOP_K, TD), lambda i: (0, i), memory_space=pltpu.SMEM),
                  pl.BlockSpec((1, pads_per_step), lambda i: (0, i), memory_space=pltpu.SMEM),
                  pl.BlockSpec((TD * ROW_TILE, LANES), lambda i: (i, 0))],
        out_specs=pl.BlockSpec(memory_space=pl.ANY),
        out_shape=jax.ShapeDtypeStruct((n_slots * ROW_TILE, LANES), F32),
        scratch_shapes=[pltpu.VMEM((ROW_TILE, LANES), F32), pltpu.SemaphoreType.DMA(())],
        compiler_params=pltpu.CompilerParams(dimension_semantics=("arbitrary",), has_side_effects=True),
        name="dispatch",
    )(dest_t, pad_slots.reshape(1, -1), h2)


def _experts_kernel(be_ref, first_ref, rows_ref, next_ref, par_ref, xs_ref, bup_ref, bdn_ref, wup_hbm, wdn_hbm,
                    ys_ref, wup_f32, wdn_f32, wup_bf, wdn_bf, sems):
    for j in range(STEP_BLOCKS):
        _expert_block(pl.program_id(0) * STEP_BLOCKS + j, j, be_ref, first_ref, rows_ref, next_ref, par_ref, xs_ref,
                      bup_ref, bdn_ref, wup_hbm, wdn_hbm, ys_ref, wup_f32, wdn_f32, wup_bf, wdn_bf, sems)


def _expert_block(i, j, be_ref, first_ref, rows_ref, next_ref, par_ref, xs_ref, bup_ref, bdn_ref, wup_hbm, wdn_hbm,
                  ys_ref, wup_f32, wdn_f32, wup_bf, wdn_bf, sems):
    cast_rows = 128
    n_rows = rows_ref[i]
    expert = be_ref[i]
    first_token = j * BM

    def weight_copies(expert, par):
        return (pltpu.make_async_copy(wup_hbm.at[expert], wup_f32.at[par], sems.at[0, par]),
                pltpu.make_async_copy(wdn_hbm.at[expert], wdn_f32.at[par], sems.at[1, par]))

    @pl.when(first_ref[i] == 1)
    def _():
        par, nxt = par_ref[i], next_ref[i]

        @pl.when(i == 0)
        def _():
            for cp in weight_copies(expert, par):
                cp.start()

        for cp in weight_copies(expert, par):
            cp.wait()

        @pl.when(nxt >= 0)
        def _():
            for cp in weight_copies(nxt, 1 - par):
                cp.start()

        def cast_up(r, carry):
            rows = pl.ds(pl.multiple_of(r * cast_rows, cast_rows), cast_rows)
            wup_bf[rows, :] = wup_f32[par, rows, :].astype(BF16)
            return carry

        def cast_dn(r, carry):
            rows = pl.ds(pl.multiple_of(r * cast_rows, cast_rows), cast_rows)
            wdn_bf[rows, :] = wdn_f32[par, rows, :].astype(BF16)
            return carry

        lax.fori_loop(0, D_MODEL // cast_rows, cast_up, 0)
        lax.fori_loop(0, D_FF // cast_rows, cast_dn, 0)

    @pl.when(n_rows > 0)
    def _():
        xb = jnp.concatenate([xs_ref[_chunk_of_tokens(c, BM, first_token), :].astype(BF16)
                              for c in range(ROW_TILE)], axis=1)
        b_up = bup_ref[pl.ds(expert, 1), :]
        b_dn = bdn_ref[pl.ds(expert, 1), :]
        up = jnp.dot(xb, wup_bf[...], preferred_element_type=F32) + b_up
        glu = jnp.minimum(up[:, :D_FF], SWIGLU_LIMIT)
        lin = jnp.clip(up[:, D_FF:], -SWIGLU_LIMIT, SWIGLU_LIMIT)
        act = (glu * jax.nn.sigmoid(SWIGLU_ALPHA * glu) * (lin + 1.0)).astype(BF16)
        chunks_per_tile = OUT_TILE // LANES
        for n in range(D_MODEL // OUT_TILE):
            cols = slice(n * OUT_TILE, (n + 1) * OUT_TILE)
            y = jnp.dot(act, wdn_bf[:, cols], preferred_element_type=F32) + b_dn[:, cols]
            for c in range(chunks_per_tile):
                ys_ref[_chunk_of_tokens(n * chunks_per_tile + c, BM, first_token), :] = (
                    y[:, c * LANES:(c + 1) * LANES])

    @pl.when(n_rows == 0)
    def _():
        ys_ref[pl.ds(first_token * ROW_TILE, BM * ROW_TILE), :] = jnp.zeros((BM * ROW_TILE, LANES), F32)


def _experts(xs, blk_expert, blk_first, blk_rows, blk_next, blk_par, w_up, b_up, w_down, b_down):
    n_blocks = xs.shape[0] // (ROW_TILE * BM)
    assert n_blocks % STEP_BLOCKS == 0
    slot_blk = lambda i, *_: (i, 0)
    grid_spec = pltpu.PrefetchScalarGridSpec(
        num_scalar_prefetch=5,
        grid=(n_blocks // STEP_BLOCKS,),
        in_specs=[
            pl.BlockSpec((STEP_BLOCKS * BM * ROW_TILE, LANES), slot_blk),
            _const_spec((N_EXPERTS, 2 * D_FF)),
            _const_spec((N_EXPERTS, D_MODEL)),
            pl.BlockSpec(memory_space=pl.ANY),
            pl.BlockSpec(memory_space=pl.ANY),
        ],
        out_specs=pl.BlockSpec((STEP_BLOCKS * BM * ROW_TILE, LANES), slot_blk),
        scratch_shapes=[pltpu.VMEM((2, D_MODEL, 2 * D_FF), F32), pltpu.VMEM((2, D_FF, D_MODEL), F32),
                        pltpu.VMEM((D_MODEL, 2 * D_FF), BF16), pltpu.VMEM((D_FF, D_MODEL), BF16),
                        pltpu.SemaphoreType.DMA((2, 2))],
    )
    return pl.pallas_call(
        _experts_kernel,
        grid_spec=grid_spec,
        out_shape=jax.ShapeDtypeStruct(xs.shape, F32),
        compiler_params=pltpu.CompilerParams(dimension_semantics=("arbitrary",), vmem_limit_bytes=MOE_VMEM),
        name="experts",
    )(blk_expert, blk_first, blk_rows, blk_next, blk_par, xs, b_up, b_down, w_up, w_down)


def _combine_kernel(dest_ref, dest_next_ref, x1_ref, gate_ref, gf_ref, ys_ref, out_ref, buf, sems):
    i = pl.program_id(0)
    slot = lax.rem(i, 2)

    def gather(d_ref, s):
        return [_tile_copy(ys_ref, d_ref[k, r], buf.at[s, k], r, sems.at[s])
                for r in range(TB) for k in range(TOP_K)]

    @pl.when(i == 0)
    def _():
        for a, cp in enumerate(gather(dest_ref, slot)):
            cp.start(priority=a % 2)

    @pl.when(i + 1 < pl.num_programs(0))
    def _():
        for a, cp in enumerate(gather(dest_next_ref, 1 - slot)):
            cp.start(priority=a % 2)

    for cp in gather(dest_ref, slot):
        cp.wait()

    cols = []
    for c in range(ROW_TILE):
        acc = x1_ref[:, c * LANES:(c + 1) * LANES]
        for k in range(TOP_K):
            acc = acc + gate_ref[:, k:k + 1] * buf[slot, k, _chunk_of_tokens(c, TB), :]
        cols.append(acc)
    out_ref[...] = _rmsnorm(jnp.concatenate(cols, axis=1), gf_ref[...])


def _combine(x1, gates, ys, dest_t, gf):
    t = x1.shape[0]
    n_blk = t // TB
    return pl.pallas_call(
        _combine_kernel,
        grid=(n_blk,),
        in_specs=[pl.BlockSpec((TOP_K, TB), lambda i: (0, i), memory_space=pltpu.SMEM),
                  pl.BlockSpec((TOP_K, TB), lambda i: (0, jnp.minimum(i + 1, n_blk - 1)),
                               memory_space=pltpu.SMEM),
                  pl.BlockSpec((TB, D_MODEL), lambda i: (i, 0)),
                  pl.BlockSpec((TB, TOP_K), lambda i: (i, 0)),
                  _const_spec((1, D_MODEL)),
                  pl.BlockSpec(memory_space=pl.ANY)],
        out_specs=pl.BlockSpec((TB, D_MODEL), lambda i: (i, 0)),
        out_shape=jax.ShapeDtypeStruct((t, D_MODEL), F32),
        scratch_shapes=[pltpu.VMEM((2, TOP_K, TB * ROW_TILE, LANES), F32), pltpu.SemaphoreType.DMA((2,))],
        compiler_params=pltpu.CompilerParams(dimension_semantics=("arbitrary",)),
        name="combine",
    )(dest_t, dest_t, x1, gates, gf.reshape(1, -1), ys)


def _pick(member, values):
    return jnp.sum(jnp.where(member, values[:, None], 0), axis=0, dtype=jnp.int32)


def _routing(top_idx_t, local_rank_t, blk_counts):
    t = top_idx_t.shape[1]
    n_slots = t * TOP_K + N_EXPERTS * BM
    n_blocks = n_slots // BM
    ids = jnp.arange(N_EXPERTS, dtype=jnp.int32)
    counts = jnp.sum(blk_counts, axis=0)
    padded = (counts + BM - 1) // BM * BM
    pend = jnp.cumsum(padded)
    pstart = pend - padded
    base = pstart[None, :] + jnp.cumsum(blk_counts, axis=0) - blk_counts
    base_tok_t = jnp.repeat(base.T, TM, axis=1)
    dest_t = local_rank_t + jnp.sum(
        jnp.where(top_idx_t[:, None, :] == ids[None, :, None], base_tok_t[None], 0), axis=1)

    blk_start = jnp.arange(n_blocks, dtype=jnp.int32) * BM
    member = (pstart[:, None] <= blk_start[None, :]) & (blk_start[None, :] < pend[:, None])
    blk_expert = _pick(member, ids)
    blk_rows = jnp.sum(jnp.where(member, jnp.clip((pstart + counts)[:, None] - blk_start[None, :], 0, BM), 0),
                       axis=0, dtype=jnp.int32)
    blk_first = _pick(member & (blk_start[None, :] == pstart[:, None]), jnp.ones_like(ids))
    used = (counts > 0).astype(jnp.int32)
    blk_par = _pick(member, (jnp.cumsum(used) - used) % 2)
    later_used = (used > 0)[None, :] & (ids[None, :] > ids[:, None])
    next_used = jnp.min(jnp.where(later_used, ids[None, :], N_EXPERTS), axis=1)
    blk_next = _pick(member, jnp.where(next_used == N_EXPERTS, -1, next_used))

    seg_size = jnp.concatenate([padded - counts, n_slots - pend[-1:]])
    seg_slot = jnp.concatenate([pstart + counts, pend[-1:]])
    seg_end = jnp.cumsum(seg_size)
    seg_begin = seg_end - seg_size
    ordinal = jnp.arange(n_slots - t * TOP_K, dtype=jnp.int32)
    in_seg = (seg_begin[:, None] <= ordinal[None, :]) & (ordinal[None, :] < seg_end[:, None])
    pad_slots = ordinal + _pick(in_seg, seg_slot - seg_begin)
    return (dest_t.astype(jnp.int32), pad_slots, blk_expert, blk_first, blk_rows, blk_next, blk_par, n_slots)


def kernel(x, positions, norm1_g, w_in, b_in, attn_sinks, w_attn_br, pool_w, pool_scale, w_pool_br, w_out,
           norm2_g, w_router, b_router, w_up, b_up, w_down, b_down, normf_g):
    b, s, d = x.shape
    t = b * s
    assert norm1_g.shape[0] == 1, "single-layer problem: the combine kernel applies the final RMSNorm"
    assert d == D_MODEL and s % TM == 0 and t % ROPE_TN == 0 and t % TD == 0 and t % TB == 0
    inv_freq = jnp.power(jnp.float32(ROPE_THETA), -jnp.arange(ROPE_HALF, dtype=F32) * (2.0 / ROPE_DIM))
    cs = _rope_table(positions.reshape(1, t), inv_freq).T
    cs = jnp.pad(cs, ((0, 0), (0, LANES - 2 * ROPE_HALF)))
    x1, h2, route, blk_counts = _mixer(
        x.reshape(t, d), cs, attn_sinks[0], norm1_g[0], w_in[0], b_in[0], w_attn_br[0], pool_w[0],
        pool_scale[0], w_pool_br[0], w_out[0], norm2_g[0], w_router[0], b_router[0], s)
    top_idx_t = route[0:TOP_K].astype(jnp.int32)
    local_rank_t = route[TOP_K:2 * TOP_K].astype(jnp.int32)
    gates = route[2 * TOP_K:3 * TOP_K].T
    dest_t, pad_slots, blk_expert, blk_first, blk_rows, blk_next, blk_par, n_slots = _routing(
        top_idx_t, local_rank_t, blk_counts[:, :, 0])
    xs = _dispatch(h2, dest_t, pad_slots, n_slots)
    ys = _experts(xs, blk_expert, blk_first, blk_rows, blk_next, blk_par, w_up[0], b_up[0], w_down[0], b_down[0])
    out = _combine(x1, gates, ys, dest_t, normf_g)
    return out.reshape(b, s, d)
```

```python
import functools

import numpy as np
import jax
import jax.numpy as jnp
from jax import lax
from jax.experimental import pallas as pl
from jax.experimental.pallas import tpu as pltpu

F32 = jnp.float32
BF16 = jnp.bfloat16

D_MODEL = 1024
HEAD_DIM = 64
N_Q_HEADS = 16
N_KV_HEADS = 2
GROUP = N_Q_HEADS // N_KV_HEADS
WINDOW = 128
ROPE_THETA = 500000.0
ROPE_DIM = HEAD_DIM // 4
ROPE_HALF = ROPE_DIM // 2
Q_WIDTH = N_Q_HEADS * HEAD_DIM
KV_WIDTH = N_KV_HEADS * HEAD_DIM
NEG_INF = -1e30
POOL_WINDOWS = (2, 4, 8, 16)
POOL_WIDTH = D_MODEL // 2
POOL_GROUP_WIDTH = POOL_WIDTH // len(POOL_WINDOWS)
POOL_HALO = 16
IN_WIDTH = Q_WIDTH + 2 * KV_WIDTH + POOL_WIDTH + 2 * D_MODEL
O_K = Q_WIDTH
O_V = O_K + KV_WIDTH
O_P = O_V + KV_WIDTH
O_GA = O_P + POOL_WIDTH
O_GP = O_GA + D_MODEL
N_EXPERTS = 32
TOP_K = 4
D_FF = D_MODEL
SWIGLU_ALPHA = 1.702
SWIGLU_LIMIT = 7.0
RMS_EPS = 1e-5

LANES = 128
ROW_TILE = D_MODEL // LANES
ROUTE_ROWS = 16
TM = 512
TD = 512
TB = 256
BM = 256
STEP_BLOCKS = 4
OUT_TILE = 256
ROPE_TN = 2048
V7X_VMEM_BYTES = 64 * 1024 * 1024
MIX_VMEM = V7X_VMEM_BYTES - 8 * 1024 * 1024
MOE_VMEM = V7X_VMEM_BYTES - 6 * 1024 * 1024
COMBINE_VMEM = V7X_VMEM_BYTES // 2
NT_DIMS = (((1,), (1,)), ((), ()))
TN_DIMS = (((0,), (0,)), ((), ()))


def _const_spec(shape):
    return pl.BlockSpec(shape, lambda *_: (0,) * len(shape))


def _chunk_of_tokens(c, n_tokens, first_token=0):
    return pl.ds(first_token * ROW_TILE + c, n_tokens, stride=ROW_TILE)


def _tile_of_token(token):
    if isinstance(token, int):
        return pl.ds(token * ROW_TILE, ROW_TILE)
    return pl.ds(pl.multiple_of(token * ROW_TILE, ROW_TILE), ROW_TILE)


def _rope_table_kernel(pos_ref, invf_ref, out_ref):
    ang = invf_ref[...] * pos_ref[...].astype(F32)
    out_ref[0:ROPE_HALF, :] = jnp.cos(ang)
    out_ref[ROPE_HALF:, :] = jnp.sin(ang)


def _rope_table(pos_row, inv_freq):
    t = pos_row.shape[1]
    return pl.pallas_call(
        _rope_table_kernel,
        grid=(t // ROPE_TN,),
        in_specs=[pl.BlockSpec((1, ROPE_TN), lambda i: (0, i)), _const_spec((ROPE_HALF, 1))],
        out_specs=pl.BlockSpec((2 * ROPE_HALF, ROPE_TN), lambda i: (0, i)),
        out_shape=jax.ShapeDtypeStruct((2 * ROPE_HALF, t), F32),
        name="rope_table",
    )(pos_row, inv_freq.reshape(ROPE_HALF, 1))


def _rope_expand_matrix():
    e = np.zeros((2 * ROPE_HALF, 2 * LANES), np.float32)
    ones = np.zeros((1, LANES), np.float32)
    for l in range(LANES):
        hl = l % HEAD_DIM
        if hl < ROPE_DIM:
            j = hl % ROPE_HALF
            e[j, l] = 1.0
            e[ROPE_HALF + j, LANES + l] = -1.0 if hl < ROPE_HALF else 1.0
        else:
            ones[0, l] = 1.0
    return e, ones


def _rmsnorm(x, g):
    ms = jnp.mean(x * x, axis=-1, keepdims=True)
    return x * lax.rsqrt(ms + RMS_EPS) * g


def _split3(x):
    a = x.astype(BF16)
    r = x - a.astype(F32)
    b = r.astype(BF16)
    c = (r - b.astype(F32)).astype(BF16)
    return a, b, c


def _mixer_kernel(sinks_ref, x_ref, cs_ref, erope_ref, ones_ref, g1_ref, win_ref, bin_ref, wab_ref,
                  pw_ref, ps_ref, wpb_ref, wout_ref, g2_ref, wr2_ref, brc_ref,
                  x1_ref, h2_ref, route_ref, cnt_ref,
                  kprev, vprev, uprev, q_scr, attn_scr, ga_scr, gp_scr, yat_scr, *, blocks_per_seq):
    i = pl.program_id(0)
    blk_in_seq = lax.rem(i, blocks_per_seq)
    seq_start = blk_in_seq == 0

    @pl.when(seq_start)
    def _():
        kprev[...] = jnp.zeros_like(kprev)
        vprev[...] = jnp.zeros_like(vprev)
        uprev[...] = jnp.zeros_like(uprev)

    x = x_ref[...]
    h = _rmsnorm(x, g1_ref[...]).astype(BF16)

    c1, c2, c3 = _split3(cs_ref[...])
    er = erope_ref[...]
    tab = (lax.dot_general(c1, er, TN_DIMS, preferred_element_type=F32)
           + lax.dot_general(c2, er, TN_DIMS, preferred_element_type=F32)
           + lax.dot_general(c3, er, TN_DIMS, preferred_element_type=F32))
    cos_t = tab[:, :LANES] + ones_ref[...]
    sin_t = tab[:, LANES:]
    lane = lax.broadcasted_iota(jnp.int32, (TM, LANES), 1)
    first_half = (lane & (HEAD_DIM - 1)) < ROPE_HALF

    def rope(t):
        partner = jnp.where(first_half, pltpu.roll(t, LANES - ROPE_HALF, 1), pltpu.roll(t, ROPE_HALF, 1))
        return t * cos_t + partner * sin_t

    q = jnp.dot(h, win_ref[:, 0:Q_WIDTH], preferred_element_type=F32) + bin_ref[:, 0:Q_WIDTH]
    for c in range(Q_WIDTH // LANES):
        qc = rope(q[:, c * LANES:(c + 1) * LANES]) * (HEAD_DIM ** -0.5)
        q_scr[:, c * LANES:(c + 1) * LANES] = qc.astype(BF16)

    kvu = jnp.dot(h, win_ref[:, O_K:O_GA], preferred_element_type=F32) + bin_ref[:, O_K:O_GA]
    k = rope(kvu[:, 0:KV_WIDTH])
    v = kvu[:, KV_WIDTH:2 * KV_WIDTH]
    u = kvu[:, 2 * KV_WIDTH:]

    lane_b = lax.broadcasted_iota(jnp.int32, (TM + WINDOW, LANES), 1)
    low = lane_b < HEAD_DIM

    def head_bands(prev_ref, cur):
        band = jnp.concatenate([prev_ref[...], cur], axis=0)
        swapped = pltpu.roll(band, HEAD_DIM, 1)
        zero = jnp.zeros_like(band)
        a0 = jnp.where(low, band, zero).astype(BF16)
        b0 = jnp.where(low, zero, swapped).astype(BF16)
        a1 = jnp.where(low, swapped, zero).astype(BF16)
        b1 = jnp.where(low, zero, band).astype(BF16)
        return ((a0, b0), (a1, b1))

    kb = head_bands(kprev, k)
    vb = head_bands(vprev, v)
    kprev[...] = k[TM - WINDOW:, :]
    vprev[...] = v[TM - WINDOW:, :]

    qi = lax.broadcasted_iota(jnp.int32, (WINDOW, 2 * WINDOW), 0)
    kj = lax.broadcasted_iota(jnp.int32, (WINDOW, 2 * WINDOW), 1)
    mask_mid = jnp.logical_or(jnp.logical_and(kj < WINDOW, kj > qi),
                              jnp.logical_and(kj >= WINDOW, (kj - WINDOW) <= qi))
    first_key = jnp.where(seq_start, WINDOW, 0)
    mask_first = jnp.logical_and(mask_mid, kj >= first_key)

    low_w = lax.broadcasted_iota(jnp.int32, (WINDOW, LANES), 1) < HEAD_DIM
    n_chunks = GROUP // 2
    ones_rows = lax.broadcasted_iota(jnp.int32, (4 * WINDOW, LANES), 0)
    ones_lanes = lax.broadcasted_iota(jnp.int32, (4 * WINDOW, LANES), 1)
    sum_cols = jnp.where((ones_rows < 2 * WINDOW) == (ones_lanes < HEAD_DIM), 1.0, 0.0).astype(BF16)

    pairs = [(j, hkv) for j in range(TM // WINDOW) for hkv in range(N_KV_HEADS)]

    def scores(j, hkv):
        rows = slice(j * WINDOW, (j + 1) * WINDOW)
        band_rows = slice(j * WINDOW, j * WINDOW + 2 * WINDOW)
        q4 = jnp.concatenate(
            [q_scr[rows, (hkv * n_chunks + c) * LANES:(hkv * n_chunks + c + 1) * LANES]
             for c in range(n_chunks)], axis=0)
        kk = jnp.concatenate([kb[hkv][0][band_rows], kb[hkv][1][band_rows]], axis=0)
        return lax.dot_general(q4, kk, NT_DIMS, preferred_element_type=F32)

    def attend(j, hkv, s):
        mask = mask_first if j == 0 else mask_mid
        rows = slice(j * WINDOW, (j + 1) * WINDOW)
        band_rows = slice(j * WINDOW, j * WINDOW + 2 * WINDOW)
        vv = jnp.concatenate([vb[hkv][0][band_rows], vb[hkv][1][band_rows]], axis=0)
        p_rows, sink_terms = [], []
        for c in range(n_chunks):
            p_cols, st = [], []
            for par in range(2):
                sink = sinks_ref[hkv * GROUP + 2 * c + par]
                sc = s[c * WINDOW:(c + 1) * WINDOW, par * 2 * WINDOW:(par + 1) * 2 * WINDOW]
                sc = jnp.where(mask, sc, NEG_INF)
                m = jnp.maximum(jnp.max(sc, axis=-1, keepdims=True), sink)
                p_cols.append(jnp.exp(sc - m).astype(BF16))
                st.append(jnp.exp(sink - m))
            p_rows.append(jnp.concatenate(p_cols, axis=1))
            sink_terms.append(st)
        p = jnp.concatenate(p_rows, axis=0)
        o2 = jnp.dot(p, jnp.concatenate([vv, sum_cols], axis=1), preferred_element_type=F32)
        for c in range(n_chunks):
            den = o2[c * WINDOW:(c + 1) * WINDOW, LANES:] + jnp.where(low_w, sink_terms[c][0], sink_terms[c][1])
            oc = o2[c * WINDOW:(c + 1) * WINDOW, :LANES] / den
            col = (hkv * n_chunks + c) * LANES
            attn_scr[rows, col:col + LANES] = oc.astype(BF16)

    half = D_MODEL // 2

    def gate(scr, offset, part):
        cols = slice(offset + part * half, offset + (part + 1) * half)
        scr[:, part * half:(part + 1) * half] = jax.nn.sigmoid(
            jnp.dot(h, win_ref[:, cols], preferred_element_type=F32) + bin_ref[:, cols])

    def attn_branch(j):
        rows = slice(j * WINDOW, (j + 1) * WINDOW)
        yat_scr[rows, :] = jnp.dot(attn_scr[rows, :], wab_ref[...], preferred_element_type=F32)

    def pooling():
        ext = jnp.concatenate([uprev[...], u], axis=0)
        uprev[...] = u[TM - POOL_HALO:, :]
        pos_in_seq = blk_in_seq * TM + lax.broadcasted_iota(jnp.int32, (TM, 1), 0)
        mixed = []
        for g, w in enumerate(POOL_WINDOWS):
            cols = slice(g * POOL_GROUP_WIDTH, (g + 1) * POOL_GROUP_WIDTH)
            sg = ext[:, cols]
            step = 1
            while step < w:
                sg = sg + pltpu.roll(sg, step, 0)
                step *= 2
            count = jnp.minimum(pos_in_seq + 1, w).astype(F32)
            pooled = sg[POOL_HALO:, :] / count - u[:, cols]
            mg = jnp.dot(pooled.astype(BF16), pw_ref[g], preferred_element_type=F32)
            mixed.append((mg * ps_ref[:, cols]).astype(BF16))
        return jnp.dot(jnp.concatenate(mixed, axis=1), wpb_ref[...], preferred_element_type=F32)

    fillers = [functools.partial(gate, ga_scr, O_GA, 0), functools.partial(gate, ga_scr, O_GA, 1),
               functools.partial(gate, gp_scr, O_GP, 0), functools.partial(gate, gp_scr, O_GP, 1), pooling]
    n_row_blocks = TM // WINDOW
    branch_done = 0
    y_pool = None
    s_next = scores(*pairs[0])
    for n, (j, hkv) in enumerate(pairs):
        s_cur = s_next
        if n + 1 < len(pairs):
            s_next = scores(*pairs[n + 1])
        if n < len(fillers):
            out = fillers[n]()
            y_pool = out if out is not None else y_pool
        else:
            while branch_done < n_row_blocks and N_KV_HEADS * (branch_done + 1) <= n:
                attn_branch(branch_done)
                branch_done += 1
        attend(j, hkv, s_cur)
    for jb in range(branch_done, n_row_blocks):
        attn_branch(jb)

    hm = TM // 2
    e_iota = lax.broadcasted_iota(jnp.int32, (N_EXPERTS, hm), 0)

    def out_proj(part):
        rows = slice(part * hm, (part + 1) * hm)
        merged = (ga_scr[rows, :] * yat_scr[rows, :] + gp_scr[rows, :] * y_pool[rows, :]).astype(BF16)
        x1 = x[rows, :] + jnp.dot(merged, wout_ref[...], preferred_element_type=F32)
        x1_ref[rows, :] = x1
        return x1

    def route(part, x1):
        h2 = _rmsnorm(x1, g2_ref[...])
        for c in range(ROW_TILE):
            h2_ref[_chunk_of_tokens(c, hm, part * hm), :] = h2[:, c * LANES:(c + 1) * LANES]
        ha = h2.astype(BF16)
        hb = (h2 - ha.astype(F32)).astype(BF16)
        both = lax.dot_general(wr2_ref[...], ha, NT_DIMS, preferred_element_type=F32)
        logits = (both[:N_EXPERTS] + both[N_EXPERTS:]
                  + lax.dot_general(wr2_ref[0:N_EXPERTS, :], hb, NT_DIMS, preferred_element_type=F32)
                  + brc_ref[...])
        vals, idx, hot = [], [], []
        work = logits
        for _ in range(TOP_K):
            m = jnp.max(work, axis=0, keepdims=True)
            ix = jnp.min(jnp.where(work == m, e_iota, N_EXPERTS), axis=0, keepdims=True)
            is_hot = e_iota == ix
            vals.append(m)
            idx.append(ix)
            hot.append(jnp.where(is_hot, 1.0, 0.0))
            work = jnp.where(is_hot, -jnp.inf, work)
        ex = [jnp.exp(vk - vals[0]) for vk in vals]
        den = ex[0] + ex[1] + ex[2] + ex[3]
        return idx, [e / den for e in ex], hot

    x1_parts = [out_proj(0), out_proj(1)]
    routed = [route(part, x1_parts[part]) for part in range(2)]
    idxs = [jnp.concatenate([routed[0][0][k], routed[1][0][k]], axis=1) for k in range(TOP_K)]
    gates = [jnp.concatenate([routed[0][1][k], routed[1][1][k]], axis=1) for k in range(TOP_K)]
    hots = [jnp.concatenate([routed[0][2][k], routed[1][2][k]], axis=1) for k in range(TOP_K)]

    picked = (hots[0] + hots[1] + hots[2] + hots[3]).astype(BF16)
    t_row = lax.broadcasted_iota(jnp.int32, (TM, TM), 0)
    t_col = lax.broadcasted_iota(jnp.int32, (TM, TM), 1)
    earlier = jnp.where(t_row < t_col, 1.0, 0.0).astype(BF16)
    before = jnp.dot(picked, earlier, preferred_element_type=F32)
    cnt_ref[0] = jnp.dot(picked, jnp.ones((TM, LANES), BF16), preferred_element_type=F32).astype(jnp.int32)

    for kk_ in range(TOP_K):
        local_rank = jnp.sum(hots[kk_] * before, axis=0, keepdims=True)
        route_ref[kk_:kk_ + 1, :] = idxs[kk_].astype(F32)
        route_ref[TOP_K + kk_:TOP_K + kk_ + 1, :] = local_rank
        route_ref[2 * TOP_K + kk_:2 * TOP_K + kk_ + 1, :] = gates[kk_]
    route_ref[3 * TOP_K:, :] = jnp.zeros((ROUTE_ROWS - 3 * TOP_K, TM), F32)


def _mixer(x2d, cs, sinks, g1, w_in, b_in, w_ab, pool_w, pool_scale, w_pb, w_out, g2, w_r, b_r, seq_len):
    t = x2d.shape[0]
    erope, ones = _rope_expand_matrix()
    w_rh = w_r.astype(BF16)
    w_rl = (w_r - w_rh.astype(F32)).astype(BF16)
    w_r2 = jnp.concatenate([w_rh.T, w_rl.T], axis=0)
    row = lambda i: (i, 0)
    in_specs = [
        pl.BlockSpec(memory_space=pltpu.SMEM),
        pl.BlockSpec((TM, D_MODEL), row),
        pl.BlockSpec((2 * ROPE_HALF, TM), lambda i: (0, i)),
        _const_spec((2 * ROPE_HALF, 2 * LANES)),
        _const_spec((1, LANES)),
        _const_spec((1, D_MODEL)),
        _const_spec((D_MODEL, IN_WIDTH)),
        _const_spec((1, IN_WIDTH)),
        _const_spec((Q_WIDTH, D_MODEL)),
        _const_spec((len(POOL_WINDOWS), POOL_GROUP_WIDTH, POOL_GROUP_WIDTH)),
        _const_spec((1, POOL_WIDTH)),
        _const_spec((POOL_WIDTH, D_MODEL)),
        _const_spec((D_MODEL, D_MODEL)),
        _const_spec((1, D_MODEL)),
        _const_spec((2 * N_EXPERTS, D_MODEL)),
        _const_spec((N_EXPERTS, 1)),
    ]
    out_specs = [pl.BlockSpec((TM, D_MODEL), row), pl.BlockSpec((TM * ROW_TILE, LANES), row),
                 pl.BlockSpec((ROUTE_ROWS, TM), lambda i: (0, i)),
                 pl.BlockSpec((1, N_EXPERTS, LANES), lambda i: (i, 0, 0))]
    out_shape = [jax.ShapeDtypeStruct((t, D_MODEL), F32), jax.ShapeDtypeStruct((t * ROW_TILE, LANES), F32),
                 jax.ShapeDtypeStruct((ROUTE_ROWS, t), F32),
                 jax.ShapeDtypeStruct((t // TM, N_EXPERTS, LANES), jnp.int32)]
    scratch = [pltpu.VMEM((WINDOW, KV_WIDTH), F32), pltpu.VMEM((WINDOW, KV_WIDTH), F32),
               pltpu.VMEM((POOL_HALO, POOL_WIDTH), F32),
               pltpu.VMEM((TM, Q_WIDTH), BF16), pltpu.VMEM((TM, Q_WIDTH), BF16),
               pltpu.VMEM((TM, D_MODEL), F32), pltpu.VMEM((TM, D_MODEL), F32), pltpu.VMEM((TM, D_MODEL), F32)]
    return pl.pallas_call(
        functools.partial(_mixer_kernel, blocks_per_seq=seq_len // TM),
        grid=(t // TM,),
        in_specs=in_specs, out_specs=out_specs, out_shape=out_shape, scratch_shapes=scratch,
        compiler_params=pltpu.CompilerParams(dimension_semantics=("arbitrary",), vmem_limit_bytes=MIX_VMEM),
        name="mixer",
    )(sinks, x2d, cs, jnp.asarray(erope, BF16), jnp.asarray(ones), g1.reshape(1, -1), w_in.astype(BF16),
      b_in.reshape(1, -1), w_ab.astype(BF16), pool_w.astype(BF16), pool_scale.reshape(1, -1),
      w_pb.astype(BF16), w_out.astype(BF16), g2.reshape(1, -1), w_r2, b_r.reshape(-1, 1))


def _tile_copy(src, src_token, dst, dst_token, sem):
    return pltpu.make_async_copy(src.at[_tile_of_token(src_token)], dst.at[_tile_of_token(dst_token)], sem)


def _dispatch_kernel(dest_ref, pad_ref, h2_ref, xs_ref, zero_tile, sem, *, pads_per_step):
    zero_tile[...] = jnp.zeros_like(zero_tile)
    copies = [_tile_copy(h2_ref, r, xs_ref, dest_ref[k, r], sem) for r in range(TD) for k in range(TOP_K)]
    copies += [pltpu.make_async_copy(zero_tile, xs_ref.at[_tile_of_token(pad_ref[0, p])], sem)
               for p in range(pads_per_step)]
    for a, cp in enumerate(copies):
        cp.start(priority=a % 2)
    for cp in copies:
        cp.wait()


def _dispatch(h2, dest_t, pad_slots, n_slots):
    t = h2.shape[0] // ROW_TILE
    n_steps = t // TD
    pads_per_step = pad_slots.shape[0] // n_steps
    assert pads_per_step * n_steps == pad_slots.shape[0]
    return pl.pallas_call(
        functools.partial(_dispatch_kernel, pads_per_step=pads_per_step),
        grid=(n_steps,),
        in_specs=[pl.BlockSpec((TOP_K, TD), lambda i: (0, i), memory_space=pltpu.SMEM),
                  pl.BlockSpec((1, pads_per_step), lambda i: (0, i), memory_space=pltpu.SMEM),
                  pl.BlockSpec((TD * ROW_TILE, LANES), lambda i: (i, 0))],
        out_specs=pl.BlockSpec(memory_space=pl.ANY),
        out_shape=jax.ShapeDtypeStruct((n_slots * ROW_TILE, LANES), F32),
        scratch_shapes=[pltpu.VMEM((ROW_TILE, LANES), F32), pltpu.SemaphoreType.DMA(())],
        compiler_params=pltpu.CompilerParams(dimension_semantics=("arbitrary",), has_side_effects=True),
        name="dispatch",
    )(dest_t, pad_slots.reshape(1, -1), h2)


def _experts_kernel(be_ref, first_ref, rows_ref, next_ref, par_ref, xs_ref, bup_ref, bdn_ref, wup_hbm, wdn_hbm,
                    ys_ref, wup_f32, wdn_f32, wup_bf, wdn_bf, sems):
    for j in range(STEP_BLOCKS):
        _expert_block(pl.program_id(0) * STEP_BLOCKS + j, j, be_ref, first_ref, rows_ref, next_ref, par_ref, xs_ref,
                      bup_ref, bdn_ref, wup_hbm, wdn_hbm, ys_ref, wup_f32, wdn_f32, wup_bf, wdn_bf, sems)


def _expert_block(i, j, be_ref, first_ref, rows_ref, next_ref, par_ref, xs_ref, bup_ref, bdn_ref, wup_hbm, wdn_hbm,
                  ys_ref, wup_f32, wdn_f32, wup_bf, wdn_bf, sems):
    cast_rows = 128
    n_rows = rows_ref[i]
    expert = be_ref[i]
    first_token = j * BM

    def weight_copies(expert, par):
        return (pltpu.make_async_copy(wup_hbm.at[expert], wup_f32.at[par], sems.at[0, par]),
                pltpu.make_async_copy(wdn_hbm.at[expert], wdn_f32.at[par], sems.at[1, par]))

    @pl.when(first_ref[i] == 1)
    def _():
        par, nxt = par_ref[i], next_ref[i]

        @pl.when(i == 0)
        def _():
            for cp in weight_copies(expert, par):
                cp.start()

        for cp in weight_copies(expert, par):
            cp.wait()

        @pl.when(nxt >= 0)
        def _():
            for cp in weight_copies(nxt, 1 - par):
                cp.start()

        def cast_up(r, carry):
            rows = pl.ds(pl.multiple_of(r * cast_rows, cast_rows), cast_rows)
            wup_bf[rows, :] = wup_f32[par, rows, :].astype(BF16)
            return carry

        def cast_dn(r, carry):
            rows = pl.ds(pl.multiple_of(r * cast_rows, cast_rows), cast_rows)
            wdn_bf[rows, :] = wdn_f32[par, rows, :].astype(BF16)
            return carry

        lax.fori_loop(0, D_MODEL // cast_rows, cast_up, 0)
        lax.fori_loop(0, D_FF // cast_rows, cast_dn, 0)

    @pl.when(n_rows > 0)
    def _():
        xb = jnp.concatenate([xs_ref[_chunk_of_tokens(c, BM, first_token), :].astype(BF16)
                              for c in range(ROW_TILE)], axis=1)
        b_up = bup_ref[pl.ds(expert, 1), :]
        b_dn = bdn_ref[pl.ds(expert, 1), :]
        up = jnp.dot(xb, wup_bf[...], preferred_element_type=F32) + b_up
        glu = jnp.minimum(up[:, :D_FF], SWIGLU_LIMIT)
        lin = jnp.clip(up[:, D_FF:], -SWIGLU_LIMIT, SWIGLU_LIMIT)
        act = (glu * jax.nn.sigmoid(SWIGLU_ALPHA * glu) * (lin + 1.0)).astype(BF16)
        chunks_per_tile = OUT_TILE // LANES
        for n in range(D_MODEL // OUT_TILE):
            cols = slice(n * OUT_TILE, (n + 1) * OUT_TILE)
            y = jnp.dot(act, wdn_bf[:, cols], preferred_element_type=F32) + b_dn[:, cols]
            for c in range(chunks_per_tile):
                ys_ref[_chunk_of_tokens(n * chunks_per_tile + c, BM, first_token), :] = (
                    y[:, c * LANES:(c + 1) * LANES])

    @pl.when(n_rows == 0)
    def _():
        ys_ref[pl.ds(first_token * ROW_TILE, BM * ROW_TILE), :] = jnp.zeros((BM * ROW_TILE, LANES), F32)


def _experts(xs, blk_expert, blk_first, blk_rows, blk_next, blk_par, w_up, b_up, w_down, b_down):
    n_blocks = xs.shape[0] // (ROW_TILE * BM)
    assert n_blocks % STEP_BLOCKS == 0
    slot_blk = lambda i, *_: (i, 0)
    grid_spec = pltpu.PrefetchScalarGridSpec(
        num_scalar_prefetch=5,
        grid=(n_blocks // STEP_BLOCKS,),
        in_specs=[
            pl.BlockSpec((STEP_BLOCKS * BM * ROW_TILE, LANES), slot_blk),
            _const_spec((N_EXPERTS, 2 * D_FF)),
            _const_spec((N_EXPERTS, D_MODEL)),
            pl.BlockSpec(memory_space=pl.ANY),
            pl.BlockSpec(memory_space=pl.ANY),
        ],
        out_specs=pl.BlockSpec((STEP_BLOCKS * BM * ROW_TILE, LANES), slot_blk),
        scratch_shapes=[pltpu.VMEM((2, D_MODEL, 2 * D_FF), F32), pltpu.VMEM((2, D_FF, D_MODEL), F32),
                        pltpu.VMEM((D_MODEL, 2 * D_FF), BF16), pltpu.VMEM((D_FF, D_MODEL), BF16),
                        pltpu.SemaphoreType.DMA((2, 2))],
    )
    return pl.pallas_call(
        _experts_kernel,
        grid_spec=grid_spec,
        out_shape=jax.ShapeDtypeStruct(xs.shape, F32),
        compiler_params=pltpu.CompilerParams(dimension_semantics=("arbitrary",), vmem_limit_bytes=MOE_VMEM),
        name="experts",
    )(blk_expert, blk_first, blk_rows, blk_next, blk_par, xs, b_up, b_down, w_up, w_down)


def _combine_kernel(dest_ref, dest_next_ref, x1_ref, gate_ref, gf_ref, ys_ref, out_ref, buf, sems):
    i = pl.program_id(0)
    slot = lax.rem(i, 2)

    def gather(d_ref, s):
        return [_tile_copy(ys_ref, d_ref[k, r], buf.at[s, k], r, sems.at[s])
                for r in range(TB) for k in range(TOP_K)]

    @pl.when(i == 0)
    def _():
        for a, cp in enumerate(gather(dest_ref, slot)):
            cp.start(priority=a % 2)

    @pl.when(i + 1 < pl.num_programs(0))
    def _():
        for a, cp in enumerate(gather(dest_next_ref, 1 - slot)):
            cp.start(priority=a % 2)

    for cp in gather(dest_ref, slot):
        cp.wait()

    cols = []
    for c in range(ROW_TILE):
        acc = x1_ref[:, c * LANES:(c + 1) * LANES]
        for k in range(TOP_K):
            acc = acc + gate_ref[:, k:k + 1] * buf[slot, k, _chunk_of_tokens(c, TB), :]
        cols.append(acc)
    out_ref[...] = _rmsnorm(jnp.concatenate(cols, axis=1), gf_ref[...])


def _combine(x1, gates, ys, dest_t, gf):
    t = x1.shape[0]
    n_blk = t // TB
    return pl.pallas_call(
        _combine_kernel,
        grid=(n_blk,),
        in_specs=[pl.BlockSpec((TOP_K, TB), lambda i: (0, i), memory_space=pltpu.SMEM),
                  pl.BlockSpec((TOP_K, TB), lambda i: (0, jnp.minimum(i + 1, n_blk - 1)),
                               memory_space=pltpu.SMEM),
                  pl.BlockSpec((TB, D_MODEL), lambda i: (i, 0)),
                  pl.BlockSpec((TB, TOP_K), lambda i: (i, 0)),
                  _const_spec((1, D_MODEL)),
                  pl.BlockSpec(memory_space=pl.ANY)],
        out_specs=pl.BlockSpec((TB, D_MODEL), lambda i: (i, 0)),
        out_shape=jax.ShapeDtypeStruct((t, D_MODEL), F32),
        scratch_shapes=[pltpu.VMEM((2, TOP_K, TB * ROW_TILE, LANES), F32), pltpu.SemaphoreType.DMA((2,))],
        compiler_params=pltpu.CompilerParams(dimension_semantics=("arbitrary",), vmem_limit_bytes=COMBINE_VMEM),
        name="combine",
    )(dest_t, dest_t, x1, gates, gf.reshape(1, -1), ys)


def _pick(member, values):
    return jnp.sum(jnp.where(member, values[:, None], 0), axis=0, dtype=jnp.int32)


def _routing(top_idx_t, local_rank_t, blk_counts):
    t = top_idx_t.shape[1]
    n_slots = t * TOP_K + N_EXPERTS * BM
    n_blocks = n_slots // BM
    ids = jnp.arange(N_EXPERTS, dtype=jnp.int32)
    counts = jnp.sum(blk_counts, axis=0)
    padded = (counts + BM - 1) // BM * BM
    pend = jnp.cumsum(padded)
    pstart = pend - padded
    base = pstart[None, :] + jnp.cumsum(blk_counts, axis=0) - blk_counts
    base_tok_t = jnp.repeat(base.T, TM, axis=1)
    dest_t = local_rank_t + jnp.sum(
        jnp.where(top_idx_t[:, None, :] == ids[None, :, None], base_tok_t[None], 0), axis=1)

    blk_start = jnp.arange(n_blocks, dtype=jnp.int32) * BM
    member = (pstart[:, None] <= blk_start[None, :]) & (blk_start[None, :] < pend[:, None])
    blk_expert = _pick(member, ids)
    blk_rows = jnp.sum(jnp.where(member, jnp.clip((pstart + counts)[:, None] - blk_start[None, :], 0, BM), 0),
                       axis=0, dtype=jnp.int32)
    blk_first = _pick(member & (blk_start[None, :] == pstart[:, None]), jnp.ones_like(ids))
    used = (counts > 0).astype(jnp.int32)
    blk_par = _pick(member, (jnp.cumsum(used) - used) % 2)
    later_used = (used > 0)[None, :] & (ids[None, :] > ids[:, None])
    next_used = jnp.min(jnp.where(later_used, ids[None, :], N_EXPERTS), axis=1)
    blk_next = _pick(member, jnp.where(next_used == N_EXPERTS, -1, next_used))

    seg_size = jnp.concatenate([padded - counts, n_slots - pend[-1:]])
    seg_slot = jnp.concatenate([pstart + counts, pend[-1:]])
    seg_end = jnp.cumsum(seg_size)
    seg_begin = seg_end - seg_size
    ordinal = jnp.arange(n_slots - t * TOP_K, dtype=jnp.int32)
    in_seg = (seg_begin[:, None] <= ordinal[None, :]) & (ordinal[None, :] < seg_end[:, None])
    pad_slots = ordinal + _pick(in_seg, seg_slot - seg_begin)
    return (dest_t.astype(jnp.int32), pad_slots, blk_expert, blk_first, blk_rows, blk_next, blk_par, n_slots)


def kernel(x, positions, norm1_g, w_in, b_in, attn_sinks, w_attn_br, pool_w, pool_scale, w_pool_br, w_out,
           norm2_g, w_router, b_router, w_up, b_up, w_down, b_down, normf_g):
    b, s, d = x.shape
    t = b * s
    assert norm1_g.shape[0] == 1, "single-layer problem: the combine kernel applies the final RMSNorm"
    assert d == D_MODEL and s % TM == 0 and t % ROPE_TN == 0 and t % TD == 0 and t % TB == 0
    inv_freq = jnp.power(jnp.float32(ROPE_THETA), -jnp.arange(ROPE_HALF, dtype=F32) * (2.0 / ROPE_DIM))
    cs = _rope_table(positions.reshape(1, t), inv_freq)
    x1, h2, route, blk_counts = _mixer(
        x.reshape(t, d), cs, attn_sinks[0], norm1_g[0], w_in[0], b_in[0], w_attn_br[0], pool_w[0],
        pool_scale[0], w_pool_br[0], w_out[0], norm2_g[0], w_router[0], b_router[0], s)
    top_idx_t = route[0:TOP_K].astype(jnp.int32)
    local_rank_t = route[TOP_K:2 * TOP_K].astype(jnp.int32)
    gates = route[2 * TOP_K:3 * TOP_K].T
    dest_t, pad_slots, blk_expert, blk_first, blk_rows, blk_next, blk_par, n_slots = _routing(
        top_idx_t, local_rank_t, blk_counts[:, :, 0])
    xs = _dispatch(h2, dest_t, pad_slots, n_slots)
    ys = _experts(xs, blk_expert, blk_first, blk_rows, blk_next, blk_par, w_up[0], b_up[0], w_down[0], b_down[0])
    out = _combine(x1, gates, ys, dest_t, normf_g)
    return out.reshape(b, s, d)
```

```python
import functools

import numpy as np
import jax
import jax.numpy as jnp
from jax import lax
from jax.experimental import pallas as pl
from jax.experimental.pallas import tpu as pltpu

F32 = jnp.float32
BF16 = jnp.bfloat16

D_MODEL = 1024
HEAD_DIM = 64
N_Q_HEADS = 16
N_KV_HEADS = 2
GROUP = N_Q_HEADS // N_KV_HEADS
WINDOW = 128
ROPE_THETA = 500000.0
ROPE_DIM = HEAD_DIM // 4
ROPE_HALF = ROPE_DIM // 2
Q_WIDTH = N_Q_HEADS * HEAD_DIM
KV_WIDTH = N_KV_HEADS * HEAD_DIM
NEG_INF = -1e30
LOG2_E = 1.4426950408889634
POOL_WINDOWS = (2, 4, 8, 16)
POOL_WIDTH = D_MODEL // 2
POOL_GROUP_WIDTH = POOL_WIDTH // len(POOL_WINDOWS)
POOL_HALO = 16
IN_WIDTH = Q_WIDTH + 2 * KV_WIDTH + POOL_WIDTH + 2 * D_MODEL
O_K = Q_WIDTH
O_V = O_K + KV_WIDTH
O_P = O_V + KV_WIDTH
O_GA = O_P + POOL_WIDTH
O_GP = O_GA + D_MODEL
N_EXPERTS = 32
TOP_K = 4
D_FF = D_MODEL
SWIGLU_ALPHA = 1.702
SWIGLU_LIMIT = 7.0
RMS_EPS = 1e-5

LANES = 128
ROW_TILE = D_MODEL // LANES
ROUTE_ROWS = 16
TM = 512
TD = 512
TB = 256
COMBINE_BUFS = 3
BM = 256
STEP_BLOCKS = 4
OUT_TILE = 256
ROPE_TN = 2048
V7X_VMEM_BYTES = 64 * 1024 * 1024
MIX_VMEM = V7X_VMEM_BYTES - 8 * 1024 * 1024
MOE_VMEM = V7X_VMEM_BYTES - 6 * 1024 * 1024
COMBINE_VMEM = V7X_VMEM_BYTES // 2
NT_DIMS = (((1,), (1,)), ((), ()))
TN_DIMS = (((0,), (0,)), ((), ()))


def _const_spec(shape):
    return pl.BlockSpec(shape, lambda *_: (0,) * len(shape))


def _chunk_of_tokens(c, n_tokens, first_token=0):
    return pl.ds(first_token * ROW_TILE + c, n_tokens, stride=ROW_TILE)


def _tile_of_token(token):
    if isinstance(token, int):
        return pl.ds(token * ROW_TILE, ROW_TILE)
    return pl.ds(pl.multiple_of(token * ROW_TILE, ROW_TILE), ROW_TILE)


def _rope_table_kernel(pos_ref, invf_ref, out_ref):
    ang = invf_ref[...] * pos_ref[...].astype(F32)
    out_ref[0:ROPE_HALF, :] = jnp.cos(ang)
    out_ref[ROPE_HALF:, :] = jnp.sin(ang)


def _rope_table(pos_row, inv_freq):
    t = pos_row.shape[1]
    return pl.pallas_call(
        _rope_table_kernel,
        grid=(t // ROPE_TN,),
        in_specs=[pl.BlockSpec((1, ROPE_TN), lambda i: (0, i)), _const_spec((ROPE_HALF, 1))],
        out_specs=pl.BlockSpec((2 * ROPE_HALF, ROPE_TN), lambda i: (0, i)),
        out_shape=jax.ShapeDtypeStruct((2 * ROPE_HALF, t), F32),
        name="rope_table",
    )(pos_row, inv_freq.reshape(ROPE_HALF, 1))


def _rope_expand_matrix():
    e = np.zeros((2 * ROPE_HALF, 2 * LANES), np.float32)
    ones = np.zeros((1, LANES), np.float32)
    for l in range(LANES):
        hl = l % HEAD_DIM
        if hl < ROPE_DIM:
            j = hl % ROPE_HALF
            e[j, l] = 1.0
            e[ROPE_HALF + j, LANES + l] = -1.0 if hl < ROPE_HALF else 1.0
        else:
            ones[0, l] = 1.0
    return e, ones


def _rmsnorm(x, g):
    ms = jnp.mean(x * x, axis=-1, keepdims=True)
    return x * lax.rsqrt(ms + RMS_EPS) * g


def _split3(x):
    a = x.astype(BF16)
    r = x - a.astype(F32)
    b = r.astype(BF16)
    c = (r - b.astype(F32)).astype(BF16)
    return a, b, c


def _mixer_kernel(sinks_ref, x_ref, cs_ref, erope_ref, ones_ref, g1_ref, win_ref, bin_ref, wab_ref,
                  pw_ref, ps_ref, wpb_ref, wout_ref, g2_ref, wr2_ref, brc_ref,
                  x1_ref, h2_ref, route_ref, cnt_ref,
                  kprev, vprev, uprev, q_scr, attn_scr, ga_scr, gp_scr, yat_scr, *, blocks_per_seq):
    i = pl.program_id(0)
    blk_in_seq = lax.rem(i, blocks_per_seq)
    seq_start = blk_in_seq == 0

    @pl.when(seq_start)
    def _():
        kprev[...] = jnp.zeros_like(kprev)
        vprev[...] = jnp.zeros_like(vprev)
        uprev[...] = jnp.zeros_like(uprev)

    x = x_ref[...]
    h = _rmsnorm(x, g1_ref[...]).astype(BF16)

    c1, c2, c3 = _split3(cs_ref[...])
    er = erope_ref[...]
    tab = (lax.dot_general(c1, er, TN_DIMS, preferred_element_type=F32)
           + lax.dot_general(c2, er, TN_DIMS, preferred_element_type=F32)
           + lax.dot_general(c3, er, TN_DIMS, preferred_element_type=F32))
    cos_t = tab[:, :LANES] + ones_ref[...]
    sin_t = tab[:, LANES:]
    lane = lax.broadcasted_iota(jnp.int32, (TM, LANES), 1)
    first_half = (lane & (HEAD_DIM - 1)) < ROPE_HALF

    def rope(t):
        partner = jnp.where(first_half, pltpu.roll(t, LANES - ROPE_HALF, 1), pltpu.roll(t, ROPE_HALF, 1))
        return t * cos_t + partner * sin_t

    q = jnp.dot(h, win_ref[:, 0:Q_WIDTH], preferred_element_type=F32) + bin_ref[:, 0:Q_WIDTH]
    for c in range(Q_WIDTH // LANES):
        qc = rope(q[:, c * LANES:(c + 1) * LANES]) * (HEAD_DIM ** -0.5 * LOG2_E)
        q_scr[:, c * LANES:(c + 1) * LANES] = qc.astype(BF16)

    kvu = jnp.dot(h, win_ref[:, O_K:O_GA], preferred_element_type=F32) + bin_ref[:, O_K:O_GA]
    k = rope(kvu[:, 0:KV_WIDTH])
    v = kvu[:, KV_WIDTH:2 * KV_WIDTH]
    u = kvu[:, 2 * KV_WIDTH:]

    lane_b = lax.broadcasted_iota(jnp.int32, (TM + WINDOW, LANES), 1)
    low = lane_b < HEAD_DIM

    def head_bands(prev_ref, cur):
        band = jnp.concatenate([prev_ref[...], cur], axis=0)
        swapped = pltpu.roll(band, HEAD_DIM, 1)
        zero = jnp.zeros_like(band)
        a0 = jnp.where(low, band, zero).astype(BF16)
        b0 = jnp.where(low, zero, swapped).astype(BF16)
        a1 = jnp.where(low, swapped, zero).astype(BF16)
        b1 = jnp.where(low, zero, band).astype(BF16)
        return ((a0, b0), (a1, b1))

    kb = head_bands(kprev, k)
    vb = head_bands(vprev, v)
    kprev[...] = k[TM - WINDOW:, :]
    vprev[...] = v[TM - WINDOW:, :]

    qi = lax.broadcasted_iota(jnp.int32, (WINDOW, 2 * WINDOW), 0)
    kj = lax.broadcasted_iota(jnp.int32, (WINDOW, 2 * WINDOW), 1)
    mask_mid = jnp.logical_or(jnp.logical_and(kj < WINDOW, kj > qi),
                              jnp.logical_and(kj >= WINDOW, (kj - WINDOW) <= qi))
    first_key = jnp.where(seq_start, WINDOW, 0)
    mask_first = jnp.logical_and(mask_mid, kj >= first_key)

    low_w = lax.broadcasted_iota(jnp.int32, (WINDOW, LANES), 1) < HEAD_DIM
    n_chunks = GROUP // 2
    ones_rows = lax.broadcasted_iota(jnp.int32, (4 * WINDOW, LANES), 0)
    ones_lanes = lax.broadcasted_iota(jnp.int32, (4 * WINDOW, LANES), 1)
    sum_cols = jnp.where((ones_rows < 2 * WINDOW) == (ones_lanes < HEAD_DIM), 1.0, 0.0).astype(BF16)

    pairs = [(j, hkv) for j in range(TM // WINDOW) for hkv in range(N_KV_HEADS)]

    def scores(j, hkv):
        rows = slice(j * WINDOW, (j + 1) * WINDOW)
        band_rows = slice(j * WINDOW, j * WINDOW + 2 * WINDOW)
        q4 = jnp.concatenate(
            [q_scr[rows, (hkv * n_chunks + c) * LANES:(hkv * n_chunks + c + 1) * LANES]
             for c in range(n_chunks)], axis=0)
        kk = jnp.concatenate([kb[hkv][0][band_rows], kb[hkv][1][band_rows]], axis=0)
        return lax.dot_general(q4, kk, NT_DIMS, preferred_element_type=F32)

    def attend(j, hkv, s):
        mask = mask_first if j == 0 else mask_mid
        rows = slice(j * WINDOW, (j + 1) * WINDOW)
        band_rows = slice(j * WINDOW, j * WINDOW + 2 * WINDOW)
        vv = jnp.concatenate([vb[hkv][0][band_rows], vb[hkv][1][band_rows]], axis=0)
        p_rows, sink_terms = [], []
        for c in range(n_chunks):
            p_cols, st = [], []
            for par in range(2):
                sink = sinks_ref[hkv * GROUP + 2 * c + par] * LOG2_E
                sc = s[c * WINDOW:(c + 1) * WINDOW, par * 2 * WINDOW:(par + 1) * 2 * WINDOW]
                sc = jnp.where(mask, sc, NEG_INF)
                m = jnp.maximum(jnp.max(sc, axis=-1, keepdims=True), sink)
                p_cols.append(jnp.exp2(sc - m).astype(BF16))
                st.append(jnp.exp2(sink - m))
            p_rows.append(jnp.concatenate(p_cols, axis=1))
            sink_terms.append(st)
        p = jnp.concatenate(p_rows, axis=0)
        o2 = jnp.dot(p, jnp.concatenate([vv, sum_cols], axis=1), preferred_element_type=F32)
        for c in range(n_chunks):
            den = o2[c * WINDOW:(c + 1) * WINDOW, LANES:] + jnp.where(low_w, sink_terms[c][0], sink_terms[c][1])
            oc = o2[c * WINDOW:(c + 1) * WINDOW, :LANES] / den
            col = (hkv * n_chunks + c) * LANES
            attn_scr[rows, col:col + LANES] = oc.astype(BF16)

    half = D_MODEL // 2

    def gate(scr, offset, part):
        cols = slice(offset + part * half, offset + (part + 1) * half)
        scr[:, part * half:(part + 1) * half] = jax.nn.sigmoid(
            jnp.dot(h, win_ref[:, cols], preferred_element_type=F32) + bin_ref[:, cols])

    def attn_branch(j):
        rows = slice(j * WINDOW, (j + 1) * WINDOW)
        yat_scr[rows, :] = jnp.dot(attn_scr[rows, :], wab_ref[...], preferred_element_type=F32)

    def pooling():
        ext = jnp.concatenate([uprev[...], u], axis=0)
        uprev[...] = u[TM - POOL_HALO:, :]
        pos_in_seq = blk_in_seq * TM + lax.broadcasted_iota(jnp.int32, (TM, 1), 0)
        mixed = []
        for g, w in enumerate(POOL_WINDOWS):
            cols = slice(g * POOL_GROUP_WIDTH, (g + 1) * POOL_GROUP_WIDTH)
            sg = ext[:, cols]
            step = 1
            while step < w:
                sg = sg + pltpu.roll(sg, step, 0)
                step *= 2
            count = jnp.minimum(pos_in_seq + 1, w).astype(F32)
            pooled = sg[POOL_HALO:, :] / count - u[:, cols]
            mg = jnp.dot(pooled.astype(BF16), pw_ref[g], preferred_element_type=F32)
            mixed.append((mg * ps_ref[:, cols]).astype(BF16))
        return jnp.dot(jnp.concatenate(mixed, axis=1), wpb_ref[...], preferred_element_type=F32)

    fillers = [functools.partial(gate, ga_scr, O_GA, 0), functools.partial(gate, ga_scr, O_GA, 1),
               functools.partial(gate, gp_scr, O_GP, 0), functools.partial(gate, gp_scr, O_GP, 1), pooling]
    n_row_blocks = TM // WINDOW
    branch_done = 0
    y_pool = None
    s_next = scores(*pairs[0])
    for n, (j, hkv) in enumerate(pairs):
        s_cur = s_next
        if n + 1 < len(pairs):
            s_next = scores(*pairs[n + 1])
        if n < len(fillers):
            out = fillers[n]()
            y_pool = out if out is not None else y_pool
        else:
            while branch_done < n_row_blocks and N_KV_HEADS * (branch_done + 1) <= n:
                attn_branch(branch_done)
                branch_done += 1
        attend(j, hkv, s_cur)
    for jb in range(branch_done, n_row_blocks):
        attn_branch(jb)

    hm = TM // 2
    e_iota = lax.broadcasted_iota(jnp.int32, (N_EXPERTS, hm), 0)

    def out_proj(part):
        rows = slice(part * hm, (part + 1) * hm)
        merged = (ga_scr[rows, :] * yat_scr[rows, :] + gp_scr[rows, :] * y_pool[rows, :]).astype(BF16)
        x1 = x[rows, :] + jnp.dot(merged, wout_ref[...], preferred_element_type=F32)
        x1_ref[rows, :] = x1
        return x1

    def route(part, x1):
        h2 = _rmsnorm(x1, g2_ref[...])
        for c in range(ROW_TILE):
            h2_ref[_chunk_of_tokens(c, hm, part * hm), :] = h2[:, c * LANES:(c + 1) * LANES]
        ha = h2.astype(BF16)
        hb = (h2 - ha.astype(F32)).astype(BF16)
        both = lax.dot_general(wr2_ref[...], ha, NT_DIMS, preferred_element_type=F32)
        logits = (both[:N_EXPERTS] + both[N_EXPERTS:]
                  + lax.dot_general(wr2_ref[0:N_EXPERTS, :], hb, NT_DIMS, preferred_element_type=F32)
                  + brc_ref[...])
        vals, idx, hot = [], [], []
        work = logits
        for _ in range(TOP_K):
            m = jnp.max(work, axis=0, keepdims=True)
            ix = jnp.min(jnp.where(work == m, e_iota, N_EXPERTS), axis=0, keepdims=True)
            is_hot = e_iota == ix
            vals.append(m)
            idx.append(ix)
            hot.append(jnp.where(is_hot, 1.0, 0.0))
            work = jnp.where(is_hot, -jnp.inf, work)
        ex = [jnp.exp(vk - vals[0]) for vk in vals]
        den = ex[0] + ex[1] + ex[2] + ex[3]
        return idx, [e / den for e in ex], hot

    x1_parts = [out_proj(0), out_proj(1)]
    routed = [route(part, x1_parts[part]) for part in range(2)]
    idxs = [jnp.concatenate([routed[0][0][k], routed[1][0][k]], axis=1) for k in range(TOP_K)]
    gates = [jnp.concatenate([routed[0][1][k], routed[1][1][k]], axis=1) for k in range(TOP_K)]
    hots = [jnp.concatenate([routed[0][2][k], routed[1][2][k]], axis=1) for k in range(TOP_K)]

    picked = (hots[0] + hots[1] + hots[2] + hots[3]).astype(BF16)
    t_row = lax.broadcasted_iota(jnp.int32, (TM, TM), 0)
    t_col = lax.broadcasted_iota(jnp.int32, (TM, TM), 1)
    earlier = jnp.where(t_row < t_col, 1.0, 0.0).astype(BF16)
    before = jnp.dot(picked, earlier, preferred_element_type=F32)
    cnt_ref[0] = jnp.dot(picked, jnp.ones((TM, LANES), BF16), preferred_element_type=F32).astype(jnp.int32)

    for kk_ in range(TOP_K):
        local_rank = jnp.sum(hots[kk_] * before, axis=0, keepdims=True)
        route_ref[kk_:kk_ + 1, :] = idxs[kk_].astype(F32)
        route_ref[TOP_K + kk_:TOP_K + kk_ + 1, :] = local_rank
        route_ref[2 * TOP_K + kk_:2 * TOP_K + kk_ + 1, :] = gates[kk_]
    route_ref[3 * TOP_K:, :] = jnp.zeros((ROUTE_ROWS - 3 * TOP_K, TM), F32)


def _mixer(x2d, cs, sinks, g1, w_in, b_in, w_ab, pool_w, pool_scale, w_pb, w_out, g2, w_r, b_r, seq_len):
    t = x2d.shape[0]
    erope, ones = _rope_expand_matrix()
    w_rh = w_r.astype(BF16)
    w_rl = (w_r - w_rh.astype(F32)).astype(BF16)
    w_r2 = jnp.concatenate([w_rh.T, w_rl.T], axis=0)
    row = lambda i: (i, 0)
    in_specs = [
        pl.BlockSpec(memory_space=pltpu.SMEM),
        pl.BlockSpec((TM, D_MODEL), row),
        pl.BlockSpec((2 * ROPE_HALF, TM), lambda i: (0, i)),
        _const_spec((2 * ROPE_HALF, 2 * LANES)),
        _const_spec((1, LANES)),
        _const_spec((1, D_MODEL)),
        _const_spec((D_MODEL, IN_WIDTH)),
        _const_spec((1, IN_WIDTH)),
        _const_spec((Q_WIDTH, D_MODEL)),
        _const_spec((len(POOL_WINDOWS), POOL_GROUP_WIDTH, POOL_GROUP_WIDTH)),
        _const_spec((1, POOL_WIDTH)),
        _const_spec((POOL_WIDTH, D_MODEL)),
        _const_spec((D_MODEL, D_MODEL)),
        _const_spec((1, D_MODEL)),
        _const_spec((2 * N_EXPERTS, D_MODEL)),
        _const_spec((N_EXPERTS, 1)),
    ]
    out_specs = [pl.BlockSpec((TM, D_MODEL), row), pl.BlockSpec((TM * ROW_TILE, LANES), row),
                 pl.BlockSpec((ROUTE_ROWS, TM), lambda i: (0, i)),
                 pl.BlockSpec((1, N_EXPERTS, LANES), lambda i: (i, 0, 0))]
    out_shape = [jax.ShapeDtypeStruct((t, D_MODEL), F32), jax.ShapeDtypeStruct((t * ROW_TILE, LANES), F32),
                 jax.ShapeDtypeStruct((ROUTE_ROWS, t), F32),
                 jax.ShapeDtypeStruct((t // TM, N_EXPERTS, LANES), jnp.int32)]
    scratch = [pltpu.VMEM((WINDOW, KV_WIDTH), F32), pltpu.VMEM((WINDOW, KV_WIDTH), F32),
               pltpu.VMEM((POOL_HALO, POOL_WIDTH), F32),
               pltpu.VMEM((TM, Q_WIDTH), BF16), pltpu.VMEM((TM, Q_WIDTH), BF16),
               pltpu.VMEM((TM, D_MODEL), F32), pltpu.VMEM((TM, D_MODEL), F32), pltpu.VMEM((TM, D_MODEL), F32)]
    return pl.pallas_call(
        functools.partial(_mixer_kernel, blocks_per_seq=seq_len // TM),
        grid=(t // TM,),
        in_specs=in_specs, out_specs=out_specs, out_shape=out_shape, scratch_shapes=scratch,
        compiler_params=pltpu.CompilerParams(dimension_semantics=("arbitrary",), vmem_limit_bytes=MIX_VMEM),
        name="mixer",
    )(sinks, x2d, cs, jnp.asarray(erope, BF16), jnp.asarray(ones), g1.reshape(1, -1), w_in.astype(BF16),
      b_in.reshape(1, -1), w_ab.astype(BF16), pool_w.astype(BF16), pool_scale.reshape(1, -1),
      w_pb.astype(BF16), w_out.astype(BF16), g2.reshape(1, -1), w_r2, b_r.reshape(-1, 1))


def _tile_copy(src, src_token, dst, dst_token, sem):
    return pltpu.make_async_copy(src.at[_tile_of_token(src_token)], dst.at[_tile_of_token(dst_token)], sem)


def _dispatch_kernel(dest_ref, pad_ref, h2_ref, xs_ref, zero_tile, sem, *, pads_per_step):
    zero_tile[...] = jnp.zeros_like(zero_tile)
    copies = [_tile_copy(h2_ref, r, xs_ref, dest_ref[k, r], sem) for r in range(TD) for k in range(TOP_K)]
    copies += [pltpu.make_async_copy(zero_tile, xs_ref.at[_tile_of_token(pad_ref[0, p])], sem)
               for p in range(pads_per_step)]
    for a, cp in enumerate(copies):
        cp.start(priority=a % 2)
    for cp in copies:
        cp.wait()


def _dispatch(h2, dest_t, pad_slots, n_slots):
    t = h2.shape[0] // ROW_TILE
    n_steps = t // TD
    pads_per_step = pad_slots.shape[0] // n_steps
    assert pads_per_step * n_steps == pad_slots.shape[0]
    return pl.pallas_call(
        functools.partial(_dispatch_kernel, pads_per_step=pads_per_step),
        grid=(n_steps,),
        in_specs=[pl.BlockSpec((TOP_K, TD), lambda i: (0, i), memory_space=pltpu.SMEM),
                  pl.BlockSpec((1, pads_per_step), lambda i: (0, i), memory_space=pltpu.SMEM),
                  pl.BlockSpec((TD * ROW_TILE, LANES), lambda i: (i, 0))],
        out_specs=pl.BlockSpec(memory_space=pl.ANY),
        out_shape=jax.ShapeDtypeStruct((n_slots * ROW_TILE, LANES), F32),
        scratch_shapes=[pltpu.VMEM((ROW_TILE, LANES), F32), pltpu.SemaphoreType.DMA(())],
        compiler_params=pltpu.CompilerParams(dimension_semantics=("arbitrary",), has_side_effects=True),
        name="dispatch",
    )(dest_t, pad_slots.reshape(1, -1), h2)


def _experts_kernel(be_ref, first_ref, rows_ref, next_ref, par_ref, xs_ref, bup_ref, bdn_ref, wup_hbm, wdn_hbm,
                    ys_ref, wup_f32, wdn_f32, wup_bf, wdn_bf, sems):
    for j in range(STEP_BLOCKS):
        _expert_block(pl.program_id(0) * STEP_BLOCKS + j, j, be_ref, first_ref, rows_ref, next_ref, par_ref, xs_ref,
                      bup_ref, bdn_ref, wup_hbm, wdn_hbm, ys_ref, wup_f32, wdn_f32, wup_bf, wdn_bf, sems)


def _expert_block(i, j, be_ref, first_ref, rows_ref, next_ref, par_ref, xs_ref, bup_ref, bdn_ref, wup_hbm, wdn_hbm,
                  ys_ref, wup_f32, wdn_f32, wup_bf, wdn_bf, sems):
    cast_rows = 128
    n_rows = rows_ref[i]
    expert = be_ref[i]
    first_token = j * BM

    def weight_copies(expert, par):
        return (pltpu.make_async_copy(wup_hbm.at[expert], wup_f32.at[par], sems.at[0, par]),
                pltpu.make_async_copy(wdn_hbm.at[expert], wdn_f32.at[par], sems.at[1, par]))

    @pl.when(first_ref[i] == 1)
    def _():
        par, nxt = par_ref[i], next_ref[i]

        @pl.when(i == 0)
        def _():
            for cp in weight_copies(expert, par):
                cp.start()

        for cp in weight_copies(expert, par):
            cp.wait()

        @pl.when(nxt >= 0)
        def _():
            for cp in weight_copies(nxt, 1 - par):
                cp.start()

        def cast_up(r, carry):
            rows = pl.ds(pl.multiple_of(r * cast_rows, cast_rows), cast_rows)
            wup_bf[rows, :] = wup_f32[par, rows, :].astype(BF16)
            return carry

        def cast_dn(r, carry):
            rows = pl.ds(pl.multiple_of(r * cast_rows, cast_rows), cast_rows)
            wdn_bf[rows, :] = wdn_f32[par, rows, :].astype(BF16)
            return carry

        lax.fori_loop(0, D_MODEL // cast_rows, cast_up, 0)
        lax.fori_loop(0, D_FF // cast_rows, cast_dn, 0)

    @pl.when(n_rows > 0)
    def _():
        xb = jnp.concatenate([xs_ref[_chunk_of_tokens(c, BM, first_token), :].astype(BF16)
                              for c in range(ROW_TILE)], axis=1)
        b_up = bup_ref[pl.ds(expert, 1), :]
        b_dn = bdn_ref[pl.ds(expert, 1), :]
        up = jnp.dot(xb, wup_bf[...], preferred_element_type=F32) + b_up
        glu = jnp.minimum(up[:, :D_FF], SWIGLU_LIMIT)
        lin = jnp.clip(up[:, D_FF:], -SWIGLU_LIMIT, SWIGLU_LIMIT)
        act = (glu * jax.nn.sigmoid(SWIGLU_ALPHA * glu) * (lin + 1.0)).astype(BF16)
        chunks_per_tile = OUT_TILE // LANES
        for n in range(D_MODEL // OUT_TILE):
            cols = slice(n * OUT_TILE, (n + 1) * OUT_TILE)
            y = jnp.dot(act, wdn_bf[:, cols], preferred_element_type=F32) + b_dn[:, cols]
            for c in range(chunks_per_tile):
                ys_ref[_chunk_of_tokens(n * chunks_per_tile + c, BM, first_token), :] = (
                    y[:, c * LANES:(c + 1) * LANES])

    @pl.when(n_rows == 0)
    def _():
        ys_ref[pl.ds(first_token * ROW_TILE, BM * ROW_TILE), :] = jnp.zeros((BM * ROW_TILE, LANES), F32)


def _experts(xs, blk_expert, blk_first, blk_rows, blk_next, blk_par, w_up, b_up, w_down, b_down):
    n_blocks = xs.shape[0] // (ROW_TILE * BM)
    assert n_blocks % STEP_BLOCKS == 0
    slot_blk = lambda i, *_: (i, 0)
    grid_spec = pltpu.PrefetchScalarGridSpec(
        num_scalar_prefetch=5,
        grid=(n_blocks // STEP_BLOCKS,),
        in_specs=[
            pl.BlockSpec((STEP_BLOCKS * BM * ROW_TILE, LANES), slot_blk),
            _const_spec((N_EXPERTS, 2 * D_FF)),
            _const_spec((N_EXPERTS, D_MODEL)),
            pl.BlockSpec(memory_space=pl.ANY),
            pl.BlockSpec(memory_space=pl.ANY),
        ],
        out_specs=pl.BlockSpec((STEP_BLOCKS * BM * ROW_TILE, LANES), slot_blk),
        scratch_shapes=[pltpu.VMEM((2, D_MODEL, 2 * D_FF), F32), pltpu.VMEM((2, D_FF, D_MODEL), F32),
                        pltpu.VMEM((D_MODEL, 2 * D_FF), BF16), pltpu.VMEM((D_FF, D_MODEL), BF16),
                        pltpu.SemaphoreType.DMA((2, 2))],
    )
    return pl.pallas_call(
        _experts_kernel,
        grid_spec=grid_spec,
        out_shape=jax.ShapeDtypeStruct(xs.shape, F32),
        compiler_params=pltpu.CompilerParams(dimension_semantics=("arbitrary",), vmem_limit_bytes=MOE_VMEM),
        name="experts",
    )(blk_expert, blk_first, blk_rows, blk_next, blk_par, xs, b_up, b_down, w_up, w_down)


def _combine_kernel(dest_ref, dest_1_ref, dest_2_ref, x1_ref, gate_ref, gf_ref, ys_ref, out_ref, buf, sems):
    i = pl.program_id(0)
    slot = lax.rem(i, COMBINE_BUFS)

    def gather(d_ref, s):
        return [_tile_copy(ys_ref, d_ref[k, r], buf.at[s, k], r, sems.at[s])
                for r in range(TB) for k in range(TOP_K)]

    def start(d_ref, s):
        for a, cp in enumerate(gather(d_ref, s)):
            cp.start(priority=a % 2)

    @pl.when(i == 0)
    def _():
        start(dest_ref, 0)
        start(dest_1_ref, 1)

    @pl.when(i + 2 < pl.num_programs(0))
    def _():
        start(dest_2_ref, lax.rem(i + 2, COMBINE_BUFS))

    for cp in gather(dest_ref, slot):
        cp.wait()

    cols = []
    for c in range(ROW_TILE):
        acc = x1_ref[:, c * LANES:(c + 1) * LANES]
        for k in range(TOP_K):
            acc = acc + gate_ref[:, k:k + 1] * buf[slot, k, _chunk_of_tokens(c, TB), :]
        cols.append(acc)
    out_ref[...] = _rmsnorm(jnp.concatenate(cols, axis=1), gf_ref[...])


def _combine(x1, gates, ys, dest_t, gf):
    t = x1.shape[0]
    n_blk = t // TB
    assert n_blk >= COMBINE_BUFS
    dest_ahead = lambda ahead: pl.BlockSpec((TOP_K, TB), lambda i: (0, jnp.minimum(i + ahead, n_blk - 1)),
                                            memory_space=pltpu.SMEM)
    return pl.pallas_call(
        _combine_kernel,
        grid=(n_blk,),
        in_specs=[dest_ahead(0), dest_ahead(1), dest_ahead(2),
                  pl.BlockSpec((TB, D_MODEL), lambda i: (i, 0)),
                  pl.BlockSpec((TB, TOP_K), lambda i: (i, 0)),
                  _const_spec((1, D_MODEL)),
                  pl.BlockSpec(memory_space=pl.ANY)],
        out_specs=pl.BlockSpec((TB, D_MODEL), lambda i: (i, 0)),
        out_shape=jax.ShapeDtypeStruct((t, D_MODEL), F32),
        scratch_shapes=[pltpu.VMEM((COMBINE_BUFS, TOP_K, TB * ROW_TILE, LANES), F32),
                        pltpu.SemaphoreType.DMA((COMBINE_BUFS,))],
        compiler_params=pltpu.CompilerParams(dimension_semantics=("arbitrary",), vmem_limit_bytes=COMBINE_VMEM),
        name="combine",
    )(dest_t, dest_t, dest_t, x1, gates, gf.reshape(1, -1), ys)


def _pick(member, values):
    return jnp.sum(jnp.where(member, values[:, None], 0), axis=0, dtype=jnp.int32)


def _routing(top_idx_t, local_rank_t, blk_counts):
    t = top_idx_t.shape[1]
    n_slots = t * TOP_K + N_EXPERTS * BM
    n_blocks = n_slots // BM
    ids = jnp.arange(N_EXPERTS, dtype=jnp.int32)
    counts = jnp.sum(blk_counts, axis=0)
    padded = (counts + BM - 1) // BM * BM
    pend = jnp.cumsum(padded)
    pstart = pend - padded
    base = pstart[None, :] + jnp.cumsum(blk_counts, axis=0) - blk_counts
    base_tok_t = jnp.repeat(base.T, TM, axis=1)
    dest_t = local_rank_t + jnp.sum(
        jnp.where(top_idx_t[:, None, :] == ids[None, :, None], base_tok_t[None], 0), axis=1)

    blk_start = jnp.arange(n_blocks, dtype=jnp.int32) * BM
    member = (pstart[:, None] <= blk_start[None, :]) & (blk_start[None, :] < pend[:, None])
    blk_expert = _pick(member, ids)
    blk_rows = jnp.sum(jnp.where(member, jnp.clip((pstart + counts)[:, None] - blk_start[None, :], 0, BM), 0),
                       axis=0, dtype=jnp.int32)
    blk_first = _pick(member & (blk_start[None, :] == pstart[:, None]), jnp.ones_like(ids))
    used = (counts > 0).astype(jnp.int32)
    blk_par = _pick(member, (jnp.cumsum(used) - used) % 2)
    later_used = (used > 0)[None, :] & (ids[None, :] > ids[:, None])
    next_used = jnp.min(jnp.where(later_used, ids[None, :], N_EXPERTS), axis=1)
    blk_next = _pick(member, jnp.where(next_used == N_EXPERTS, -1, next_used))

    seg_size = jnp.concatenate([padded - counts, n_slots - pend[-1:]])
    seg_slot = jnp.concatenate([pstart + counts, pend[-1:]])
    seg_end = jnp.cumsum(seg_size)
    seg_begin = seg_end - seg_size
    ordinal = jnp.arange(n_slots - t * TOP_K, dtype=jnp.int32)
    in_seg = (seg_begin[:, None] <= ordinal[None, :]) & (ordinal[None, :] < seg_end[:, None])
    pad_slots = ordinal + _pick(in_seg, seg_slot - seg_begin)
    return (dest_t.astype(jnp.int32), pad_slots, blk_expert, blk_first, blk_rows, blk_next, blk_par, n_slots)


def kernel(x, positions, norm1_g, w_in, b_in, attn_sinks, w_attn_br, pool_w, pool_scale, w_pool_br, w_out,
           norm2_g, w_router, b_router, w_up, b_up, w_down, b_down, normf_g):
    b, s, d = x.shape
    t = b * s
    assert norm1_g.shape[0] == 1, "single-layer problem: the combine kernel applies the final RMSNorm"
    assert d == D_MODEL and s % TM == 0 and t % ROPE_TN == 0 and t % TD == 0 and t % TB == 0
    inv_freq = jnp.power(jnp.float32(ROPE_THETA), -jnp.arange(ROPE_HALF, dtype=F32) * (2.0 / ROPE_DIM))
    cs = _rope_table(positions.reshape(1, t), inv_freq)
    x1, h2, route, blk_counts = _mixer(
        x.reshape(t, d), cs, attn_sinks[0], norm1_g[0], w_in[0], b_in[0], w_attn_br[0], pool_w[0],
        pool_scale[0], w_pool_br[0], w_out[0], norm2_g[0], w_router[0], b_router[0], s)
    top_idx_t = route[0:TOP_K].astype(jnp.int32)
    local_rank_t = route[TOP_K:2 * TOP_K].astype(jnp.int32)
    gates = route[2 * TOP_K:3 * TOP_K].T
    dest_t, pad_slots, blk_expert, blk_first, blk_rows, blk_next, blk_par, n_slots = _routing(
        top_idx_t, local_rank_t, blk_counts[:, :, 0])
    xs = _dispatch(h2, dest_t, pad_slots, n_slots)
    ys = _experts(xs, blk_expert, blk_first, blk_rows, blk_next, blk_par, w_up[0], b_up[0], w_down[0], b_down[0])
    out = _combine(x1, gates, ys, dest_t, normf_g)
    return out.reshape(b, s, d)
```

```python
import functools

import numpy as np
import jax
import jax.numpy as jnp
from jax import lax
from jax.experimental import pallas as pl
from jax.experimental.pallas import tpu as pltpu

F32 = jnp.float32
BF16 = jnp.bfloat16

D_MODEL = 1024
HEAD_DIM = 64
N_Q_HEADS = 16
N_KV_HEADS = 2
GROUP = N_Q_HEADS // N_KV_HEADS
WINDOW = 128
ROPE_THETA = 500000.0
ROPE_DIM = HEAD_DIM // 4
ROPE_HALF = ROPE_DIM // 2
Q_WIDTH = N_Q_HEADS * HEAD_DIM
KV_WIDTH = N_KV_HEADS * HEAD_DIM
NEG_INF = -1e30
LOG2_E = 1.4426950408889634
POOL_WINDOWS = (2, 4, 8, 16)
POOL_WIDTH = D_MODEL // 2
POOL_GROUP_WIDTH = POOL_WIDTH // len(POOL_WINDOWS)
POOL_HALO = 16
IN_WIDTH = Q_WIDTH + 2 * KV_WIDTH + POOL_WIDTH + 2 * D_MODEL
O_K = Q_WIDTH
O_V = O_K + KV_WIDTH
O_P = O_V + KV_WIDTH
O_GA = O_P + POOL_WIDTH
O_GP = O_GA + D_MODEL
N_EXPERTS = 32
TOP_K = 4
D_FF = D_MODEL
SWIGLU_ALPHA = 1.702
SWIGLU_LIMIT = 7.0
RMS_EPS = 1e-5

LANES = 128
ROW_TILE = D_MODEL // LANES
ROUTE_ROWS = 16
TM = 512
TD = 512
TB = 256
BM = 256
STEP_BLOCKS = 4
OUT_TILE = 256
ROPE_TN = 2048
V7X_VMEM_BYTES = 64 * 1024 * 1024
MIX_VMEM = V7X_VMEM_BYTES - 8 * 1024 * 1024
MOE_VMEM = V7X_VMEM_BYTES - 6 * 1024 * 1024
COMBINE_VMEM = V7X_VMEM_BYTES // 2
NT_DIMS = (((1,), (1,)), ((), ()))
TN_DIMS = (((0,), (0,)), ((), ()))


def _const_spec(shape):
    return pl.BlockSpec(shape, lambda *_: (0,) * len(shape))


def _chunk_of_tokens(c, n_tokens, first_token=0):
    return pl.ds(first_token * ROW_TILE + c, n_tokens, stride=ROW_TILE)


def _tile_of_token(token):
    if isinstance(token, int):
        return pl.ds(token * ROW_TILE, ROW_TILE)
    return pl.ds(pl.multiple_of(token * ROW_TILE, ROW_TILE), ROW_TILE)


def _rope_table_kernel(pos_ref, invf_ref, out_ref):
    ang = invf_ref[...] * pos_ref[...].astype(F32)
    out_ref[0:ROPE_HALF, :] = jnp.cos(ang)
    out_ref[ROPE_HALF:, :] = jnp.sin(ang)


def _rope_table(pos_row, inv_freq):
    t = pos_row.shape[1]
    return pl.pallas_call(
        _rope_table_kernel,
        grid=(t // ROPE_TN,),
        in_specs=[pl.BlockSpec((1, ROPE_TN), lambda i: (0, i)), _const_spec((ROPE_HALF, 1))],
        out_specs=pl.BlockSpec((2 * ROPE_HALF, ROPE_TN), lambda i: (0, i)),
        out_shape=jax.ShapeDtypeStruct((2 * ROPE_HALF, t), F32),
        name="rope_table",
    )(pos_row, inv_freq.reshape(ROPE_HALF, 1))


def _rope_expand_matrix():
    e = np.zeros((2 * ROPE_HALF, 2 * LANES), np.float32)
    ones = np.zeros((1, LANES), np.float32)
    for l in range(LANES):
        hl = l % HEAD_DIM
        if hl < ROPE_DIM:
            j = hl % ROPE_HALF
            e[j, l] = 1.0
            e[ROPE_HALF + j, LANES + l] = -1.0 if hl < ROPE_HALF else 1.0
        else:
            ones[0, l] = 1.0
    return e, ones


def _rmsnorm(x, g):
    ms = jnp.mean(x * x, axis=-1, keepdims=True)
    return x * lax.rsqrt(ms + RMS_EPS) * g


def _split3(x):
    a = x.astype(BF16)
    r = x - a.astype(F32)
    b = r.astype(BF16)
    c = (r - b.astype(F32)).astype(BF16)
    return a, b, c


def _mixer_kernel(sinks_ref, x_ref, cs_ref, erope_ref, ones_ref, g1_ref, win_ref, bin_ref, wab_ref,
                  pw_ref, ps_ref, wpb_ref, wout_ref, g2_ref, wr2_ref, brc_ref,
                  x1_ref, h2_ref, route_ref, cnt_ref,
                  kprev, vprev, uprev, q_scr, attn_scr, ga_scr, gp_scr, yat_scr, *, blocks_per_seq):
    i = pl.program_id(0)
    blk_in_seq = lax.rem(i, blocks_per_seq)
    seq_start = blk_in_seq == 0

    @pl.when(seq_start)
    def _():
        kprev[...] = jnp.zeros_like(kprev)
        vprev[...] = jnp.zeros_like(vprev)
        uprev[...] = jnp.zeros_like(uprev)

    x = x_ref[...]
    h = _rmsnorm(x, g1_ref[...]).astype(BF16)

    c1, c2, c3 = _split3(cs_ref[...])
    er = erope_ref[...]
    tab = (lax.dot_general(c1, er, TN_DIMS, preferred_element_type=F32)
           + lax.dot_general(c2, er, TN_DIMS, preferred_element_type=F32)
           + lax.dot_general(c3, er, TN_DIMS, preferred_element_type=F32))
    cos_t = tab[:, :LANES] + ones_ref[...]
    sin_t = tab[:, LANES:]
    lane = lax.broadcasted_iota(jnp.int32, (TM, LANES), 1)
    first_half = (lane & (HEAD_DIM - 1)) < ROPE_HALF

    def rope(t):
        partner = jnp.where(first_half, pltpu.roll(t, LANES - ROPE_HALF, 1), pltpu.roll(t, ROPE_HALF, 1))
        return t * cos_t + partner * sin_t

    q = jnp.dot(h, win_ref[:, 0:Q_WIDTH], preferred_element_type=F32) + bin_ref[:, 0:Q_WIDTH]
    for c in range(Q_WIDTH // LANES):
        qc = rope(q[:, c * LANES:(c + 1) * LANES]) * (HEAD_DIM ** -0.5 * LOG2_E)
        q_scr[:, c * LANES:(c + 1) * LANES] = qc.astype(BF16)

    kvu = jnp.dot(h, win_ref[:, O_K:O_GA], preferred_element_type=F32) + bin_ref[:, O_K:O_GA]
    k = rope(kvu[:, 0:KV_WIDTH])
    v = kvu[:, KV_WIDTH:2 * KV_WIDTH]
    u = kvu[:, 2 * KV_WIDTH:]

    lane_b = lax.broadcasted_iota(jnp.int32, (TM + WINDOW, LANES), 1)
    low = lane_b < HEAD_DIM

    def head_bands(prev_ref, cur):
        band = jnp.concatenate([prev_ref[...], cur], axis=0)
        swapped = pltpu.roll(band, HEAD_DIM, 1)
        zero = jnp.zeros_like(band)
        a0 = jnp.where(low, band, zero).astype(BF16)
        b0 = jnp.where(low, zero, swapped).astype(BF16)
        a1 = jnp.where(low, swapped, zero).astype(BF16)
        b1 = jnp.where(low, zero, band).astype(BF16)
        return ((a0, b0), (a1, b1))

    kb = head_bands(kprev, k)
    vb = head_bands(vprev, v)
    kprev[...] = k[TM - WINDOW:, :]
    vprev[...] = v[TM - WINDOW:, :]

    qi = lax.broadcasted_iota(jnp.int32, (WINDOW, 2 * WINDOW), 0)
    kj = lax.broadcasted_iota(jnp.int32, (WINDOW, 2 * WINDOW), 1)
    mask_mid = jnp.logical_or(jnp.logical_and(kj < WINDOW, kj > qi),
                              jnp.logical_and(kj >= WINDOW, (kj - WINDOW) <= qi))
    first_key = jnp.where(seq_start, WINDOW, 0)
    mask_first = jnp.logical_and(mask_mid, kj >= first_key)

    low_w = lax.broadcasted_iota(jnp.int32, (WINDOW, LANES), 1) < HEAD_DIM
    n_chunks = GROUP // 2
    ones_rows = lax.broadcasted_iota(jnp.int32, (4 * WINDOW, LANES), 0)
    ones_lanes = lax.broadcasted_iota(jnp.int32, (4 * WINDOW, LANES), 1)
    sum_cols = jnp.where((ones_rows < 2 * WINDOW) == (ones_lanes < HEAD_DIM), 1.0, 0.0).astype(BF16)

    pairs = [(j, hkv) for j in range(TM // WINDOW) for hkv in range(N_KV_HEADS)]

    def scores(j, hkv):
        rows = slice(j * WINDOW, (j + 1) * WINDOW)
        band_rows = slice(j * WINDOW, j * WINDOW + 2 * WINDOW)
        q4 = jnp.concatenate(
            [q_scr[rows, (hkv * n_chunks + c) * LANES:(hkv * n_chunks + c + 1) * LANES]
             for c in range(n_chunks)], axis=0)
        kk = jnp.concatenate([kb[hkv][0][band_rows], kb[hkv][1][band_rows]], axis=0)
        return lax.dot_general(q4, kk, NT_DIMS, preferred_element_type=F32)

    def attend(j, hkv, s):
        mask = mask_first if j == 0 else mask_mid
        rows = slice(j * WINDOW, (j + 1) * WINDOW)
        band_rows = slice(j * WINDOW, j * WINDOW + 2 * WINDOW)
        vv = jnp.concatenate([vb[hkv][0][band_rows], vb[hkv][1][band_rows]], axis=0)
        p_rows, sink_terms = [], []
        for c in range(n_chunks):
            p_cols, st = [], []
            for par in range(2):
                sink = sinks_ref[hkv * GROUP + 2 * c + par] * LOG2_E
                sc = s[c * WINDOW:(c + 1) * WINDOW, par * 2 * WINDOW:(par + 1) * 2 * WINDOW]
                sc = jnp.where(mask, sc, NEG_INF)
                m = jnp.maximum(jnp.max(sc, axis=-1, keepdims=True), sink)
                p_cols.append(jnp.exp2(sc - m).astype(BF16))
                st.append(jnp.exp2(sink - m))
            p_rows.append(jnp.concatenate(p_cols, axis=1))
            sink_terms.append(st)
        p = jnp.concatenate(p_rows, axis=0)
        o2 = jnp.dot(p, jnp.concatenate([vv, sum_cols], axis=1), preferred_element_type=F32)
        for c in range(n_chunks):
            den = o2[c * WINDOW:(c + 1) * WINDOW, LANES:] + jnp.where(low_w, sink_terms[c][0], sink_terms[c][1])
            oc = o2[c * WINDOW:(c + 1) * WINDOW, :LANES] / den
            col = (hkv * n_chunks + c) * LANES
            attn_scr[rows, col:col + LANES] = oc.astype(BF16)

    half = D_MODEL // 2

    def gate(scr, offset, part):
        cols = slice(offset + part * half, offset + (part + 1) * half)
        scr[:, part * half:(part + 1) * half] = jax.nn.sigmoid(
            jnp.dot(h, win_ref[:, cols], preferred_element_type=F32) + bin_ref[:, cols])

    def attn_branch(j):
        rows = slice(j * WINDOW, (j + 1) * WINDOW)
        yat_scr[rows, :] = jnp.dot(attn_scr[rows, :], wab_ref[...], preferred_element_type=F32)

    def pooling():
        ext = jnp.concatenate([uprev[...], u], axis=0)
        uprev[...] = u[TM - POOL_HALO:, :]
        pos_in_seq = blk_in_seq * TM + lax.broadcasted_iota(jnp.int32, (TM, 1), 0)
        mixed = []
        for g, w in enumerate(POOL_WINDOWS):
            cols = slice(g * POOL_GROUP_WIDTH, (g + 1) * POOL_GROUP_WIDTH)
            sg = ext[:, cols]
            step = 1
            while step < w:
                sg = sg + pltpu.roll(sg, step, 0)
                step *= 2
            count = jnp.minimum(pos_in_seq + 1, w).astype(F32)
            pooled = sg[POOL_HALO:, :] / count - u[:, cols]
            mg = jnp.dot(pooled.astype(BF16), pw_ref[g], preferred_element_type=F32)
            mixed.append((mg * ps_ref[:, cols]).astype(BF16))
        return jnp.dot(jnp.concatenate(mixed, axis=1), wpb_ref[...], preferred_element_type=F32)

    fillers = [functools.partial(gate, ga_scr, O_GA, 0), functools.partial(gate, ga_scr, O_GA, 1),
               functools.partial(gate, gp_scr, O_GP, 0), functools.partial(gate, gp_scr, O_GP, 1), pooling]
    n_row_blocks = TM // WINDOW
    branch_done = 0
    y_pool = None
    s_next = scores(*pairs[0])
    for n, (j, hkv) in enumerate(pairs):
        s_cur = s_next
        if n + 1 < len(pairs):
            s_next = scores(*pairs[n + 1])
        if n < len(fillers):
            out = fillers[n]()
            y_pool = out if out is not None else y_pool
        else:
            while branch_done < n_row_blocks and N_KV_HEADS * (branch_done + 1) <= n:
                attn_branch(branch_done)
                branch_done += 1
        attend(j, hkv, s_cur)
    for jb in range(branch_done, n_row_blocks):
        attn_branch(jb)

    hm = TM // 2
    e_iota = lax.broadcasted_iota(jnp.int32, (N_EXPERTS, hm), 0)

    def out_proj(part):
        rows = slice(part * hm, (part + 1) * hm)
        merged = (ga_scr[rows, :] * yat_scr[rows, :] + gp_scr[rows, :] * y_pool[rows, :]).astype(BF16)
        x1 = x[rows, :] + jnp.dot(merged, wout_ref[...], preferred_element_type=F32)
        x1_ref[rows, :] = x1
        return x1

    def route(part, x1):
        h2 = _rmsnorm(x1, g2_ref[...])
        for c in range(ROW_TILE):
            h2_ref[_chunk_of_tokens(c, hm, part * hm), :] = h2[:, c * LANES:(c + 1) * LANES]
        ha = h2.astype(BF16)
        hb = (h2 - ha.astype(F32)).astype(BF16)
        both = lax.dot_general(wr2_ref[...], ha, NT_DIMS, preferred_element_type=F32)
        logits = (both[:N_EXPERTS] + both[N_EXPERTS:]
                  + lax.dot_general(wr2_ref[0:N_EXPERTS, :], hb, NT_DIMS, preferred_element_type=F32)
                  + brc_ref[...])
        vals, idx, hot = [], [], []
        work = logits
        for _ in range(TOP_K):
            m = jnp.max(work, axis=0, keepdims=True)
            ix = jnp.min(jnp.where(work == m, e_iota, N_EXPERTS), axis=0, keepdims=True)
            is_hot = e_iota == ix
            vals.append(m)
            idx.append(ix)
            hot.append(jnp.where(is_hot, 1.0, 0.0))
            work = jnp.where(is_hot, -jnp.inf, work)
        ex = [jnp.exp(vk - vals[0]) for vk in vals]
        den = ex[0] + ex[1] + ex[2] + ex[3]
        return idx, [e / den for e in ex], hot

    x1_parts = [out_proj(0), out_proj(1)]
    routed = [route(part, x1_parts[part]) for part in range(2)]
    idxs = [jnp.concatenate([routed[0][0][k], routed[1][0][k]], axis=1) for k in range(TOP_K)]
    gates = [jnp.concatenate([routed[0][1][k], routed[1][1][k]], axis=1) for k in range(TOP_K)]
    hots = [jnp.concatenate([routed[0][2][k], routed[1][2][k]], axis=1) for k in range(TOP_K)]

    picked = (hots[0] + hots[1] + hots[2] + hots[3]).astype(BF16)
    t_row = lax.broadcasted_iota(jnp.int32, (TM, TM), 0)
    t_col = lax.broadcasted_iota(jnp.int32, (TM, TM), 1)
    earlier = jnp.where(t_row < t_col, 1.0, 0.0).astype(BF16)
    before = jnp.dot(picked, earlier, preferred_element_type=F32)
    cnt_ref[0] = jnp.dot(picked, jnp.ones((TM, LANES), BF16), preferred_element_type=F32).astype(jnp.int32)

    for kk_ in range(TOP_K):
        local_rank = jnp.sum(hots[kk_] * before, axis=0, keepdims=True)
        route_ref[kk_:kk_ + 1, :] = idxs[kk_].astype(F32)
        route_ref[TOP_K + kk_:TOP_K + kk_ + 1, :] = local_rank
        route_ref[2 * TOP_K + kk_:2 * TOP_K + kk_ + 1, :] = gates[kk_]
    route_ref[3 * TOP_K:, :] = jnp.zeros((ROUTE_ROWS - 3 * TOP_K, TM), F32)


def _mixer(x2d, cs, sinks, g1, w_in, b_in, w_ab, pool_w, pool_scale, w_pb, w_out, g2, w_r, b_r, seq_len):
    t = x2d.shape[0]
    erope, ones = _rope_expand_matrix()
    w_rh = w_r.astype(BF16)
    w_rl = (w_r - w_rh.astype(F32)).astype(BF16)
    w_r2 = jnp.concatenate([w_rh.T, w_rl.T], axis=0)
    row = lambda i: (i, 0)
    in_specs = [
        pl.BlockSpec(memory_space=pltpu.SMEM),
        pl.BlockSpec((TM, D_MODEL), row),
        pl.BlockSpec((2 * ROPE_HALF, TM), lambda i: (0, i)),
        _const_spec((2 * ROPE_HALF, 2 * LANES)),
        _const_spec((1, LANES)),
        _const_spec((1, D_MODEL)),
        _const_spec((D_MODEL, IN_WIDTH)),
        _const_spec((1, IN_WIDTH)),
        _const_spec((Q_WIDTH, D_MODEL)),
        _const_spec((len(POOL_WINDOWS), POOL_GROUP_WIDTH, POOL_GROUP_WIDTH)),
        _const_spec((1, POOL_WIDTH)),
        _const_spec((POOL_WIDTH, D_MODEL)),
        _const_spec((D_MODEL, D_MODEL)),
        _const_spec((1, D_MODEL)),
        _const_spec((2 * N_EXPERTS, D_MODEL)),
        _const_spec((N_EXPERTS, 1)),
    ]
    out_specs = [pl.BlockSpec((TM, D_MODEL), row), pl.BlockSpec((TM * ROW_TILE, LANES), row),
                 pl.BlockSpec((ROUTE_ROWS, TM), lambda i: (0, i)),
                 pl.BlockSpec((1, N_EXPERTS, LANES), lambda i: (i, 0, 0))]
    out_shape = [jax.ShapeDtypeStruct((t, D_MODEL), F32), jax.ShapeDtypeStruct((t * ROW_TILE, LANES), F32),
                 jax.ShapeDtypeStruct((ROUTE_ROWS, t), F32),
                 jax.ShapeDtypeStruct((t // TM, N_EXPERTS, LANES), jnp.int32)]
    scratch = [pltpu.VMEM((WINDOW, KV_WIDTH), F32), pltpu.VMEM((WINDOW, KV_WIDTH), F32),
               pltpu.VMEM((POOL_HALO, POOL_WIDTH), F32),
               pltpu.VMEM((TM, Q_WIDTH), BF16), pltpu.VMEM((TM, Q_WIDTH), BF16),
               pltpu.VMEM((TM, D_MODEL), F32), pltpu.VMEM((TM, D_MODEL), F32), pltpu.VMEM((TM, D_MODEL), F32)]
    return pl.pallas_call(
        functools.partial(_mixer_kernel, blocks_per_seq=seq_len // TM),
        grid=(t // TM,),
        in_specs=in_specs, out_specs=out_specs, out_shape=out_shape, scratch_shapes=scratch,
        compiler_params=pltpu.CompilerParams(dimension_semantics=("arbitrary",), vmem_limit_bytes=MIX_VMEM),
        name="mixer",
    )(sinks, x2d, cs, jnp.asarray(erope, BF16), jnp.asarray(ones), g1.reshape(1, -1), w_in.astype(BF16),
      b_in.reshape(1, -1), w_ab.astype(BF16), pool_w.astype(BF16), pool_scale.reshape(1, -1),
      w_pb.astype(BF16), w_out.astype(BF16), g2.reshape(1, -1), w_r2, b_r.reshape(-1, 1))


def _tile_copy(src, src_token, dst, dst_token, sem):
    return pltpu.make_async_copy(src.at[_tile_of_token(src_token)], dst.at[_tile_of_token(dst_token)], sem)


def _dispatch_kernel(dest_ref, pad_ref, h2_ref, xs_ref, zeros, sem, pad_sem, *, runs_per_step):
    i = pl.program_id(0)

    @pl.when(i == 0)
    def _():
        zeros[...] = jnp.zeros_like(zeros)

    def zero_fill(first_slot, n_slots):
        return pltpu.make_async_copy(zeros.at[pl.ds(0, n_slots * ROW_TILE)],
                                     xs_ref.at[pl.ds(pl.multiple_of(first_slot * ROW_TILE, ROW_TILE),
                                                     n_slots * ROW_TILE)], pad_sem)

    pad_copies = []
    for j in range(runs_per_step):
        run = i * runs_per_step + j
        run_c = jnp.minimum(run, N_EXPERTS - 1)
        first, length = pad_ref[0, run_c], pad_ref[1, run_c]
        piece = BM // 2
        while piece >= 1:
            wanted = jnp.logical_and(run < N_EXPERTS, (length & piece) != 0)
            pad_copies.append((wanted, zero_fill(first, piece)))
            first = first + (length & piece)
            piece //= 2
        pad_copies.append((run < pad_ref[2, 1], zero_fill(pad_ref[2, 0] + run * BM, BM)))

    copies = [_tile_copy(h2_ref, r, xs_ref, dest_ref[k, r], sem) for r in range(TD) for k in range(TOP_K)]
    for a, cp in enumerate(copies):
        cp.start(priority=a % 2)
    for wanted, cp in pad_copies:
        pl.when(wanted)(cp.start)
    for cp in copies:
        cp.wait()
    for wanted, cp in pad_copies:
        pl.when(wanted)(cp.wait)


def _dispatch(h2, dest_t, pad_info, n_slots):
    t = h2.shape[0] // ROW_TILE
    n_steps = t // TD
    runs_per_step = -(-N_EXPERTS // n_steps)
    return pl.pallas_call(
        functools.partial(_dispatch_kernel, runs_per_step=runs_per_step),
        grid=(n_steps,),
        in_specs=[pl.BlockSpec((TOP_K, TD), lambda i: (0, i), memory_space=pltpu.SMEM),
                  pl.BlockSpec(memory_space=pltpu.SMEM),
                  pl.BlockSpec((TD * ROW_TILE, LANES), lambda i: (i, 0))],
        out_specs=pl.BlockSpec(memory_space=pl.ANY),
        out_shape=jax.ShapeDtypeStruct((n_slots * ROW_TILE, LANES), F32),
        scratch_shapes=[pltpu.VMEM((BM * ROW_TILE, LANES), F32), pltpu.SemaphoreType.DMA(()),
                        pltpu.SemaphoreType.DMA(())],
        compiler_params=pltpu.CompilerParams(dimension_semantics=("arbitrary",), has_side_effects=True),
        name="dispatch",
    )(dest_t, pad_info, h2)


def _experts_kernel(be_ref, first_ref, rows_ref, next_ref, par_ref, xs_ref, bup_ref, bdn_ref, wup_hbm, wdn_hbm,
                    ys_ref, wup_f32, wdn_f32, wup_bf, wdn_bf, sems):
    for j in range(STEP_BLOCKS):
        _expert_block(pl.program_id(0) * STEP_BLOCKS + j, j, be_ref, first_ref, rows_ref, next_ref, par_ref, xs_ref,
                      bup_ref, bdn_ref, wup_hbm, wdn_hbm, ys_ref, wup_f32, wdn_f32, wup_bf, wdn_bf, sems)


def _expert_block(i, j, be_ref, first_ref, rows_ref, next_ref, par_ref, xs_ref, bup_ref, bdn_ref, wup_hbm, wdn_hbm,
                  ys_ref, wup_f32, wdn_f32, wup_bf, wdn_bf, sems):
    cast_rows = 128
    n_rows = rows_ref[i]
    expert = be_ref[i]
    first_token = j * BM

    def weight_copies(expert, par):
        return (pltpu.make_async_copy(wup_hbm.at[expert], wup_f32.at[par], sems.at[0, par]),
                pltpu.make_async_copy(wdn_hbm.at[expert], wdn_f32.at[par], sems.at[1, par]))

    @pl.when(first_ref[i] == 1)
    def _():
        par, nxt = par_ref[i], next_ref[i]

        @pl.when(i == 0)
        def _():
            for cp in weight_copies(expert, par):
                cp.start()

        for cp in weight_copies(expert, par):
            cp.wait()

        @pl.when(nxt >= 0)
        def _():
            for cp in weight_copies(nxt, 1 - par):
                cp.start()

        def cast_up(r, carry):
            rows = pl.ds(pl.multiple_of(r * cast_rows, cast_rows), cast_rows)
            wup_bf[rows, :] = wup_f32[par, rows, :].astype(BF16)
            return carry

        def cast_dn(r, carry):
            rows = pl.ds(pl.multiple_of(r * cast_rows, cast_rows), cast_rows)
            wdn_bf[rows, :] = wdn_f32[par, rows, :].astype(BF16)
            return carry

        lax.fori_loop(0, D_MODEL // cast_rows, cast_up, 0)
        lax.fori_loop(0, D_FF // cast_rows, cast_dn, 0)

    @pl.when(n_rows > 0)
    def _():
        xb = jnp.concatenate([xs_ref[_chunk_of_tokens(c, BM, first_token), :].astype(BF16)
                              for c in range(ROW_TILE)], axis=1)
        b_up = bup_ref[pl.ds(expert, 1), :]
        b_dn = bdn_ref[pl.ds(expert, 1), :]
        up = jnp.dot(xb, wup_bf[...], preferred_element_type=F32) + b_up
        glu = jnp.minimum(up[:, :D_FF], SWIGLU_LIMIT)
        lin = jnp.clip(up[:, D_FF:], -SWIGLU_LIMIT, SWIGLU_LIMIT)
        act = (glu * jax.nn.sigmoid(SWIGLU_ALPHA * glu) * (lin + 1.0)).astype(BF16)
        chunks_per_tile = OUT_TILE // LANES
        for n in range(D_MODEL // OUT_TILE):
            cols = slice(n * OUT_TILE, (n + 1) * OUT_TILE)
            y = jnp.dot(act, wdn_bf[:, cols], preferred_element_type=F32) + b_dn[:, cols]
            for c in range(chunks_per_tile):
                ys_ref[_chunk_of_tokens(n * chunks_per_tile + c, BM, first_token), :] = (
                    y[:, c * LANES:(c + 1) * LANES])

    @pl.when(n_rows == 0)
    def _():
        ys_ref[pl.ds(first_token * ROW_TILE, BM * ROW_TILE), :] = jnp.zeros((BM * ROW_TILE, LANES), F32)


def _experts(xs, blk_expert, blk_first, blk_rows, blk_next, blk_par, w_up, b_up, w_down, b_down):
    n_blocks = xs.shape[0] // (ROW_TILE * BM)
    assert n_blocks % STEP_BLOCKS == 0
    slot_blk = lambda i, *_: (i, 0)
    grid_spec = pltpu.PrefetchScalarGridSpec(
        num_scalar_prefetch=5,
        grid=(n_blocks // STEP_BLOCKS,),
        in_specs=[
            pl.BlockSpec((STEP_BLOCKS * BM * ROW_TILE, LANES), slot_blk),
            _const_spec((N_EXPERTS, 2 * D_FF)),
            _const_spec((N_EXPERTS, D_MODEL)),
            pl.BlockSpec(memory_space=pl.ANY),
            pl.BlockSpec(memory_space=pl.ANY),
        ],
        out_specs=pl.BlockSpec((STEP_BLOCKS * BM * ROW_TILE, LANES), slot_blk),
        scratch_shapes=[pltpu.VMEM((2, D_MODEL, 2 * D_FF), F32), pltpu.VMEM((2, D_FF, D_MODEL), F32),
                        pltpu.VMEM((D_MODEL, 2 * D_FF), BF16), pltpu.VMEM((D_FF, D_MODEL), BF16),
                        pltpu.SemaphoreType.DMA((2, 2))],
    )
    return pl.pallas_call(
        _experts_kernel,
        grid_spec=grid_spec,
        out_shape=jax.ShapeDtypeStruct(xs.shape, F32),
        compiler_params=pltpu.CompilerParams(dimension_semantics=("arbitrary",), vmem_limit_bytes=MOE_VMEM),
        name="experts",
    )(blk_expert, blk_first, blk_rows, blk_next, blk_par, xs, b_up, b_down, w_up, w_down)


def _combine_kernel(dest_ref, dest_next_ref, x1_ref, gate_ref, gf_ref, ys_ref, out_ref, buf, sems):
    i = pl.program_id(0)
    slot = lax.rem(i, 2)

    def gather(d_ref, s):
        return [_tile_copy(ys_ref, d_ref[k, r], buf.at[s, k], r, sems.at[s])
                for r in range(TB) for k in range(TOP_K)]

    def start(d_ref, s):
        for a, cp in enumerate(gather(d_ref, s)):
            cp.start(priority=a % 2)

    @pl.when(i == 0)
    def _():
        start(dest_ref, slot)

    @pl.when(i + 1 < pl.num_programs(0))
    def _():
        start(dest_next_ref, 1 - slot)

    for cp in gather(dest_ref, slot):
        cp.wait()

    cols = []
    for c in range(ROW_TILE):
        acc = x1_ref[:, c * LANES:(c + 1) * LANES]
        for k in range(TOP_K):
            acc = acc + gate_ref[:, k:k + 1] * buf[slot, k, _chunk_of_tokens(c, TB), :]
        cols.append(acc)
    out_ref[...] = _rmsnorm(jnp.concatenate(cols, axis=1), gf_ref[...])


def _combine(x1, gates, ys, dest_t, gf):
    t = x1.shape[0]
    n_blk = t // TB
    dest_ahead = lambda ahead: pl.BlockSpec((TOP_K, TB), lambda i: (0, jnp.minimum(i + ahead, n_blk - 1)),
                                            memory_space=pltpu.SMEM)
    return pl.pallas_call(
        _combine_kernel,
        grid=(n_blk,),
        in_specs=[dest_ahead(0), dest_ahead(1),
                  pl.BlockSpec((TB, D_MODEL), lambda i: (i, 0)),
                  pl.BlockSpec((TB, TOP_K), lambda i: (i, 0)),
                  _const_spec((1, D_MODEL)),
                  pl.BlockSpec(memory_space=pl.ANY)],
        out_specs=pl.BlockSpec((TB, D_MODEL), lambda i: (i, 0)),
        out_shape=jax.ShapeDtypeStruct((t, D_MODEL), F32),
        scratch_shapes=[pltpu.VMEM((2, TOP_K, TB * ROW_TILE, LANES), F32), pltpu.SemaphoreType.DMA((2,))],
        compiler_params=pltpu.CompilerParams(dimension_semantics=("arbitrary",), vmem_limit_bytes=COMBINE_VMEM),
        name="combine",
    )(dest_t, dest_t, x1, gates, gf.reshape(1, -1), ys)


def _pick(member, values):
    return jnp.sum(jnp.where(member, values[:, None], 0), axis=0, dtype=jnp.int32)


def _routing(top_idx_t, local_rank_t, blk_counts):
    t = top_idx_t.shape[1]
    n_slots = t * TOP_K + N_EXPERTS * BM
    n_blocks = n_slots // BM
    ids = jnp.arange(N_EXPERTS, dtype=jnp.int32)
    counts = jnp.sum(blk_counts, axis=0)
    padded = (counts + BM - 1) // BM * BM
    pend = jnp.cumsum(padded)
    pstart = pend - padded
    base = pstart[None, :] + jnp.cumsum(blk_counts, axis=0) - blk_counts
    base_tok_t = jnp.repeat(base.T, TM, axis=1)
    dest_t = local_rank_t + jnp.sum(
        jnp.where(top_idx_t[:, None, :] == ids[None, :, None], base_tok_t[None], 0), axis=1)

    blk_start = jnp.arange(n_blocks, dtype=jnp.int32) * BM
    member = (pstart[:, None] <= blk_start[None, :]) & (blk_start[None, :] < pend[:, None])
    blk_expert = _pick(member, ids)
    blk_rows = jnp.sum(jnp.where(member, jnp.clip((pstart + counts)[:, None] - blk_start[None, :], 0, BM), 0),
                       axis=0, dtype=jnp.int32)
    blk_first = _pick(member & (blk_start[None, :] == pstart[:, None]), jnp.ones_like(ids))
    used = (counts > 0).astype(jnp.int32)
    blk_par = _pick(member, (jnp.cumsum(used) - used) % 2)
    later_used = (used > 0)[None, :] & (ids[None, :] > ids[:, None])
    next_used = jnp.min(jnp.where(later_used, ids[None, :], N_EXPERTS), axis=1)
    blk_next = _pick(member, jnp.where(next_used == N_EXPERTS, -1, next_used))

    tail = jnp.zeros((N_EXPERTS,), jnp.int32).at[0].set(pend[-1]).at[1].set((n_slots - pend[-1]) // BM)
    pad_info = jnp.stack([pstart + counts, padded - counts, tail]).astype(jnp.int32)
    return (dest_t.astype(jnp.int32), pad_info, blk_expert, blk_first, blk_rows, blk_next, blk_par, n_slots)


def kernel(x, positions, norm1_g, w_in, b_in, attn_sinks, w_attn_br, pool_w, pool_scale, w_pool_br, w_out,
           norm2_g, w_router, b_router, w_up, b_up, w_down, b_down, normf_g):
    b, s, d = x.shape
    t = b * s
    assert norm1_g.shape[0] == 1, "single-layer problem: the combine kernel applies the final RMSNorm"
    assert d == D_MODEL and s % TM == 0 and t % ROPE_TN == 0 and t % TD == 0 and t % TB == 0
    inv_freq = jnp.power(jnp.float32(ROPE_THETA), -jnp.arange(ROPE_HALF, dtype=F32) * (2.0 / ROPE_DIM))
    cs = _rope_table(positions.reshape(1, t), inv_freq)
    x1, h2, route, blk_counts = _mixer(
        x.reshape(t, d), cs, attn_sinks[0], norm1_g[0], w_in[0], b_in[0], w_attn_br[0], pool_w[0],
        pool_scale[0], w_pool_br[0], w_out[0], norm2_g[0], w_router[0], b_router[0], s)
    top_idx_t = route[0:TOP_K].astype(jnp.int32)
    local_rank_t = route[TOP_K:2 * TOP_K].astype(jnp.int32)
    gates = route[2 * TOP_K:3 * TOP_K].T
    dest_t, pad_info, blk_expert, blk_first, blk_rows, blk_next, blk_par, n_slots = _routing(
        top_idx_t, local_rank_t, blk_counts[:, :, 0])
    xs = _dispatch(h2, dest_t, pad_info, n_slots)
    ys = _experts(xs, blk_expert, blk_first, blk_rows, blk_next, blk_par, w_up[0], b_up[0], w_down[0], b_down[0])
    out = _combine(x1, gates, ys, dest_t, normf_g)
    return out.reshape(b, s, d)
```

```python
import functools

import numpy as np
import jax
import jax.numpy as jnp
from jax import lax
from jax.experimental import pallas as pl
from jax.experimental.pallas import tpu as pltpu

F32 = jnp.float32
BF16 = jnp.bfloat16

D_MODEL = 1024
HEAD_DIM = 64
N_Q_HEADS = 16
N_KV_HEADS = 2
GROUP = N_Q_HEADS // N_KV_HEADS
WINDOW = 128
ROPE_THETA = 500000.0
ROPE_DIM = HEAD_DIM // 4
ROPE_HALF = ROPE_DIM // 2
Q_WIDTH = N_Q_HEADS * HEAD_DIM
KV_WIDTH = N_KV_HEADS * HEAD_DIM
NEG_INF = -1e30
LOG2_E = 1.4426950408889634
POOL_WINDOWS = (2, 4, 8, 16)
POOL_WIDTH = D_MODEL // 2
POOL_GROUP_WIDTH = POOL_WIDTH // len(POOL_WINDOWS)
POOL_HALO = 16
IN_WIDTH = Q_WIDTH + 2 * KV_WIDTH + POOL_WIDTH + 2 * D_MODEL
O_K = Q_WIDTH
O_V = O_K + KV_WIDTH
O_P = O_V + KV_WIDTH
O_GA = O_P + POOL_WIDTH
O_GP = O_GA + D_MODEL
N_EXPERTS = 32
TOP_K = 4
D_FF = D_MODEL
SWIGLU_ALPHA = 1.702
SWIGLU_LIMIT = 7.0
RMS_EPS = 1e-5

LANES = 128
ROW_TILE = D_MODEL // LANES
ROUTE_ROWS = 16
TM = 512
TD = 512
TB = 256
BM = 256
STEP_BLOCKS = 4
OUT_TILE = 256
ROPE_TN = 2048
V7X_VMEM_BYTES = 64 * 1024 * 1024
MIX_VMEM = V7X_VMEM_BYTES - 8 * 1024 * 1024
MOE_VMEM = V7X_VMEM_BYTES - 6 * 1024 * 1024
COMBINE_VMEM = V7X_VMEM_BYTES // 2
NT_DIMS = (((1,), (1,)), ((), ()))
TN_DIMS = (((0,), (0,)), ((), ()))


def _const_spec(shape):
    return pl.BlockSpec(shape, lambda *_: (0,) * len(shape))


def _chunk_of_tokens(c, n_tokens, first_token=0):
    return pl.ds(first_token * ROW_TILE + c, n_tokens, stride=ROW_TILE)


def _tile_of_token(token):
    if isinstance(token, int):
        return pl.ds(token * ROW_TILE, ROW_TILE)
    return pl.ds(pl.multiple_of(token * ROW_TILE, ROW_TILE), ROW_TILE)


def _rope_table_kernel(pos_ref, invf_ref, out_ref):
    ang = invf_ref[...] * pos_ref[...].astype(F32)
    out_ref[0:ROPE_HALF, :] = jnp.cos(ang)
    out_ref[ROPE_HALF:, :] = jnp.sin(ang)


def _rope_table(pos_row, inv_freq):
    t = pos_row.shape[1]
    return pl.pallas_call(
        _rope_table_kernel,
        grid=(t // ROPE_TN,),
        in_specs=[pl.BlockSpec((1, ROPE_TN), lambda i: (0, i)), _const_spec((ROPE_HALF, 1))],
        out_specs=pl.BlockSpec((2 * ROPE_HALF, ROPE_TN), lambda i: (0, i)),
        out_shape=jax.ShapeDtypeStruct((2 * ROPE_HALF, t), F32),
        name="rope_table",
    )(pos_row, inv_freq.reshape(ROPE_HALF, 1))


def _rope_expand_matrix():
    e = np.zeros((2 * ROPE_HALF, 2 * LANES), np.float32)
    ones = np.zeros((1, LANES), np.float32)
    for l in range(LANES):
        hl = l % HEAD_DIM
        if hl < ROPE_DIM:
            j = hl % ROPE_HALF
            e[j, l] = 1.0
            e[ROPE_HALF + j, LANES + l] = -1.0 if hl < ROPE_HALF else 1.0
        else:
            ones[0, l] = 1.0
    return e, ones


def _rmsnorm(x, g):
    ms = jnp.mean(x * x, axis=-1, keepdims=True)
    return x * lax.rsqrt(ms + RMS_EPS) * g


def _split3(x):
    a = x.astype(BF16)
    r = x - a.astype(F32)
    b = r.astype(BF16)
    c = (r - b.astype(F32)).astype(BF16)
    return a, b, c


def _mixer_kernel(sinks_ref, x_ref, cs_ref, erope_ref, ones_ref, g1_ref, win_ref, bin_ref, wab_ref,
                  pw_ref, ps_ref, wpb_ref, wout_ref, g2_ref, wr2_ref, brc_ref,
                  x1_ref, h2_ref, route_ref, cnt_ref,
                  kprev, vprev, uprev, q_scr, attn_scr, ga_scr, gp_scr, yat_scr, *, blocks_per_seq):
    i = pl.program_id(0)
    blk_in_seq = lax.rem(i, blocks_per_seq)
    seq_start = blk_in_seq == 0

    @pl.when(seq_start)
    def _():
        kprev[...] = jnp.zeros_like(kprev)
        vprev[...] = jnp.zeros_like(vprev)
        uprev[...] = jnp.zeros_like(uprev)

    x = x_ref[...]
    h = _rmsnorm(x, g1_ref[...]).astype(BF16)

    c1, c2, c3 = _split3(cs_ref[...])
    er = erope_ref[...]
    tab = (lax.dot_general(c1, er, TN_DIMS, preferred_element_type=F32)
           + lax.dot_general(c2, er, TN_DIMS, preferred_element_type=F32)
           + lax.dot_general(c3, er, TN_DIMS, preferred_element_type=F32))
    cos_t = tab[:, :LANES] + ones_ref[...]
    sin_t = tab[:, LANES:]
    lane = lax.broadcasted_iota(jnp.int32, (TM, LANES), 1)
    first_half = (lane & (HEAD_DIM - 1)) < ROPE_HALF

    def rope(t):
        partner = jnp.where(first_half, pltpu.roll(t, LANES - ROPE_HALF, 1), pltpu.roll(t, ROPE_HALF, 1))
        return t * cos_t + partner * sin_t

    q = jnp.dot(h, win_ref[:, 0:Q_WIDTH], preferred_element_type=F32) + bin_ref[:, 0:Q_WIDTH]
    for c in range(Q_WIDTH // LANES):
        qc = rope(q[:, c * LANES:(c + 1) * LANES]) * (HEAD_DIM ** -0.5 * LOG2_E)
        q_scr[:, c * LANES:(c + 1) * LANES] = qc.astype(BF16)

    kvu = jnp.dot(h, win_ref[:, O_K:O_GA], preferred_element_type=F32) + bin_ref[:, O_K:O_GA]
    k = rope(kvu[:, 0:KV_WIDTH])
    v = kvu[:, KV_WIDTH:2 * KV_WIDTH]
    u = kvu[:, 2 * KV_WIDTH:]

    lane_b = lax.broadcasted_iota(jnp.int32, (TM + WINDOW, LANES), 1)
    low = lane_b < HEAD_DIM

    def head_bands(prev_ref, cur):
        band = jnp.concatenate([prev_ref[...], cur], axis=0)
        swapped = pltpu.roll(band, HEAD_DIM, 1)
        zero = jnp.zeros_like(band)
        a0 = jnp.where(low, band, zero).astype(BF16)
        b0 = jnp.where(low, zero, swapped).astype(BF16)
        a1 = jnp.where(low, swapped, zero).astype(BF16)
        b1 = jnp.where(low, zero, band).astype(BF16)
        return ((a0, b0), (a1, b1))

    kb = head_bands(kprev, k)
    vb = head_bands(vprev, v)
    kprev[...] = k[TM - WINDOW:, :]
    vprev[...] = v[TM - WINDOW:, :]

    qi = lax.broadcasted_iota(jnp.int32, (WINDOW, 2 * WINDOW), 0)
    kj = lax.broadcasted_iota(jnp.int32, (WINDOW, 2 * WINDOW), 1)
    mask_mid = jnp.logical_or(jnp.logical_and(kj < WINDOW, kj > qi),
                              jnp.logical_and(kj >= WINDOW, (kj - WINDOW) <= qi))
    first_key = jnp.where(seq_start, WINDOW, 0)
    mask_first = jnp.logical_and(mask_mid, kj >= first_key)

    low_w = lax.broadcasted_iota(jnp.int32, (WINDOW, LANES), 1) < HEAD_DIM
    n_chunks = GROUP // 2
    ones_rows = lax.broadcasted_iota(jnp.int32, (4 * WINDOW, LANES), 0)
    ones_lanes = lax.broadcasted_iota(jnp.int32, (4 * WINDOW, LANES), 1)
    sum_cols = jnp.where((ones_rows < 2 * WINDOW) == (ones_lanes < HEAD_DIM), 1.0, 0.0).astype(BF16)

    pairs = [(j, hkv) for j in range(TM // WINDOW) for hkv in range(N_KV_HEADS)]

    def scores(j, hkv):
        rows = slice(j * WINDOW, (j + 1) * WINDOW)
        band_rows = slice(j * WINDOW, j * WINDOW + 2 * WINDOW)
        q4 = jnp.concatenate(
            [q_scr[rows, (hkv * n_chunks + c) * LANES:(hkv * n_chunks + c + 1) * LANES]
             for c in range(n_chunks)], axis=0)
        kk = jnp.concatenate([kb[hkv][0][band_rows], kb[hkv][1][band_rows]], axis=0)
        return lax.dot_general(q4, kk, NT_DIMS, preferred_element_type=F32)

    def attend(j, hkv, s):
        mask = mask_first if j == 0 else mask_mid
        rows = slice(j * WINDOW, (j + 1) * WINDOW)
        band_rows = slice(j * WINDOW, j * WINDOW + 2 * WINDOW)
        vv = jnp.concatenate([vb[hkv][0][band_rows], vb[hkv][1][band_rows]], axis=0)
        p_rows, sink_terms = [], []
        for c in range(n_chunks):
            p_cols, st = [], []
            for par in range(2):
                sink = sinks_ref[hkv * GROUP + 2 * c + par] * LOG2_E
                sc = s[c * WINDOW:(c + 1) * WINDOW, par * 2 * WINDOW:(par + 1) * 2 * WINDOW]
                sc = jnp.where(mask, sc, NEG_INF)
                m = jnp.maximum(jnp.max(sc, axis=-1, keepdims=True), sink)
                p_cols.append(jnp.exp2(sc - m).astype(BF16))
                st.append(jnp.exp2(sink - m))
            p_rows.append(jnp.concatenate(p_cols, axis=1))
            sink_terms.append(st)
        p = jnp.concatenate(p_rows, axis=0)
        o2 = jnp.dot(p, jnp.concatenate([vv, sum_cols], axis=1), preferred_element_type=F32)
        for c in range(n_chunks):
            den = o2[c * WINDOW:(c + 1) * WINDOW, LANES:] + jnp.where(low_w, sink_terms[c][0], sink_terms[c][1])
            oc = o2[c * WINDOW:(c + 1) * WINDOW, :LANES] / den
            col = (hkv * n_chunks + c) * LANES
            attn_scr[rows, col:col + LANES] = oc.astype(BF16)

    half = D_MODEL // 2

    def gate(scr, offset, part):
        cols = slice(offset + part * half, offset + (part + 1) * half)
        scr[:, part * half:(part + 1) * half] = jax.nn.sigmoid(
            jnp.dot(h, win_ref[:, cols], preferred_element_type=F32) + bin_ref[:, cols])

    def attn_branch(j):
        rows = slice(j * WINDOW, (j + 1) * WINDOW)
        yat_scr[rows, :] = jnp.dot(attn_scr[rows, :], wab_ref[...], preferred_element_type=F32)

    def pooling():
        ext = jnp.concatenate([uprev[...], u], axis=0)
        uprev[...] = u[TM - POOL_HALO:, :]
        pos_in_seq = blk_in_seq * TM + lax.broadcasted_iota(jnp.int32, (TM, 1), 0)
        mixed = []
        for g, w in enumerate(POOL_WINDOWS):
            cols = slice(g * POOL_GROUP_WIDTH, (g + 1) * POOL_GROUP_WIDTH)
            sg = ext[:, cols]
            step = 1
            while step < w:
                sg = sg + pltpu.roll(sg, step, 0)
                step *= 2
            count = jnp.minimum(pos_in_seq + 1, w).astype(F32)
            pooled = sg[POOL_HALO:, :] / count - u[:, cols]
            mg = jnp.dot(pooled.astype(BF16), pw_ref[g], preferred_element_type=F32)
            mixed.append((mg * ps_ref[:, cols]).astype(BF16))
        return jnp.dot(jnp.concatenate(mixed, axis=1), wpb_ref[...], preferred_element_type=F32)

    fillers = [functools.partial(gate, ga_scr, O_GA, 0), functools.partial(gate, ga_scr, O_GA, 1),
               functools.partial(gate, gp_scr, O_GP, 0), functools.partial(gate, gp_scr, O_GP, 1), pooling]
    n_row_blocks = TM // WINDOW
    branch_done = 0
    y_pool = None
    s_next = scores(*pairs[0])
    for n, (j, hkv) in enumerate(pairs):
        s_cur = s_next
        if n + 1 < len(pairs):
            s_next = scores(*pairs[n + 1])
        if n < len(fillers):
            out = fillers[n]()
            y_pool = out if out is not None else y_pool
        else:
            while branch_done < n_row_blocks and N_KV_HEADS * (branch_done + 1) <= n:
                attn_branch(branch_done)
                branch_done += 1
        attend(j, hkv, s_cur)
    for jb in range(branch_done, n_row_blocks):
        attn_branch(jb)

    hm = TM // 2
    e_iota = lax.broadcasted_iota(jnp.int32, (N_EXPERTS, hm), 0)

    def out_proj(part):
        rows = slice(part * hm, (part + 1) * hm)
        merged = (ga_scr[rows, :] * yat_scr[rows, :] + gp_scr[rows, :] * y_pool[rows, :]).astype(BF16)
        x1 = x[rows, :] + jnp.dot(merged, wout_ref[...], preferred_element_type=F32)
        x1_ref[rows, :] = x1
        return x1

    def route(part, x1):
        h2 = _rmsnorm(x1, g2_ref[...])
        for c in range(ROW_TILE):
            h2_ref[_chunk_of_tokens(c, hm, part * hm), :] = h2[:, c * LANES:(c + 1) * LANES]
        ha = h2.astype(BF16)
        hb = (h2 - ha.astype(F32)).astype(BF16)
        both = lax.dot_general(wr2_ref[...], ha, NT_DIMS, preferred_element_type=F32)
        logits = (both[:N_EXPERTS] + both[N_EXPERTS:]
                  + lax.dot_general(wr2_ref[0:N_EXPERTS, :], hb, NT_DIMS, preferred_element_type=F32)
                  + brc_ref[...])
        vals, idx, hot = [], [], []
        work = logits
        for _ in range(TOP_K):
            m = jnp.max(work, axis=0, keepdims=True)
            ix = jnp.min(jnp.where(work == m, e_iota, N_EXPERTS), axis=0, keepdims=True)
            is_hot = e_iota == ix
            vals.append(m)
            idx.append(ix)
            hot.append(jnp.where(is_hot, 1.0, 0.0))
            work = jnp.where(is_hot, -jnp.inf, work)
        ex = [jnp.exp(vk - vals[0]) for vk in vals]
        den = ex[0] + ex[1] + ex[2] + ex[3]
        return idx, [e / den for e in ex], hot

    x1_parts = [out_proj(0), out_proj(1)]
    routed = [route(part, x1_parts[part]) for part in range(2)]
    idxs = [jnp.concatenate([routed[0][0][k], routed[1][0][k]], axis=1) for k in range(TOP_K)]
    gates = [jnp.concatenate([routed[0][1][k], routed[1][1][k]], axis=1) for k in range(TOP_K)]
    hots = [jnp.concatenate([routed[0][2][k], routed[1][2][k]], axis=1) for k in range(TOP_K)]

    picked = (hots[0] + hots[1] + hots[2] + hots[3]).astype(BF16)
    t_row = lax.broadcasted_iota(jnp.int32, (TM, TM), 0)
    t_col = lax.broadcasted_iota(jnp.int32, (TM, TM), 1)
    earlier = jnp.where(t_row < t_col, 1.0, 0.0).astype(BF16)
    before = jnp.dot(picked, earlier, preferred_element_type=F32)
    cnt_ref[0] = jnp.dot(picked, jnp.ones((TM, LANES), BF16), preferred_element_type=F32).astype(jnp.int32)

    for kk_ in range(TOP_K):
        local_rank = jnp.sum(hots[kk_] * before, axis=0, keepdims=True)
        route_ref[kk_:kk_ + 1, :] = idxs[kk_].astype(F32)
        route_ref[TOP_K + kk_:TOP_K + kk_ + 1, :] = local_rank
        route_ref[2 * TOP_K + kk_:2 * TOP_K + kk_ + 1, :] = gates[kk_]
    route_ref[3 * TOP_K:, :] = jnp.zeros((ROUTE_ROWS - 3 * TOP_K, TM), F32)


def _mixer(x2d, cs, sinks, g1, w_in, b_in, w_ab, pool_w, pool_scale, w_pb, w_out, g2, w_r, b_r, seq_len):
    t = x2d.shape[0]
    erope, ones = _rope_expand_matrix()
    w_rh = w_r.astype(BF16)
    w_rl = (w_r - w_rh.astype(F32)).astype(BF16)
    w_r2 = jnp.concatenate([w_rh.T, w_rl.T], axis=0)
    row = lambda i: (i, 0)
    in_specs = [
        pl.BlockSpec(memory_space=pltpu.SMEM),
        pl.BlockSpec((TM, D_MODEL), row),
        pl.BlockSpec((2 * ROPE_HALF, TM), lambda i: (0, i)),
        _const_spec((2 * ROPE_HALF, 2 * LANES)),
        _const_spec((1, LANES)),
        _const_spec((1, D_MODEL)),
        _const_spec((D_MODEL, IN_WIDTH)),
        _const_spec((1, IN_WIDTH)),
        _const_spec((Q_WIDTH, D_MODEL)),
        _const_spec((len(POOL_WINDOWS), POOL_GROUP_WIDTH, POOL_GROUP_WIDTH)),
        _const_spec((1, POOL_WIDTH)),
        _const_spec((POOL_WIDTH, D_MODEL)),
        _const_spec((D_MODEL, D_MODEL)),
        _const_spec((1, D_MODEL)),
        _const_spec((2 * N_EXPERTS, D_MODEL)),
        _const_spec((N_EXPERTS, 1)),
    ]
    out_specs = [pl.BlockSpec((TM, D_MODEL), row), pl.BlockSpec((TM * ROW_TILE, LANES), row),
                 pl.BlockSpec((ROUTE_ROWS, TM), lambda i: (0, i)),
                 pl.BlockSpec((1, N_EXPERTS, LANES), lambda i: (i, 0, 0))]
    out_shape = [jax.ShapeDtypeStruct((t, D_MODEL), F32), jax.ShapeDtypeStruct((t * ROW_TILE, LANES), F32),
                 jax.ShapeDtypeStruct((ROUTE_ROWS, t), F32),
                 jax.ShapeDtypeStruct((t // TM, N_EXPERTS, LANES), jnp.int32)]
    scratch = [pltpu.VMEM((WINDOW, KV_WIDTH), F32), pltpu.VMEM((WINDOW, KV_WIDTH), F32),
               pltpu.VMEM((POOL_HALO, POOL_WIDTH), F32),
               pltpu.VMEM((TM, Q_WIDTH), BF16), pltpu.VMEM((TM, Q_WIDTH), BF16),
               pltpu.VMEM((TM, D_MODEL), F32), pltpu.VMEM((TM, D_MODEL), F32), pltpu.VMEM((TM, D_MODEL), F32)]
    return pl.pallas_call(
        functools.partial(_mixer_kernel, blocks_per_seq=seq_len // TM),
        grid=(t // TM,),
        in_specs=in_specs, out_specs=out_specs, out_shape=out_shape, scratch_shapes=scratch,
        compiler_params=pltpu.CompilerParams(dimension_semantics=("arbitrary",), vmem_limit_bytes=MIX_VMEM),
        name="mixer",
    )(sinks, x2d, cs, jnp.asarray(erope, BF16), jnp.asarray(ones), g1.reshape(1, -1), w_in.astype(BF16),
      b_in.reshape(1, -1), w_ab.astype(BF16), pool_w.astype(BF16), pool_scale.reshape(1, -1),
      w_pb.astype(BF16), w_out.astype(BF16), g2.reshape(1, -1), w_r2, b_r.reshape(-1, 1))


def _tile_copy(src, src_token, dst, dst_token, sem):
    return pltpu.make_async_copy(src.at[_tile_of_token(src_token)], dst.at[_tile_of_token(dst_token)], sem)


def _dispatch_kernel(dest_ref, pad_ref, h2_ref, xs_ref, zeros, sem, pad_sem, *, runs_per_step):
    i = pl.program_id(0)

    @pl.when(i == 0)
    def _():
        zeros[...] = jnp.zeros_like(zeros)

    def zero_fill(first_slot, n_slots):
        return pltpu.make_async_copy(zeros.at[pl.ds(0, n_slots * ROW_TILE)],
                                     xs_ref.at[pl.ds(pl.multiple_of(first_slot * ROW_TILE, ROW_TILE),
                                                     n_slots * ROW_TILE)], pad_sem)

    pad_copies = []
    for j in range(runs_per_step):
        run = i * runs_per_step + j
        run_c = jnp.minimum(run, N_EXPERTS - 1)
        first, length = pad_ref[0, run_c], pad_ref[1, run_c]
        piece = BM // 2
        while piece >= 1:
            wanted = jnp.logical_and(run < N_EXPERTS, (length & piece) != 0)
            pad_copies.append((wanted, zero_fill(first, piece)))
            first = first + (length & piece)
            piece //= 2
        pad_copies.append((run < pad_ref[2, 1], zero_fill(pad_ref[2, 0] + run * BM, BM)))

    copies = [_tile_copy(h2_ref, r, xs_ref, dest_ref[k, r], sem) for r in range(TD) for k in range(TOP_K)]
    for a, cp in enumerate(copies):
        cp.start(priority=a % 2)
    for wanted, cp in pad_copies:
        pl.when(wanted)(cp.start)
    for cp in copies:
        cp.wait()
    for wanted, cp in pad_copies:
        pl.when(wanted)(cp.wait)


def _dispatch(h2, dest_t, pad_info, n_slots):
    t = h2.shape[0] // ROW_TILE
    n_steps = t // TD
    runs_per_step = -(-N_EXPERTS // n_steps)
    return pl.pallas_call(
        functools.partial(_dispatch_kernel, runs_per_step=runs_per_step),
        grid=(n_steps,),
        in_specs=[pl.BlockSpec((TOP_K, TD), lambda i: (0, i), memory_space=pltpu.SMEM),
                  pl.BlockSpec(memory_space=pltpu.SMEM),
                  pl.BlockSpec((TD * ROW_TILE, LANES), lambda i: (i, 0))],
        out_specs=pl.BlockSpec(memory_space=pl.ANY),
        out_shape=jax.ShapeDtypeStruct((n_slots * ROW_TILE, LANES), F32),
        scratch_shapes=[pltpu.VMEM((BM * ROW_TILE, LANES), F32), pltpu.SemaphoreType.DMA(()),
                        pltpu.SemaphoreType.DMA(())],
        compiler_params=pltpu.CompilerParams(dimension_semantics=("arbitrary",), has_side_effects=True),
        name="dispatch",
    )(dest_t, pad_info, h2)


def _experts_kernel(be_ref, first_ref, rows_ref, next_ref, par_ref, xs_ref, bup_ref, bdn_ref, wup_hbm, wdn_hbm,
                    ys_ref, wup_f32, wdn_f32, wup_bf, wdn_bf, sems):
    for j in range(STEP_BLOCKS):
        _expert_block(pl.program_id(0) * STEP_BLOCKS + j, j, be_ref, first_ref, rows_ref, next_ref, par_ref, xs_ref,
                      bup_ref, bdn_ref, wup_hbm, wdn_hbm, ys_ref, wup_f32, wdn_f32, wup_bf, wdn_bf, sems)


def _expert_block(i, j, be_ref, first_ref, rows_ref, next_ref, par_ref, xs_ref, bup_ref, bdn_ref, wup_hbm, wdn_hbm,
                  ys_ref, wup_f32, wdn_f32, wup_bf, wdn_bf, sems):
    cast_rows = 128
    n_rows = rows_ref[i]
    expert = be_ref[i]
    first_token = j * BM

    def weight_copies(expert, par):
        return (pltpu.make_async_copy(wup_hbm.at[expert], wup_f32.at[par], sems.at[0, par]),
                pltpu.make_async_copy(wdn_hbm.at[expert], wdn_f32.at[par], sems.at[1, par]))

    @pl.when(first_ref[i] == 1)
    def _():
        par, nxt = par_ref[i], next_ref[i]

        @pl.when(i == 0)
        def _():
            for cp in weight_copies(expert, par):
                cp.start()

        for cp in weight_copies(expert, par):
            cp.wait()

        @pl.when(nxt >= 0)
        def _():
            for cp in weight_copies(nxt, 1 - par):
                cp.start(priority=1)

        def cast_up(r, carry):
            rows = pl.ds(pl.multiple_of(r * cast_rows, cast_rows), cast_rows)
            wup_bf[rows, :] = wup_f32[par, rows, :].astype(BF16)
            return carry

        def cast_dn(r, carry):
            rows = pl.ds(pl.multiple_of(r * cast_rows, cast_rows), cast_rows)
            wdn_bf[rows, :] = wdn_f32[par, rows, :].astype(BF16)
            return carry

        lax.fori_loop(0, D_MODEL // cast_rows, cast_up, 0)
        lax.fori_loop(0, D_FF // cast_rows, cast_dn, 0)

    @pl.when(n_rows > 0)
    def _():
        xb = jnp.concatenate([xs_ref[_chunk_of_tokens(c, BM, first_token), :].astype(BF16)
                              for c in range(ROW_TILE)], axis=1)
        b_up = bup_ref[pl.ds(expert, 1), :]
        b_dn = bdn_ref[pl.ds(expert, 1), :]
        up = jnp.dot(xb, wup_bf[...], preferred_element_type=F32) + b_up
        glu = jnp.minimum(up[:, :D_FF], SWIGLU_LIMIT)
        lin = jnp.clip(up[:, D_FF:], -SWIGLU_LIMIT, SWIGLU_LIMIT)
        act = (glu * jax.nn.sigmoid(SWIGLU_ALPHA * glu) * (lin + 1.0)).astype(BF16)
        chunks_per_tile = OUT_TILE // LANES
        for n in range(D_MODEL // OUT_TILE):
            cols = slice(n * OUT_TILE, (n + 1) * OUT_TILE)
            y = jnp.dot(act, wdn_bf[:, cols], preferred_element_type=F32) + b_dn[:, cols]
            for c in range(chunks_per_tile):
                ys_ref[_chunk_of_tokens(n * chunks_per_tile + c, BM, first_token), :] = (
                    y[:, c * LANES:(c + 1) * LANES])

    @pl.when(n_rows == 0)
    def _():
        ys_ref[pl.ds(first_token * ROW_TILE, BM * ROW_TILE), :] = jnp.zeros((BM * ROW_TILE, LANES), F32)


def _experts(xs, blk_expert, blk_first, blk_rows, blk_next, blk_par, w_up, b_up, w_down, b_down):
    n_blocks = xs.shape[0] // (ROW_TILE * BM)
    assert n_blocks % STEP_BLOCKS == 0
    slot_blk = lambda i, *_: (i, 0)
    grid_spec = pltpu.PrefetchScalarGridSpec(
        num_scalar_prefetch=5,
        grid=(n_blocks // STEP_BLOCKS,),
        in_specs=[
            pl.BlockSpec((STEP_BLOCKS * BM * ROW_TILE, LANES), slot_blk),
            _const_spec((N_EXPERTS, 2 * D_FF)),
            _const_spec((N_EXPERTS, D_MODEL)),
            pl.BlockSpec(memory_space=pl.ANY),
            pl.BlockSpec(memory_space=pl.ANY),
        ],
        out_specs=pl.BlockSpec((STEP_BLOCKS * BM * ROW_TILE, LANES), slot_blk),
        scratch_shapes=[pltpu.VMEM((2, D_MODEL, 2 * D_FF), F32), pltpu.VMEM((2, D_FF, D_MODEL), F32),
                        pltpu.VMEM((D_MODEL, 2 * D_FF), BF16), pltpu.VMEM((D_FF, D_MODEL), BF16),
                        pltpu.SemaphoreType.DMA((2, 2))],
    )
    return pl.pallas_call(
        _experts_kernel,
        grid_spec=grid_spec,
        out_shape=jax.ShapeDtypeStruct(xs.shape, F32),
        compiler_params=pltpu.CompilerParams(dimension_semantics=("arbitrary",), vmem_limit_bytes=MOE_VMEM),
        name="experts",
    )(blk_expert, blk_first, blk_rows, blk_next, blk_par, xs, b_up, b_down, w_up, w_down)


def _combine_kernel(dest_ref, dest_next_ref, x1_ref, gate_ref, gf_ref, ys_ref, out_ref, buf, sems):
    i = pl.program_id(0)
    slot = lax.rem(i, 2)

    def gather(d_ref, s):
        return [_tile_copy(ys_ref, d_ref[k, r], buf.at[s, k], r, sems.at[s])
                for r in range(TB) for k in range(TOP_K)]

    def start(d_ref, s):
        for a, cp in enumerate(gather(d_ref, s)):
            cp.start(priority=a % 2)

    @pl.when(i == 0)
    def _():
        start(dest_ref, slot)

    @pl.when(i + 1 < pl.num_programs(0))
    def _():
        start(dest_next_ref, 1 - slot)

    for cp in gather(dest_ref, slot):
        cp.wait()

    cols = []
    for c in range(ROW_TILE):
        acc = x1_ref[:, c * LANES:(c + 1) * LANES]
        for k in range(TOP_K):
            acc = acc + gate_ref[:, k:k + 1] * buf[slot, k, _chunk_of_tokens(c, TB), :]
        cols.append(acc)
    out_ref[...] = _rmsnorm(jnp.concatenate(cols, axis=1), gf_ref[...])


def _combine(x1, gates, ys, dest_t, gf):
    t = x1.shape[0]
    n_blk = t // TB
    dest_ahead = lambda ahead: pl.BlockSpec((TOP_K, TB), lambda i: (0, jnp.minimum(i + ahead, n_blk - 1)),
                                            memory_space=pltpu.SMEM)
    return pl.pallas_call(
        _combine_kernel,
        grid=(n_blk,),
        in_specs=[dest_ahead(0), dest_ahead(1),
                  pl.BlockSpec((TB, D_MODEL), lambda i: (i, 0)),
                  pl.BlockSpec((TB, TOP_K), lambda i: (i, 0)),
                  _const_spec((1, D_MODEL)),
                  pl.BlockSpec(memory_space=pl.ANY)],
        out_specs=pl.BlockSpec((TB, D_MODEL), lambda i: (i, 0)),
        out_shape=jax.ShapeDtypeStruct((t, D_MODEL), F32),
        scratch_shapes=[pltpu.VMEM((2, TOP_K, TB * ROW_TILE, LANES), F32), pltpu.SemaphoreType.DMA((2,))],
        compiler_params=pltpu.CompilerParams(dimension_semantics=("arbitrary",), vmem_limit_bytes=COMBINE_VMEM),
        name="combine",
    )(dest_t, dest_t, x1, gates, gf.reshape(1, -1), ys)


def _pick(member, values):
    return jnp.sum(jnp.where(member, values[:, None], 0), axis=0, dtype=jnp.int32)


def _routing(top_idx_t, local_rank_t, blk_counts):
    t = top_idx_t.shape[1]
    n_slots = t * TOP_K + N_EXPERTS * BM
    n_blocks = n_slots // BM
    ids = jnp.arange(N_EXPERTS, dtype=jnp.int32)
    counts = jnp.sum(blk_counts, axis=0)
    padded = (counts + BM - 1) // BM * BM
    pend = jnp.cumsum(padded)
    pstart = pend - padded
    base = pstart[None, :] + jnp.cumsum(blk_counts, axis=0) - blk_counts
    base_tok_t = jnp.repeat(base.T, TM, axis=1)
    dest_t = local_rank_t + jnp.sum(
        jnp.where(top_idx_t[:, None, :] == ids[None, :, None], base_tok_t[None], 0), axis=1)

    blk_start = jnp.arange(n_blocks, dtype=jnp.int32) * BM
    member = (pstart[:, None] <= blk_start[None, :]) & (blk_start[None, :] < pend[:, None])
    blk_expert = _pick(member, ids)
    blk_rows = jnp.sum(jnp.where(member, jnp.clip((pstart + counts)[:, None] - blk_start[None, :], 0, BM), 0),
                       axis=0, dtype=jnp.int32)
    blk_first = _pick(member & (blk_start[None, :] == pstart[:, None]), jnp.ones_like(ids))
    used = (counts > 0).astype(jnp.int32)
    blk_par = _pick(member, (jnp.cumsum(used) - used) % 2)
    later_used = (used > 0)[None, :] & (ids[None, :] > ids[:, None])
    next_used = jnp.min(jnp.where(later_used, ids[None, :], N_EXPERTS), axis=1)
    blk_next = _pick(member, jnp.where(next_used == N_EXPERTS, -1, next_used))

    tail = jnp.zeros((N_EXPERTS,), jnp.int32).at[0].set(pend[-1]).at[1].set((n_slots - pend[-1]) // BM)
    pad_info = jnp.stack([pstart + counts, padded - counts, tail]).astype(jnp.int32)
    return (dest_t.astype(jnp.int32), pad_info, blk_expert, blk_first, blk_rows, blk_next, blk_par, n_slots)


def kernel(x, positions, norm1_g, w_in, b_in, attn_sinks, w_attn_br, pool_w, pool_scale, w_pool_br, w_out,
           norm2_g, w_router, b_router, w_up, b_up, w_down, b_down, normf_g):
    b, s, d = x.shape
    t = b * s
    assert norm1_g.shape[0] == 1, "single-layer problem: the combine kernel applies the final RMSNorm"
    assert d == D_MODEL and s % TM == 0 and t % ROPE_TN == 0 and t % TD == 0 and t % TB == 0
    inv_freq = jnp.power(jnp.float32(ROPE_THETA), -jnp.arange(ROPE_HALF, dtype=F32) * (2.0 / ROPE_DIM))
    cs = _rope_table(positions.reshape(1, t), inv_freq)
    x1, h2, route, blk_counts = _mixer(
        x.reshape(t, d), cs, attn_sinks[0], norm1_g[0], w_in[0], b_in[0], w_attn_br[0], pool_w[0],
        pool_scale[0], w_pool_br[0], w_out[0], norm2_g[0], w_router[0], b_router[0], s)
    top_idx_t = route[0:TOP_K].astype(jnp.int32)
    local_rank_t = route[TOP_K:2 * TOP_K].astype(jnp.int32)
    gates = route[2 * TOP_K:3 * TOP_K].T
    dest_t, pad_info, blk_expert, blk_first, blk_rows, blk_next, blk_par, n_slots = _routing(
        top_idx_t, local_rank_t, blk_counts[:, :, 0])
    xs = _dispatch(h2, dest_t, pad_info, n_slots)
    ys = _experts(xs, blk_expert, blk_first, blk_rows, blk_next, blk_par, w_up[0], b_up[0], w_down[0], b_down[0])
    out = _combine(x1, gates, ys, dest_t, normf_g)
    return out.reshape(b, s, d)
```
